```python
import jax, jax.numpy as jnp
from jax import lax
import numpy as np

D_MODEL = 1024
BATCH = 8
SEQ = 2048
DEPTH = 2
DEC_BATCH = 32
DEC_SEQ = 4
PAST_LEN = 8192
PAGE_SIZE = 128

RET_HEADS = 4
RET_DK = D_MODEL // RET_HEADS
RET_DV = 2 * RET_DK
RET_CHUNK = 128
ROPE_BASE = 10000.0
FOX_HEADS = 16
FOX_HD = D_MODEL // FOX_HEADS
Q_BLOCK = 128
N_EXPERTS = 32
TOP_K = 4
D_FF = D_MODEL
SWIGLU_LIMIT = 7.0
SWIGLU_ALPHA = 1.702
NORM_EPS = 1e-6
N_RET = (DEPTH + 1) // 2
N_FOX = DEPTH // 2

kernel_name = 'hybrid_retention_fox_moe_step'

F32 = jnp.float32


def rms_norm(x, g):
    xf = x.astype(F32)
    y = xf * lax.rsqrt(jnp.mean(xf * xf, axis=-1, keepdims=True) + NORM_EPS)
    return (y * g.astype(F32)).astype(x.dtype)


def adaln(c, w, b):
    m = jax.nn.silu(c) @ w + b
    return jnp.split(m[:, None, :], 6, axis=-1)


def rotary(x, pos):
    half = x.shape[-1] // 2
    inv = ROPE_BASE ** (-jnp.arange(half, dtype=F32) / half)
    ang = pos.astype(F32)[:, None] * inv[None, :]
    cos = jnp.cos(ang)[None, :, None, :]
    sin = jnp.sin(ang)[None, :, None, :]
    xf = x.astype(F32)
    x1, x2 = xf[..., :half], xf[..., half:]
    return jnp.concatenate([x1 * cos - x2 * sin, x1 * sin + x2 * cos], axis=-1).astype(x.dtype)


def ret_log_gamma():
    return jnp.log(1.0 - 2.0 ** (-5.0 - jnp.arange(RET_HEADS, dtype=F32)))


def retention_scan(q, k, v, s0, log_gamma):
    B, L, H, _ = q.shape
    C = RET_CHUNK if L % RET_CHUNK == 0 else L
    n = L // C
    idx = jnp.arange(C, dtype=F32)
    rel = idx[:, None] - idx[None, :]
    lg = log_gamma.astype(F32)
    decay_in = jnp.where(rel[None] >= 0, jnp.exp(lg[:, None, None] * jnp.maximum(rel, 0.0)[None]), 0.0)
    decay_q = jnp.exp(lg[:, None] * (idx[None, :] + 1.0))
    decay_k = jnp.exp(lg[:, None] * (C - 1.0 - idx[None, :]))
    decay_c = jnp.exp(lg * C)

    def to_chunks(t):
        return t.reshape(B, n, C, H, t.shape[-1]).transpose(1, 0, 3, 2, 4)

    def step(s, qkv):
        qc, kc, vc = qkv
        qf, kf, vf = qc.astype(F32), kc.astype(F32), vc.astype(F32)
        inner = jnp.einsum('bhid,bhjd->bhij', qf, kf) * decay_in
        o = jnp.einsum('bhij,bhje->bhie', inner, vf) + jnp.einsum('bhid,bhde->bhie', qf, s) * decay_q[..., None]
        s = s * decay_c[:, None, None] + jnp.einsum('bhjd,bhje->bhde', kf * decay_k[..., None], vf)
        return s, o

    s, o = lax.scan(step, s0.astype(F32), (to_chunks(q), to_chunks(k), to_chunks(v)))
    o = o.transpose(1, 0, 3, 2, 4).reshape(B, L, H, v.shape[-1])
    return o.astype(q.dtype), s.astype(s0.dtype)


def retention_mixer(h, s0, pos, w_in, gn_gain, w_out, log_gamma):
    B, L, _ = h.shape
    hk, hv = RET_HEADS * RET_DK, RET_HEADS * RET_DV
    q, k, v, g = jnp.split(h @ w_in, [hk, 2 * hk, 2 * hk + hv], axis=-1)
    q = rotary(q.reshape(B, L, RET_HEADS, RET_DK), pos) * (RET_DK ** -0.5)
    k = rotary(k.reshape(B, L, RET_HEADS, RET_DK), pos)
    v = v.reshape(B, L, RET_HEADS, RET_DV)
    o, s = retention_scan(q, k, v, s0, log_gamma)
    o = rms_norm(o, gn_gain.reshape(RET_HEADS, RET_DV)).reshape(B, L, hv)
    return (o * jax.nn.silu(g)) @ w_out, s


def fox_project(h, w_in, b_f):
    B, L, _ = h.shape
    fw = FOX_HEADS * FOX_HD
    q, k, v, g, fl = jnp.split(h @ w_in, [fw, 2 * fw, 3 * fw, 4 * fw], axis=-1)
    shp = (B, L, FOX_HEADS, FOX_HD)
    logf = jax.nn.log_sigmoid(fl.astype(F32) + b_f.astype(F32))
    return q.reshape(shp), k.reshape(shp), v.reshape(shp), g, logf


def fox_attn_prompt(q, k, v, logf):
    B, L, H, HD = q.shape
    cum = jnp.cumsum(logf, axis=1)
    ck = cum.transpose(0, 2, 1)[:, :, None, :]
    kpos = jnp.arange(L)
    scale = HD ** -0.5

    def block(bi):
        start = bi * Q_BLOCK
        qb = lax.dynamic_slice_in_dim(q, start, Q_BLOCK, axis=1)
        cq = lax.dynamic_slice_in_dim(cum, start, Q_BLOCK, axis=1).transpose(0, 2, 1)[..., None]
        s = jnp.einsum('bqhd,bkhd->bhqk', qb, k).astype(F32) * scale + (cq - ck)
        qpos = start + jnp.arange(Q_BLOCK)
        s = jnp.where(kpos[None, :] <= qpos[:, None], s, -jnp.inf)
        p = jax.nn.softmax(s, axis=-1).astype(v.dtype)
        return jnp.einsum('bhqk,bkhd->bqhd', p, v)

    o = lax.map(block, jnp.arange(L // Q_BLOCK))
    return o.transpose(1, 0, 2, 3, 4).reshape(B, L, H, HD)


def fox_attn_sample(q, k, v, logf, k_cache, v_cache, f_cache, page_table):
    DB, T, H, HD = q.shape
    kp = k_cache[page_table].reshape(DB, -1, H, HD)
    vp = v_cache[page_table].reshape(DB, -1, H, HD)
    fp = f_cache[page_table].reshape(DB, -1, H).astype(F32)
    P = kp.shape[1]
    suffix = lax.cumsum(fp, axis=1, reverse=True) - fp
    cn = jnp.cumsum(logf, axis=1).transpose(0, 2, 1)
    scale = HD ** -0.5
    s_past = (jnp.einsum('bthd,bshd->bhts', q, kp).astype(F32) * scale
              + cn[..., None] + suffix.transpose(0, 2, 1)[:, :, None, :])
    s_new = jnp.einsum('bthd,bshd->bhts', q, k).astype(F32) * scale + cn[..., :, None] - cn[..., None, :]
    causal = jnp.arange(T)[None, :] <= jnp.arange(T)[:, None]
    s_new = jnp.where(causal, s_new, -jnp.inf)
    p = jax.nn.softmax(jnp.concatenate([s_past, s_new], axis=-1), axis=-1).astype(v.dtype)
    return (jnp.einsum('bhts,bshd->bthd', p[..., :P], vp)
            + jnp.einsum('bhts,bshd->bthd', p[..., P:], v))


def fox_out(o, g, w_out):
    B, L = o.shape[:2]
    return (o.reshape(B, L, FOX_HEADS * FOX_HD) * jax.nn.sigmoid(g)) @ w_out


def moe(h, w_r, b_r, w_gu, b_gu, w_dn, b_dn):
    B, L, D = h.shape
    t = h.reshape(B * L, D)
    logits = (t @ w_r).astype(F32) + b_r.astype(F32)
    top_v, top_i = lax.top_k(logits, TOP_K)
    top_w = jax.nn.softmax(top_v, axis=-1)
    gates = jnp.sum(jax.nn.one_hot(top_i, N_EXPERTS, dtype=F32) * top_w[..., None], axis=1).astype(h.dtype)

    def expert(acc, p):
        wgu, bgu, wdn, bdn, g = p
        gate, up = jnp.split(t @ wgu + bgu, 2, axis=-1)
        gate = jnp.minimum(gate, SWIGLU_LIMIT)
        up = jnp.clip(up, -SWIGLU_LIMIT, SWIGLU_LIMIT)
        y = ((up + 1.0) * gate * jax.nn.sigmoid(SWIGLU_ALPHA * gate)) @ wdn + bdn
        return acc + g[:, None] * y, None

    acc, _ = lax.scan(expert, jnp.zeros_like(t), (w_gu, b_gu, w_dn, b_dn, gates.T))
    return acc.reshape(B, L, D)


def setup_inputs(seed: int = 0) -> dict:
    key = jax.random.key(seed)
    ks = iter(jax.random.split(key, 32))
    n_pages = PAST_LEN // PAGE_SIZE
    n_used = DEC_BATCH * n_pages
    n_pool = n_used + max(1, n_used // 4)
    D = D_MODEL
    fw = FOX_HEADS * FOX_HD

    def nrm(shape, scale):
        return jax.random.normal(next(ks), shape, F32) * scale

    inp = {}
    inp['x_prompt'] = nrm((BATCH, SEQ, D), 1.0)
    inp['x_sample'] = nrm((DEC_BATCH, DEC_SEQ, D), 1.0)
    inp['c_prompt'] = nrm((BATCH, D), 1.0)
    inp['c_sample'] = nrm((DEC_BATCH, D), 1.0)
    inp['state_ret'] = nrm((N_RET, DEC_BATCH, RET_HEADS, RET_DK, RET_DV), 1.0)
    inp['cache_fox_k'] = nrm((N_FOX, n_pool, PAGE_SIZE, FOX_HEADS, FOX_HD), 1.0)
    inp['cache_fox_v'] = nrm((N_FOX, n_pool, PAGE_SIZE, FOX_HEADS, FOX_HD), 1.0)
    inp['cache_fox_logf'] = jax.nn.log_sigmoid(nrm((N_FOX, n_pool, PAGE_SIZE, FOX_HEADS), 1.0) + 4.0)
    inp['page_table'] = jax.random.permutation(next(ks), n_pool)[:n_used].reshape(DEC_BATCH, n_pages).astype(jnp.int32)
    inp['w_ada'] = nrm((DEPTH, D, 6 * D), 0.5 * D ** -0.5)
    inp['b_ada'] = nrm((DEPTH, 6 * D), 0.02)
    inp['norm_pre_mix'] = 1.0 + nrm((DEPTH, D), 0.02)
    inp['norm_post_mix'] = 1.0 + nrm((DEPTH, D), 0.02)
    inp['norm_pre_ffn'] = 1.0 + nrm((DEPTH, D), 0.02)
    inp['norm_post_ffn'] = 1.0 + nrm((DEPTH, D), 0.02)
    inp['ret_w_in'] = nrm((N_RET, D, 2 * RET_HEADS * RET_DK + 2 * RET_HEADS * RET_DV), D ** -0.5)
    inp['ret_gn_gain'] = 1.0 + nrm((N_RET, RET_HEADS * RET_DV), 0.02)
    inp['ret_w_out'] = nrm((N_RET, RET_HEADS * RET_DV, D), (RET_HEADS * RET_DV) ** -0.5)
    inp['fox_w_in'] = nrm((N_FOX, D, 4 * fw + FOX_HEADS), D ** -0.5)
    inp['fox_b_f'] = jax.random.uniform(next(ks), (N_FOX, FOX_HEADS), F32, 1.0, 6.0)
    inp['fox_w_out'] = nrm((N_FOX, fw, D), fw ** -0.5)
    inp['moe_w_router'] = nrm((DEPTH, D, N_EXPERTS), D ** -0.5)
    inp['moe_b_router'] = nrm((DEPTH, N_EXPERTS), 0.01)
    inp['moe_w_gate_up'] = nrm((DEPTH, N_EXPERTS, D, 2 * D_FF), D ** -0.5)
    inp['moe_b_gate_up'] = nrm((DEPTH, N_EXPERTS, 2 * D_FF), 0.02)
    inp['moe_w_down'] = nrm((DEPTH, N_EXPERTS, D_FF, D), D_FF ** -0.5)
    inp['moe_b_down'] = nrm((DEPTH, N_EXPERTS, D), 0.02)
    return inp


def reference(x_prompt, x_sample, c_prompt, c_sample, state_ret, cache_fox_k, cache_fox_v, cache_fox_logf,
              page_table, w_ada, b_ada, norm_pre_mix, norm_post_mix, norm_pre_ffn, norm_post_ffn,
              ret_w_in, ret_gn_gain, ret_w_out, fox_w_in, fox_b_f, fox_w_out,
              moe_w_router, moe_b_router, moe_w_gate_up, moe_b_gate_up, moe_w_down, moe_b_down):
    log_gamma = ret_log_gamma()
    pos_p = jnp.arange(x_prompt.shape[1])
    pos_s = PAST_LEN + jnp.arange(x_sample.shape[1])
    xp, xs = x_prompt, x_sample
    ret_p, ret_s = [], []
    kp_l, vp_l, fp_l, ks_l, vs_l, fs_l = [], [], [], [], [], []
    for i in range(DEPTH):
        mp = adaln(c_prompt, w_ada[i], b_ada[i])
        ms = adaln(c_sample, w_ada[i], b_ada[i])
        hp = rms_norm(xp, norm_pre_mix[i]) * (1.0 + mp[1]) + mp[0]
        hs = rms_norm(xs, norm_pre_mix[i]) * (1.0 + ms[1]) + ms[0]
        j = i // 2
        if i % 2 == 0:
            s0p = jnp.zeros((xp.shape[0], RET_HEADS, RET_DK, RET_DV), xp.dtype)
            op, stp = retention_mixer(hp, s0p, pos_p, ret_w_in[j], ret_gn_gain[j], ret_w_out[j], log_gamma)
            os_, sts = retention_mixer(hs, state_ret[j], pos_s, ret_w_in[j], ret_gn_gain[j], ret_w_out[j], log_gamma)
            ret_p.append(stp)
            ret_s.append(sts)
        else:
            q, k, v, g, lf = fox_project(hp, fox_w_in[j], fox_b_f[j])
            op = fox_out(fox_attn_prompt(q, k, v, lf), g, fox_w_out[j])
            kp_l.append(k)
            vp_l.append(v)
            fp_l.append(lf.astype(k.dtype))
            q, k, v, g, lf = fox_project(hs, fox_w_in[j], fox_b_f[j])
            os_ = fox_out(fox_attn_sample(q, k, v, lf, cache_fox_k[j], cache_fox_v[j], cache_fox_logf[j], page_table),
                          g, fox_w_out[j])
            ks_l.append(k)
            vs_l.append(v)
            fs_l.append(lf.astype(k.dtype))
        xp = xp + mp[2] * rms_norm(op, norm_post_mix[i])
        xs = xs + ms[2] * rms_norm(os_, norm_post_mix[i])
        hp = rms_norm(xp, norm_pre_ffn[i]) * (1.0 + mp[4]) + mp[3]
        hs = rms_norm(xs, norm_pre_ffn[i]) * (1.0 + ms[4]) + ms[3]
        fp_ = moe(hp, moe_w_router[i], moe_b_router[i], moe_w_gate_up[i], moe_b_gate_up[i], moe_w_down[i], moe_b_down[i])
        fs_ = moe(hs, moe_w_router[i], moe_b_router[i], moe_w_gate_up[i], moe_b_gate_up[i], moe_w_down[i], moe_b_down[i])
        xp = xp + mp[5] * rms_norm(fp_, norm_post_ffn[i])
        xs = xs + ms[5] * rms_norm(fs_, norm_post_ffn[i])
    return (xp, xs, jnp.stack(ret_p), jnp.stack(ret_s), jnp.stack(kp_l), jnp.stack(vp_l), jnp.stack(fp_l),
            jnp.stack(ks_l), jnp.stack(vs_l), jnp.stack(fs_l))
```

```python
import functools
import math

import jax
import jax.numpy as jnp
from jax import lax
from jax.experimental import pallas as pl
from jax.experimental.pallas import tpu as pltpu

F32 = jnp.float32
BF16 = jnp.bfloat16
I32 = jnp.int32
HI = lax.Precision.HIGHEST

D_MODEL = 1024
PAST_LEN = 8192
PAGE_SIZE = 128
RET_HEADS = 4
RET_DK = D_MODEL // RET_HEADS
RET_DV = 2 * RET_DK
RET_CHUNK = 128
ROPE_BASE = 10000.0
FOX_HEADS = 16
FOX_HD = D_MODEL // FOX_HEADS
N_EXPERTS = 32
TOP_K = 4
D_FF = D_MODEL
SWIGLU_LIMIT = 7.0
SWIGLU_ALPHA = 1.702
NORM_EPS = 1e-6

LANES = 128
SUBLANES = 8
ROW_TILES = D_MODEL // LANES
TOK_PAD = SUBLANES
TM = 256
TM_FFN = 256
TM_COMB = 128
TQ = 512
PAGES_PER_STEP = 4
VMEM_LIMIT = 56 * 1024 * 1024


def _cparams(sem, vmem=VMEM_LIMIT):
    return pltpu.CompilerParams(dimension_semantics=sem, vmem_limit_bytes=vmem)


def _rms(x, gain):
    return x * lax.rsqrt(jnp.mean(x * x, axis=-1, keepdims=True) + NORM_EPS) * gain


def _dot(a, b, precision=None):
    return jnp.dot(a, b, precision=precision, preferred_element_type=F32)


def _dot_nt(a, b):
    return lax.dot_general(a, b, (((1,), (1,)), ((), ())), preferred_element_type=F32)


def _dot_tn(a, b):
    return lax.dot_general(a, b, (((0,), (0,)), ((), ())), preferred_element_type=F32)


def _load_rows(ref, rows):
    return jnp.concatenate([ref[pl.ds(s, rows, stride=ROW_TILES), :] for s in range(ROW_TILES)], axis=1)


def _store_rows(ref, val, rows):
    for s in range(ROW_TILES):
        ref[pl.ds(s, rows, stride=ROW_TILES), :] = val[:, s * LANES:(s + 1) * LANES]


def _ada_body(c_ref, w_ref, b_ref, o_ref):
    c = c_ref[...]
    a = c * jax.nn.sigmoid(c)
    o_ref[0] = _dot(a, w_ref[0], HI) + b_ref[0]


def _adaln(c_all, w_ada, b_ada):
    depth, d, n = w_ada.shape
    nb = c_all.shape[0]
    tn = 768
    return pl.pallas_call(
        _ada_body,
        grid=(depth, n // tn),
        in_specs=[pl.BlockSpec((nb, d), lambda l, j: (0, 0)),
                  pl.BlockSpec((1, d, tn), lambda l, j: (l, 0, j)),
                  pl.BlockSpec((1, 1, tn), lambda l, j: (l, 0, j))],
        out_specs=pl.BlockSpec((1, nb, tn), lambda l, j: (l, 0, j)),
        out_shape=jax.ShapeDtypeStruct((depth, nb, n), F32),
        compiler_params=_cparams(("arbitrary", "arbitrary")),
        name="adaln",
    )(c_all, w_ada, b_ada.reshape(depth, 1, n))


def _mod_spec(mods, tiles_per_seq):
    r = mods.shape[2]
    if r == 1:
        return pl.BlockSpec((6, 1, 1, D_MODEL), lambda i: (0, i // tiles_per_seq, 0, 0))
    return pl.BlockSpec((6, 1, r, D_MODEL), lambda i: (0, 0, 0, 0))


def _inproj_ret_body(x_ref, m_ref, g_ref, w_ref, cos_ref, sin_ref, o_ref):
    h = _rms(x_ref[...], g_ref[...]) * (1.0 + m_ref[1, 0]) + m_ref[0, 0]
    hb = h.astype(BF16)
    cos = cos_ref[...]
    sin = sin_ref[...]
    half = RET_DK // 2
    for c in range(2 * RET_HEADS):
        lo = c * RET_DK
        r = _dot(hb, w_ref[:, lo:lo + RET_DK])
        x1 = r[:, :half]
        x2 = r[:, half:]
        sc = RET_DK ** -0.5 if c < RET_HEADS else 1.0
        o_ref[:, lo:lo + half] = (x1 * cos - x2 * sin) * sc
        o_ref[:, lo + half:lo + RET_DK] = (x1 * sin + x2 * cos) * sc
    base = 2 * RET_HEADS * RET_DK
    for c in range(2 * RET_HEADS):
        lo = base + c * RET_DV
        o_ref[:, lo:lo + RET_DV] = _dot(hb, w_ref[:, lo:lo + RET_DV])


def _inproj_ret(x, mods, gain, w_bf, cos, sin, tiles_per_seq):
    t = x.shape[0]
    n = w_bf.shape[1]
    tm = min(TM, t)
    pos_tiles = cos.shape[0] // tm
    return pl.pallas_call(
        _inproj_ret_body,
        grid=(t // tm,),
        in_specs=[pl.BlockSpec((tm, D_MODEL), lambda i: (i, 0)),
                  _mod_spec(mods, tiles_per_seq),
                  pl.BlockSpec((1, D_MODEL), lambda i: (0, 0)),
                  pl.BlockSpec((D_MODEL, n), lambda i: (0, 0), pipeline_mode=pl.Buffered(1)),
                  pl.BlockSpec((tm, RET_DK // 2), lambda i: (i % pos_tiles, 0)),
                  pl.BlockSpec((tm, RET_DK // 2), lambda i: (i % pos_tiles, 0))],
        out_specs=pl.BlockSpec((tm, n), lambda i: (i, 0)),
        out_shape=jax.ShapeDtypeStruct((t, n), F32),
        compiler_params=_cparams(("arbitrary",)),
        name="inproj_ret",
    )(x, mods, gain, w_bf, cos, sin)


def _inproj_fox_body(x_ref, m_ref, g_ref, w_ref, wfl_ref, bf_ref, tri_ref, o_ref, lf_ref, cum_ref, carry,
                     *, tiles_per_seq, use_carry):
    i = pl.program_id(0)
    tm = x_ref.shape[0]
    h = _rms(x_ref[...], g_ref[...]) * (1.0 + m_ref[1, 0]) + m_ref[0, 0]
    hb = h.astype(BF16)
    n = o_ref.shape[1]
    cw = 512
    for c in range(n // cw):
        o_ref[:, c * cw:(c + 1) * cw] = _dot(hb, w_ref[:, c * cw:(c + 1) * cw])
    fl = _dot(h, wfl_ref[...], HI) + bf_ref[...]
    lf = jnp.minimum(fl, 0.0) - jnp.log1p(jnp.exp(-jnp.abs(fl)))
    lf_ref[...] = lf
    cs = _dot(tri_ref[...], lf, HI)
    if use_carry:
        @pl.when(i % tiles_per_seq == 0)
        def _():
            carry[...] = jnp.zeros_like(carry)
        cs = cs + carry[...]
        carry[...] = cs[tm - 1:tm, :]
    cum_ref[...] = cs


def _inproj_fox(x, mods, gain, w_bf, w_fl, b_fl, tri, tiles_per_seq, use_carry):
    t = x.shape[0]
    n = w_bf.shape[1]
    tm = min(TM, t)
    body = functools.partial(_inproj_fox_body, tiles_per_seq=tiles_per_seq, use_carry=use_carry)
    return pl.pallas_call(
        body,
        grid=(t // tm,),
        in_specs=[pl.BlockSpec((tm, D_MODEL), lambda i: (i, 0)),
                  _mod_spec(mods, tiles_per_seq),
                  pl.BlockSpec((1, D_MODEL), lambda i: (0, 0)),
                  pl.BlockSpec((D_MODEL, n), lambda i: (0, 0), pipeline_mode=pl.Buffered(1)),
                  pl.BlockSpec((D_MODEL, LANES), lambda i: (0, 0)),
                  pl.BlockSpec((1, LANES), lambda i: (0, 0)),
                  pl.BlockSpec((tm, tm), lambda i: (0, 0))],
        out_specs=[pl.BlockSpec((tm, n), lambda i: (i, 0)),
                   pl.BlockSpec((tm, LANES), lambda i: (i, 0)),
                   pl.BlockSpec((tm, LANES), lambda i: (i, 0))],
        out_shape=[jax.ShapeDtypeStruct((t, n), F32),
                   jax.ShapeDtypeStruct((t, LANES), F32),
                   jax.ShapeDtypeStruct((t, LANES), F32)],
        scratch_shapes=[pltpu.VMEM((1, LANES), F32)],
        compiler_params=_cparams(("arbitrary",)),
        name="inproj_fox",
    )(x, mods, gain, w_bf, w_fl, b_fl, tri)


def _retention_body(q_ref, k_ref, v_ref, g_ref, s0_ref, din_ref, dq_ref, dk_ref, dc_ref, gn_ref,
                    u_ref, so_ref, state, *, n_chunks, kv_rows):
    c = pl.program_id(2)

    @pl.when(c == 0)
    def _():
        state[...] = s0_ref[0, 0]

    q = q_ref[...]
    k = k_ref[...]
    v = v_ref[...]
    rows = k.shape[0]
    if rows < kv_rows:
        k = jnp.concatenate([k, jnp.zeros((kv_rows - rows, k.shape[1]), F32)], axis=0)
        v = jnp.concatenate([v, jnp.zeros((kv_rows - rows, v.shape[1]), F32)], axis=0)
    qb = q.astype(BF16)
    kb = k.astype(BF16)
    vb = v.astype(BF16)
    s_old = state[...]
    inner = _dot_nt(qb, kb) * din_ref[0]
    o = _dot(inner.astype(BF16), vb) + _dot(qb, s_old.astype(BF16)) * dq_ref[0]
    kd = (k * dk_ref[0]).astype(BF16)
    s_new = s_old * dc_ref[0] + _dot_tn(kd, vb)
    state[...] = s_new
    y = _rms(o, gn_ref[...])
    g = g_ref[...]
    u_ref[...] = y * (g * jax.nn.sigmoid(g))

    @pl.when(c == n_chunks - 1)
    def _():
        so_ref[0, 0] = s_new


def _retention(proj, s0, din, dq, dk, dc, gn_gain, n_seq, n_chunks, q_rows):
    t = proj.shape[0]
    kv_rows = din.shape[2]
    h = RET_HEADS
    body = functools.partial(_retention_body, n_chunks=n_chunks, kv_rows=kv_rows)
    row = lambda b, hh, c: b * n_chunks + c
    return pl.pallas_call(
        body,
        grid=(n_seq, h, n_chunks),
        in_specs=[pl.BlockSpec((q_rows, RET_DK), lambda b, hh, c: (row(b, hh, c), hh)),
                  pl.BlockSpec((q_rows, RET_DK), lambda b, hh, c: (row(b, hh, c), h + hh)),
                  pl.BlockSpec((q_rows, RET_DV), lambda b, hh, c: (row(b, hh, c), h + hh)),
                  pl.BlockSpec((q_rows, RET_DV), lambda b, hh, c: (row(b, hh, c), 2 * h + hh)),
                  pl.BlockSpec((1, 1, RET_DK, RET_DV), lambda b, hh, c: (b, hh, 0, 0)),
                  pl.BlockSpec((1, q_rows, kv_rows), lambda b, hh, c: (hh, 0, 0)),
                  pl.BlockSpec((1, q_rows, 1), lambda b, hh, c: (hh, 0, 0)),
                  pl.BlockSpec((1, kv_rows, 1), lambda b, hh, c: (hh, 0, 0)),
                  pl.BlockSpec((1, 1, 1), lambda b, hh, c: (hh, 0, 0)),
                  pl.BlockSpec((1, RET_DV), lambda b, hh, c: (0, hh))],
        out_specs=[pl.BlockSpec((q_rows, RET_DV), lambda b, hh, c: (row(b, hh, c), hh)),
                   pl.BlockSpec((1, 1, RET_DK, RET_DV), lambda b, hh, c: (b, hh, 0, 0))],
        out_shape=[jax.ShapeDtypeStruct((t, h * RET_DV), F32),
                   jax.ShapeDtypeStruct((n_seq, h, RET_DK, RET_DV), F32)],
        scratch_shapes=[pltpu.VMEM((RET_DK, RET_DV), F32)],
        compiler_params=_cparams(("arbitrary", "arbitrary", "arbitrary")),
        name="retention",
    )(proj, proj, proj, proj, s0, din, dq, dk, dc, gn_gain)


def _ret_decay_tables(c, q_rows, kv_rows):
    lg = jnp.log(1.0 - 2.0 ** (-5.0 - jnp.arange(RET_HEADS, dtype=F32)))
    idx = jnp.arange(c, dtype=F32)
    rel = idx[:, None] - idx[None, :]
    din = jnp.where(rel[None] >= 0, jnp.exp(lg[:, None, None] * jnp.maximum(rel, 0.0)[None]), 0.0)
    dq = jnp.exp(lg[:, None] * (idx[None, :] + 1.0))
    dk = jnp.exp(lg[:, None] * (c - 1.0 - idx[None, :]))
    dc = jnp.exp(lg * c)
    din = jnp.pad(din, ((0, 0), (0, q_rows - c), (0, kv_rows - c)))
    dq = jnp.pad(dq, ((0, 0), (0, q_rows - c)))[..., None]
    dk = jnp.pad(dk, ((0, 0), (0, kv_rows - c)))[..., None]
    return din, dq, dk, dc[:, None, None]


def _outproj_body(u_ref, w_ref, x_ref, m_ref, gpost_ref, gpre_ref, xn_ref, h2_ref):
    tm = x_ref.shape[0]
    y = _dot(u_ref[...].astype(BF16), w_ref[...])
    xn = x_ref[...] + m_ref[2, 0] * _rms(y, gpost_ref[...])
    xn_ref[...] = xn
    h2 = _rms(xn, gpre_ref[...]) * (1.0 + m_ref[4, 0]) + m_ref[3, 0]
    _store_rows(h2_ref, h2, tm)


def _outproj(u, w_bf, x, mods, gpost, gpre, tiles_per_seq):
    t = x.shape[0]
    din = u.shape[1]
    tm = min(TM, t)
    return pl.pallas_call(
        _outproj_body,
        grid=(t // tm,),
        in_specs=[pl.BlockSpec((tm, din), lambda i: (i, 0)),
                  pl.BlockSpec((din, D_MODEL), lambda i: (0, 0), pipeline_mode=pl.Buffered(1)),
                  pl.BlockSpec((tm, D_MODEL), lambda i: (i, 0)),
                  _mod_spec(mods, tiles_per_seq),
                  pl.BlockSpec((1, D_MODEL), lambda i: (0, 0)),
                  pl.BlockSpec((1, D_MODEL), lambda i: (0, 0))],
        out_specs=[pl.BlockSpec((tm, D_MODEL), lambda i: (i, 0)),
                   pl.BlockSpec((tm * ROW_TILES, LANES), lambda i: (i, 0))],
        out_shape=[jax.ShapeDtypeStruct((t, D_MODEL), F32),
                   jax.ShapeDtypeStruct((t * ROW_TILES, LANES), F32)],
        compiler_params=_cparams(("arbitrary",)),
        name="outproj",
    )(u, w_bf, x, mods, gpost, gpre)


def _router_body(h_ref, w_ref, b_ref, tri_ref, idx_ref, wgt_ref, rank_ref, cnt_ref, carry):
    i = pl.program_id(0)
    tm = idx_ref.shape[0]

    @pl.when(i == 0)
    def _():
        carry[...] = jnp.zeros_like(carry)

    h = _load_rows(h_ref, tm)
    logits = _dot(h, w_ref[...], HI) + b_ref[...]
    lane = lax.broadcasted_iota(I32, (tm, LANES), 1)
    lane_f = lane.astype(F32)
    work = logits
    vals, ids, hots = [], [], []
    for _ in range(TOP_K):
        mx = jnp.max(work, axis=1, keepdims=True)
        ik = jnp.min(jnp.where(work == mx, lane_f, float(LANES)), axis=1, keepdims=True)
        hot = lane_f == ik
        vals.append(mx)
        ids.append(ik.astype(I32))
        hots.append(hot)
        work = jnp.where(hot, -jnp.inf, work)
    ex = [jnp.exp(v - vals[0]) for v in vals]
    den = ex[0] + ex[1] + ex[2] + ex[3]
    chosen = hots[0] | hots[1] | hots[2] | hots[3]
    before = _dot(tri_ref[...], chosen.astype(BF16)) + carry[...]
    idx_o = jnp.zeros((tm, LANES), I32)
    wgt_o = jnp.zeros((tm, LANES), F32)
    rank_o = jnp.zeros((tm, LANES), I32)
    for kk in range(TOP_K):
        rk = jnp.sum(jnp.where(hots[kk], before, 0.0), axis=1, keepdims=True).astype(I32)
        idx_o = jnp.where(lane == kk, ids[kk], idx_o)
        wgt_o = jnp.where(lane == kk, ex[kk] / den, wgt_o)
        rank_o = jnp.where(lane == kk, rk, rank_o)
    idx_ref[...] = idx_o
    wgt_ref[...] = wgt_o
    rank_ref[...] = rank_o
    carry[...] = carry[...] + jnp.sum(chosen.astype(F32), axis=0, keepdims=True)
    cnt_ref[...] = carry[...]


def _router(h2, w_r, b_r, tri):
    t = h2.shape[0] // ROW_TILES
    tm = TM
    return pl.pallas_call(
        _router_body,
        grid=(t // tm,),
        in_specs=[pl.BlockSpec((tm * ROW_TILES, LANES), lambda i: (i, 0)),
                  pl.BlockSpec((D_MODEL, LANES), lambda i: (0, 0)),
                  pl.BlockSpec((1, LANES), lambda i: (0, 0)),
                  pl.BlockSpec((tm, tm), lambda i: (0, 0))],
        out_specs=[pl.BlockSpec((tm, LANES), lambda i: (i, 0)),
                   pl.BlockSpec((tm, LANES), lambda i: (i, 0)),
                   pl.BlockSpec((tm, LANES), lambda i: (i, 0)),
                   pl.BlockSpec((1, LANES), lambda i: (0, 0))],
        out_shape=[jax.ShapeDtypeStruct((t, LANES), I32),
                   jax.ShapeDtypeStruct((t, LANES), F32),
                   jax.ShapeDtypeStruct((t, LANES), I32),
                   jax.ShapeDtypeStruct((1, LANES), F32)],
        scratch_shapes=[pltpu.VMEM((1, LANES), F32)],
        compiler_params=_cparams(("arbitrary",)),
        name="router",
    )(h2, w_r, b_r, tri)


def _token_copy(src, src_row, dst, dst_row, sem):
    return pltpu.make_async_copy(src.at[pl.ds(pl.multiple_of(src_row * ROW_TILES, ROW_TILES), ROW_TILES), :],
                                 dst.at[pl.ds(pl.multiple_of(dst_row * ROW_TILES, ROW_TILES), ROW_TILES), :], sem)


def _dispatch_body(dest_ref, h_ref, xs_in_ref, xs_ref, sem):
    del xs_in_ref
    tm = h_ref.shape[0] // ROW_TILES

    def issue(r, carry):
        for kk in range(TOP_K):
            _token_copy(h_ref, r, xs_ref, dest_ref[0, 0, r * TOP_K + kk], sem).start()
        return carry

    lax.fori_loop(0, tm, issue, 0)

    def drain(r, carry):
        for kk in range(TOP_K):
            _token_copy(h_ref, 0, xs_ref, 0, sem).wait()
        return carry

    lax.fori_loop(0, tm, drain, 0)


def _dispatch(dest3, h2, xs_init):
    n_tiles, _, per = dest3.shape
    tm = per // TOP_K
    return pl.pallas_call(
        _dispatch_body,
        grid=(n_tiles,),
        in_specs=[pl.BlockSpec((1, 1, per), lambda i: (i, 0, 0), memory_space=pltpu.SMEM),
                  pl.BlockSpec((tm * ROW_TILES, LANES), lambda i: (i, 0)),
                  pl.BlockSpec(memory_space=pl.ANY)],
        out_specs=pl.BlockSpec(memory_space=pl.ANY),
        out_shape=jax.ShapeDtypeStruct(xs_init.shape, F32),
        scratch_shapes=[pltpu.SemaphoreType.DMA(())],
        input_output_aliases={2: 0},
        compiler_params=_cparams(("arbitrary",)),
        name="dispatch",
    )(dest3, h2, xs_init)


def _experts_body(te_ref, na_ref, xs_ref, wgu_ref, bgu_ref, wdn_ref, bdn_ref, ys_ref, wgu_bf, wdn_bf):
    i = pl.program_id(0)
    tm = xs_ref.shape[0] // ROW_TILES
    active = i < na_ref[0]

    @pl.when(active)
    def _():
        prev = te_ref[jnp.maximum(i - 1, 0)]

        @pl.when((i == 0) | (te_ref[i] != prev))
        def _():
            wgu_bf[...] = wgu_ref[0, 0].astype(BF16)
            wdn_bf[...] = wdn_ref[0, 0].astype(BF16)

        x = _load_rows(xs_ref, tm).astype(BF16)
        gu = _dot(x, wgu_bf[...]) + bgu_ref[0, 0]
        gate = jnp.minimum(gu[:, :D_FF], SWIGLU_LIMIT)
        up = jnp.clip(gu[:, D_FF:], -SWIGLU_LIMIT, SWIGLU_LIMIT)
        act = (up + 1.0) * gate * jax.nn.sigmoid(SWIGLU_ALPHA * gate)
        y = _dot(act.astype(BF16), wdn_bf[...]) + bdn_ref[0, 0]
        _store_rows(ys_ref, y, tm)

    @pl.when(jnp.logical_not(active))
    def _():
        ys_ref[...] = jnp.zeros_like(ys_ref)


def _experts(layer, tile_expert, n_active, xs, w_gu, b_gu, w_dn, b_dn):
    n_rows = xs.shape[0] // ROW_TILES
    tm = TM_FFN
    b_gu4 = b_gu.reshape(b_gu.shape[0], N_EXPERTS, 1, 2 * D_FF)
    b_dn4 = b_dn.reshape(b_dn.shape[0], N_EXPERTS, 1, D_MODEL)
    grid_spec = pltpu.PrefetchScalarGridSpec(
        num_scalar_prefetch=2,
        grid=(n_rows // tm,),
        in_specs=[pl.BlockSpec((tm * ROW_TILES, LANES), lambda i, te, na: (i, 0)),
                  pl.BlockSpec((1, 1, D_MODEL, 2 * D_FF), lambda i, te, na: (layer, te[i], 0, 0)),
                  pl.BlockSpec((1, 1, 1, 2 * D_FF), lambda i, te, na: (layer, te[i], 0, 0)),
                  pl.BlockSpec((1, 1, D_FF, D_MODEL), lambda i, te, na: (layer, te[i], 0, 0)),
                  pl.BlockSpec((1, 1, 1, D_MODEL), lambda i, te, na: (layer, te[i], 0, 0))],
        out_specs=pl.BlockSpec((tm * ROW_TILES, LANES), lambda i, te, na: (i, 0)),
        scratch_shapes=[pltpu.VMEM((D_MODEL, 2 * D_FF), BF16), pltpu.VMEM((D_FF, D_MODEL), BF16)],
    )
    return pl.pallas_call(
        _experts_body,
        grid_spec=grid_spec,
        out_shape=jax.ShapeDtypeStruct(xs.shape, F32),
        compiler_params=_cparams(("arbitrary",)),
        name="experts",
    )(tile_expert, n_active, xs, w_gu, b_gu4, w_dn, b_dn4)


def _combine_body(dest_ref, wgt_ref, x_ref, m_ref, g_ref, ys_ref, xo_ref, buf, sem):
    tm = x_ref.shape[0]

    def issue(r, carry):
        for kk in range(TOP_K):
            _token_copy(ys_ref, dest_ref[0, 0, r * TOP_K + kk], buf.at[kk], r, sem).start()
        return carry

    lax.fori_loop(0, tm, issue, 0)

    def drain(r, carry):
        for kk in range(TOP_K):
            _token_copy(ys_ref, 0, buf.at[kk], 0, sem).wait()
        return carry

    lax.fori_loop(0, tm, drain, 0)
    w = wgt_ref[...]
    f = jnp.zeros((tm, D_MODEL), F32)
    for kk in range(TOP_K):
        f = f + w[:, kk:kk + 1] * _load_rows(buf.at[kk], tm)
    xo_ref[...] = x_ref[...] + m_ref[5, 0] * _rms(f, g_ref[...])


def _combine(dest3, wgt, x, mods, gpost, ys, tiles_per_seq):
    t = x.shape[0]
    tm = TM_COMB
    if mods.shape[2] == 1:
        mspec = pl.BlockSpec((6, 1, 1, D_MODEL), lambda i: (0, i // tiles_per_seq, 0, 0))
    else:
        mspec = pl.BlockSpec((6, 1, tm, D_MODEL), lambda i: (0, 0, i, 0))
    return pl.pallas_call(
        _combine_body,
        grid=(t // tm,),
        in_specs=[pl.BlockSpec((1, 1, tm * TOP_K), lambda i: (i, 0, 0), memory_space=pltpu.SMEM),
                  pl.BlockSpec((tm, LANES), lambda i: (i, 0)),
                  pl.BlockSpec((tm, D_MODEL), lambda i: (i, 0)),
                  mspec,
                  pl.BlockSpec((1, D_MODEL), lambda i: (0, 0)),
                  pl.BlockSpec(memory_space=pl.ANY)],
        out_specs=pl.BlockSpec((tm, D_MODEL), lambda i: (i, 0)),
        out_shape=jax.ShapeDtypeStruct((t, D_MODEL), F32),
        scratch_shapes=[pltpu.VMEM((TOP_K, tm * ROW_TILES, LANES), F32), pltpu.SemaphoreType.DMA(())],
        compiler_params=_cparams(("arbitrary",)),
        name="combine",
    )(dest3, wgt, x, mods, gpost, ys)


def _moe(layer, h2_all, t_prompt, x_p, x_s, mods_p, mods_s, gpost, w_r, b_r, tri_strict,
         w_gu, b_gu, w_dn, b_dn, tiles_per_seq_comb):
    t_all = h2_all.shape[0] // ROW_TILES
    idx_w, wgt_w, rank_w, cnt_w = _router(h2_all, w_r, b_r, tri_strict)
    idx = idx_w[:, :TOP_K]
    rank = rank_w[:, :TOP_K]
    cnt = cnt_w[0, :N_EXPERTS].astype(I32)
    padded = ((cnt + TM_FFN - 1) // TM_FFN) * TM_FFN
    ends = jnp.cumsum(padded)
    offs = ends - padded
    dest = jnp.sum(jnp.where(idx[..., None] == jnp.arange(N_EXPERTS, dtype=I32), offs, 0), axis=-1) + rank
    n_tiles = (t_all * TOP_K) // TM_FFN + N_EXPERTS
    tile_start = jnp.arange(n_tiles, dtype=I32) * TM_FFN
    tile_expert = jnp.minimum(jnp.sum((tile_start[:, None] >= ends[None, :]).astype(I32), axis=1), N_EXPERTS - 1)
    n_active = (ends[-1] // TM_FFN).astype(I32).reshape(1)

    dest_flat = dest.reshape(-1)
    xs_init = jnp.zeros((n_tiles * TM_FFN * ROW_TILES, LANES), F32)
    xs = _dispatch(dest_flat.reshape(t_all // TM, 1, TM * TOP_K), h2_all, xs_init)
    ys = _experts(layer, tile_expert, n_active, xs, w_gu, b_gu, w_dn, b_dn)

    dest_p = dest_flat[:t_prompt * TOP_K].reshape(t_prompt // TM_COMB, 1, TM_COMB * TOP_K)
    dest_s = dest_flat[t_prompt * TOP_K:].reshape(-1, 1, TM_COMB * TOP_K)
    xo_p = _combine(dest_p, wgt_w[:t_prompt], x_p, mods_p, gpost, ys, tiles_per_seq_comb)
    xo_s = _combine(dest_s, wgt_w[t_prompt:], x_s, mods_s, gpost, ys, 1)
    return xo_p, xo_s


def _fox_prompt_body(q_ref, k_ref, v_ref, g_ref, cum_ref, ck_ref, o_ref, m_s, l_s, acc_s, cq_s):
    hp = pl.program_id(1)
    i = pl.program_id(2)
    j = pl.program_id(3)
    nk = pl.num_programs(3)
    tq = q_ref.shape[0]
    tk = k_ref.shape[0]
    lane = lax.broadcasted_iota(I32, (tq, LANES), 1)

    @pl.when(j == 0)
    def _():
        m_s[...] = jnp.full_like(m_s, -jnp.inf)
        l_s[...] = jnp.zeros_like(l_s)
        acc_s[...] = jnp.zeros_like(acc_s)
        cum = cum_ref[...]
        for a in range(2):
            cq_s[a] = jnp.sum(jnp.where(lane == 2 * hp + a, cum, 0.0), axis=1, keepdims=True)

    @pl.when(j <= i)
    def _():
        q2 = q_ref[...] * (FOX_HD ** -0.5)
        kb = k_ref[...].astype(BF16)
        vb = v_ref[...].astype(BF16)
        qpos = i * tq + lax.broadcasted_iota(I32, (tq, tk), 0)
        kpos = j * tk + lax.broadcasted_iota(I32, (tq, tk), 1)
        causal = kpos <= qpos
        for a in range(2):
            qa = jnp.where((lane >= FOX_HD) == (a == 1), q2, 0.0).astype(BF16)
            s = _dot_nt(qa, kb) + cq_s[a] - ck_ref[0, a:a + 1, :]
            s = jnp.where(causal, s, -jnp.inf)
            m_old = m_s[a]
            m_new = jnp.maximum(m_old, jnp.max(s, axis=1, keepdims=True))
            alpha = jnp.exp(m_old - m_new)
            p = jnp.exp(s - m_new)
            l_s[a] = alpha * l_s[a] + jnp.sum(p, axis=1, keepdims=True)
            acc_s[a] = alpha * acc_s[a] + _dot(p.astype(BF16), vb)
            m_s[a] = m_new

    @pl.when(j == nk - 1)
    def _():
        o = jnp.where(lane < FOX_HD, acc_s[0] / l_s[0], acc_s[1] / l_s[1])
        o_ref[...] = o * jax.nn.sigmoid(g_ref[...])


def _fox_prompt_attn(proj, cum_w, ck3, n_seq, seq):
    t = proj.shape[0]
    nq = seq // TQ
    pairs = FOX_HEADS // 2
    return pl.pallas_call(
        _fox_prompt_body,
        grid=(n_seq, pairs, nq, nq),
        in_specs=[pl.BlockSpec((TQ, LANES), lambda b, hp, i, j: (b * nq + i, hp)),
                  pl.BlockSpec((TQ, LANES), lambda b, hp, i, j: (b * nq + jnp.minimum(j, i), pairs + hp)),
                  pl.BlockSpec((TQ, LANES), lambda b, hp, i, j: (b * nq + jnp.minimum(j, i), 2 * pairs + hp)),
                  pl.BlockSpec((TQ, LANES), lambda b, hp, i, j: (b * nq + i, 3 * pairs + hp)),
                  pl.BlockSpec((TQ, LANES), lambda b, hp, i, j: (b * nq + i, 0)),
                  pl.BlockSpec((1, 2, TQ), lambda b, hp, i, j: (b * pairs + hp, 0, jnp.minimum(j, i)))],
        out_specs=pl.BlockSpec((TQ, LANES), lambda b, hp, i, j: (b * nq + i, hp)),
        out_shape=jax.ShapeDtypeStruct((t, FOX_HEADS * FOX_HD), F32),
        scratch_shapes=[pltpu.VMEM((2, TQ, 1), F32), pltpu.VMEM((2, TQ, 1), F32),
                        pltpu.VMEM((2, TQ, LANES), F32), pltpu.VMEM((2, TQ, 1), F32)],
        compiler_params=_cparams(("arbitrary", "arbitrary", "arbitrary", "arbitrary")),
        name="fox_prompt_attn",
    )(proj, proj, proj, proj, cum_w, ck3)


def _fox_sample_body(pt_ref, q_ref, kn_ref, vn_ref, g_ref, cncol_ref, cnrow_ref, *rest):
    npg = PAGES_PER_STEP
    k_refs = rest[:npg]
    v_refs = rest[npg:2 * npg]
    f_refs = rest[2 * npg:3 * npg]
    u_ref, e_ref, o_ref, m_s, l_s, acc_s, car_s = rest[3 * npg:]
    del pt_ref
    j = pl.program_id(1)
    nj = pl.num_programs(1)
    rows = FOX_HEADS * TOK_PAD

    @pl.when(j == 0)
    def _():
        m_s[...] = jnp.full_like(m_s, -jnp.inf)
        l_s[...] = jnp.zeros_like(l_s)
        acc_s[...] = jnp.zeros_like(acc_s)
        car_s[...] = jnp.zeros_like(car_s)

    q = q_ref[...] * (FOX_HD ** -0.5)
    qh = [q[:, h * FOX_HD:(h + 1) * FOX_HD].astype(BF16) for h in range(FOX_HEADS)]
    cn = cncol_ref[0]

    def update(s, v_of_head):
        m_old = m_s[...]
        m_new = jnp.maximum(m_old, jnp.max(s, axis=1, keepdims=True))
        alpha = jnp.exp(m_old - m_new)
        p = jnp.exp(s - m_new)
        l_s[...] = alpha * l_s[...] + jnp.sum(p, axis=1, keepdims=True)
        pb = p.astype(BF16)
        pv = jnp.concatenate([_dot(pb[h * TOK_PAD:(h + 1) * TOK_PAD, :], v_of_head(h)) for h in range(FOX_HEADS)],
                             axis=0)
        acc_s[...] = alpha * acc_s[...] + pv
        m_s[...] = m_new

    for r in range(npg):
        ft = f_refs[r][0]
        suffix = _dot(ft, u_ref[...], HI) + car_s[...]
        car_s[...] = car_s[...] + jnp.sum(ft, axis=1, keepdims=True)
        bias = _dot(e_ref[...], suffix, HI) + cn
        kr = k_refs[r]
        vr = v_refs[r]
        s = jnp.concatenate([_dot_nt(qh[h], kr[0, 0, :, h, :].astype(BF16)) for h in range(FOX_HEADS)], axis=0)
        update(s + bias, lambda h, vr=vr: vr[0, 0, :, h, :].astype(BF16))

    @pl.when(j == nj - 1)
    def _():
        kn = kn_ref[...]
        vn = vn_ref[...]
        zpad = jnp.zeros((PAGE_SIZE - TOK_PAD, FOX_HD), F32)

        def padded(x, h):
            return jnp.concatenate([x[:, h * FOX_HD:(h + 1) * FOX_HD], zpad], axis=0).astype(BF16)

        s = jnp.concatenate([_dot_nt(qh[h], padded(kn, h)) for h in range(FOX_HEADS)], axis=0)
        trow = lax.broadcasted_iota(I32, (rows, PAGE_SIZE), 0) & (TOK_PAD - 1)
        tcol = lax.broadcasted_iota(I32, (rows, PAGE_SIZE), 1)
        s = jnp.where(tcol <= trow, s + cn - cnrow_ref[0], -jnp.inf)
        update(s, lambda h: padded(vn, h))
        o = acc_s[...] / l_s[...]
        o8 = jnp.concatenate([o[h * TOK_PAD:(h + 1) * TOK_PAD, :] for h in range(FOX_HEADS)], axis=1)
        o_ref[...] = o8 * jax.nn.sigmoid(g_ref[...])


def _fox_sample_attn(proj, cncol, cnrow, cache_k, cache_v, cache_ft, page_table, u_mat, e_mat):
    n_seq, n_pages = page_table.shape
    npg = PAGES_PER_STEP
    steps = n_pages // npg
    wide = FOX_HEADS * FOX_HD
    rows = FOX_HEADS * TOK_PAD

    def page(r):
        return lambda b, j, pt: pt[b * n_pages + (n_pages - 1 - (j * npg + r))]

    kv_specs = [pl.BlockSpec((1, 1, PAGE_SIZE, FOX_HEADS, FOX_HD),
                             (lambda b, j, pt, r=r: (0, page(r)(b, j, pt), 0, 0, 0))) for r in range(npg)]
    f_specs = [pl.BlockSpec((1, FOX_HEADS, PAGE_SIZE), (lambda b, j, pt, r=r: (page(r)(b, j, pt), 0, 0)))
               for r in range(npg)]
    grid_spec = pltpu.PrefetchScalarGridSpec(
        num_scalar_prefetch=1,
        grid=(n_seq, steps),
        in_specs=[pl.BlockSpec((TOK_PAD, wide), lambda b, j, pt: (b, 0)),
                  pl.BlockSpec((TOK_PAD, wide), lambda b, j, pt: (b, 1)),
                  pl.BlockSpec((TOK_PAD, wide), lambda b, j, pt: (b, 2)),
                  pl.BlockSpec((TOK_PAD, wide), lambda b, j, pt: (b, 3)),
                  pl.BlockSpec((1, rows, 1), lambda b, j, pt: (b, 0, 0)),
                  pl.BlockSpec((1, rows, PAGE_SIZE), lambda b, j, pt: (b, 0, 0))]
                 + kv_specs + kv_specs + f_specs
                 + [pl.BlockSpec((PAGE_SIZE, PAGE_SIZE), lambda b, j, pt: (0, 0)),
                    pl.BlockSpec((rows, FOX_HEADS), lambda b, j, pt: (0, 0))],
        out_specs=pl.BlockSpec((TOK_PAD, wide), lambda b, j, pt: (b, 0)),
        scratch_shapes=[pltpu.VMEM((rows, 1), F32), pltpu.VMEM((rows, 1), F32),
                        pltpu.VMEM((rows, FOX_HD), F32), pltpu.VMEM((FOX_HEADS, 1), F32)],
    )
    return pl.pallas_call(
        _fox_sample_body,
        grid_spec=grid_spec,
        out_shape=jax.ShapeDtypeStruct((n_seq * TOK_PAD, wide), F32),
        compiler_params=_cparams(("arbitrary", "arbitrary")),
        name="fox_sample_attn",
    )(page_table.reshape(-1), proj, proj, proj, proj, cncol, cnrow,
      *([cache_k] * npg), *([cache_v] * npg), *([cache_ft] * npg), u_mat, e_mat)


def _rope_tables(pos):
    half = RET_DK // 2
    inv = ROPE_BASE ** (-jnp.arange(half, dtype=F32) / half)
    ang = pos.astype(F32)[:, None] * inv[None, :]
    return jnp.cos(ang), jnp.sin(ang)


def _mods_prompt(m):
    b = m.shape[0]
    return m.reshape(b, 6, D_MODEL).transpose(1, 0, 2)[:, :, None, :]


def _mods_sample(m):
    b = m.shape[0]
    mm = jnp.repeat(m.reshape(b, 6, D_MODEL), TOK_PAD, axis=0)
    return mm.transpose(1, 0, 2)[:, None, :, :]


def kernel(x_prompt, x_sample, c_prompt, c_sample, state_ret, cache_fox_k, cache_fox_v, cache_fox_logf, page_table, w_ada, b_ada, norm_pre_mix, norm_post_mix, norm_pre_ffn, norm_post_ffn, ret_w_in, ret_gn_gain, ret_w_out, fox_w_in, fox_b_f, fox_w_out, moe_w_router, moe_b_router, moe_w_gate_up, moe_b_gate_up, moe_w_down, moe_b_down):
    n_seq, seq, d = x_prompt.shape
    n_dec, dec_seq, _ = x_sample.shape
    t_p = n_seq * seq
    t_s = n_dec * TOK_PAD
    tps = seq // TM

    xp = x_prompt.reshape(t_p, d)
    xs = jnp.pad(x_sample, ((0, 0), (0, TOK_PAD - dec_seq), (0, 0))).reshape(t_s, d)

    m_all = _adaln(jnp.concatenate([c_prompt, c_sample], axis=0), w_ada, b_ada)

    ar = jnp.arange(TM)
    tri_incl = (ar[None, :] <= ar[:, None]).astype(F32)
    tri_strict = (ar[None, :] < ar[:, None]).astype(BF16)
    tri_group = ((ar[None, :] <= ar[:, None]) & (ar[None, :] // TOK_PAD == ar[:, None] // TOK_PAD)).astype(F32)

    mods_p = _mods_prompt(m_all[0, :n_seq])
    mods_s = _mods_sample(m_all[0, n_seq:])
    w_in = ret_w_in[0].astype(BF16)
    cos_p, sin_p = _rope_tables(jnp.arange(seq))
    pos_s = jnp.tile(PAST_LEN + jnp.arange(TOK_PAD), n_dec)
    cos_s, sin_s = _rope_tables(pos_s)
    g_pre = norm_pre_mix[0:1]
    proj_p = _inproj_ret(xp, mods_p, g_pre, w_in, cos_p, sin_p, tps)
    proj_s = _inproj_ret(xs, mods_s, g_pre, w_in, cos_s, sin_s, 1)

    n_chunks = seq // RET_CHUNK
    din, dq, dk, dc = _ret_decay_tables(RET_CHUNK, RET_CHUNK, RET_CHUNK)
    s0p = jnp.zeros((n_seq, RET_HEADS, RET_DK, RET_DV), F32)
    u_p, st_p = _retention(proj_p, s0p, din, dq, dk, dc, ret_gn_gain, n_seq, n_chunks, RET_CHUNK)
    din, dq, dk, dc = _ret_decay_tables(dec_seq, TOK_PAD, RET_CHUNK)
    u_s, st_s = _retention(proj_s, state_ret[0], din, dq, dk, dc, ret_gn_gain, n_dec, 1, TOK_PAD)

    w_out = ret_w_out[0].astype(BF16)
    xp, h2p = _outproj(u_p, w_out, xp, mods_p, norm_post_mix[0:1], norm_pre_ffn[0:1], tps)
    xs, h2s = _outproj(u_s, w_out, xs, mods_s, norm_post_mix[0:1], norm_pre_ffn[0:1], 1)

    def router_params(i):
        w_r = jnp.pad(moe_w_router[i], ((0, 0), (0, LANES - N_EXPERTS)))
        b_r = jnp.pad(moe_b_router[i], (0, LANES - N_EXPERTS), constant_values=-1e30)[None, :]
        return w_r, b_r

    w_r, b_r = router_params(0)
    xp, xs = _moe(0, jnp.concatenate([h2p, h2s], axis=0), t_p, xp, xs, mods_p, mods_s, norm_post_ffn[0:1],
                  w_r, b_r, tri_strict, moe_w_gate_up, moe_b_gate_up, moe_w_down, moe_b_down, seq // TM_COMB)

    mods_p = _mods_prompt(m_all[1, :n_seq])
    mods_s = _mods_sample(m_all[1, n_seq:])
    fw = FOX_HEADS * FOX_HD
    w_in = fox_w_in[0, :, :4 * fw].astype(BF16)
    w_fl = jnp.pad(fox_w_in[0, :, 4 * fw:], ((0, 0), (0, LANES - FOX_HEADS)))
    b_fl = jnp.pad(fox_b_f[0], (0, LANES - FOX_HEADS))[None, :]
    g_pre = norm_pre_mix[1:2]
    fproj_p, lf_p, cum_p = _inproj_fox(xp, mods_p, g_pre, w_in, w_fl, b_fl, tri_incl, tps, True)
    fproj_s, lf_s, cum_s = _inproj_fox(xs, mods_s, g_pre, w_in, w_fl, b_fl, tri_group, 1, False)

    ck3 = cum_p[:, :FOX_HEADS].reshape(n_seq, seq, FOX_HEADS // 2, 2).transpose(0, 2, 3, 1)
    ck3 = ck3.reshape(n_seq * (FOX_HEADS // 2), 2, seq)
    o_p = _fox_prompt_attn(fproj_p, cum_p, ck3, n_seq, seq)

    cn = cum_s[:, :FOX_HEADS].reshape(n_dec, TOK_PAD, FOX_HEADS).transpose(0, 2, 1)
    cncol = cn.reshape(n_dec, FOX_HEADS * TOK_PAD, 1)
    cnrow = jnp.repeat(cn, TOK_PAD, axis=1)
    cnrow = jnp.pad(cnrow, ((0, 0), (0, 0), (0, PAGE_SIZE - TOK_PAD)))
    pr = jnp.arange(PAGE_SIZE)
    u_mat = (pr[:, None] > pr[None, :]).astype(F32)
    e_mat = (jnp.arange(FOX_HEADS * TOK_PAD)[:, None] // TOK_PAD == jnp.arange(FOX_HEADS)[None, :]).astype(F32)
    cache_ft = cache_fox_logf[0].transpose(0, 2, 1)
    o_s = _fox_sample_attn(fproj_s, cncol, cnrow, cache_fox_k, cache_fox_v, cache_ft, page_table, u_mat, e_mat)

    w_out = fox_w_out[0].astype(BF16)
    xp, h2p = _outproj(o_p, w_out, xp, mods_p, norm_post_mix[1:2], norm_pre_ffn[1:2], tps)
    xs, h2s = _outproj(o_s, w_out, xs, mods_s, norm_post_mix[1:2], norm_pre_ffn[1:2], 1)
    w_r, b_r = router_params(1)
    xp, xs = _moe(1, jnp.concatenate([h2p, h2s], axis=0), t_p, xp, xs, mods_p, mods_s, norm_post_ffn[1:2],
                  w_r, b_r, tri_strict, moe_w_gate_up, moe_b_gate_up, moe_w_down, moe_b_down, seq // TM_COMB)

    y_prompt = xp.reshape(n_seq, seq, d)
    y_sample = xs.reshape(n_dec, TOK_PAD, d)[:, :dec_seq]
    kv_shape_p = (1, n_seq, seq, FOX_HEADS, FOX_HD)
    k_prompt = fproj_p[:, fw:2 * fw].reshape(kv_shape_p)
    v_prompt = fproj_p[:, 2 * fw:3 * fw].reshape(kv_shape_p)
    logf_prompt = lf_p[:, :FOX_HEADS].reshape(1, n_seq, seq, FOX_HEADS)
    fs = fproj_s.reshape(n_dec, TOK_PAD, 4 * fw)[:, :dec_seq]
    kv_shape_s = (1, n_dec, dec_seq, FOX_HEADS, FOX_HD)
    k_sample = fs[..., fw:2 * fw].reshape(kv_shape_s)
    v_sample = fs[..., 2 * fw:3 * fw].reshape(kv_shape_s)
    logf_sample = lf_s.reshape(n_dec, TOK_PAD, LANES)[:, :dec_seq, :FOX_HEADS][None]
    return (y_prompt, y_sample, st_p[None], st_s[None], k_prompt, v_prompt, logf_prompt,
            k_sample, v_sample, logf_sample)
```

```python
import functools
import math

import jax
import jax.numpy as jnp
from jax import lax
from jax.experimental import pallas as pl
from jax.experimental.pallas import tpu as pltpu

F32 = jnp.float32
BF16 = jnp.bfloat16
I32 = jnp.int32
HI = lax.Precision.HIGHEST

D_MODEL = 1024
PAST_LEN = 8192
PAGE_SIZE = 128
RET_HEADS = 4
RET_DK = D_MODEL // RET_HEADS
RET_DV = 2 * RET_DK
RET_CHUNK = 128
ROPE_BASE = 10000.0
FOX_HEADS = 16
FOX_HD = D_MODEL // FOX_HEADS
N_EXPERTS = 32
TOP_K = 4
D_FF = D_MODEL
SWIGLU_LIMIT = 7.0
SWIGLU_ALPHA = 1.702
NORM_EPS = 1e-6

LANES = 128
SUBLANES = 8
ROW_TILES = D_MODEL // LANES
TOK_PAD = SUBLANES
TM = 256
TM_FFN = 512
TM_COMB = 128
TQ = 512
PAGES_PER_STEP = 4
VMEM_LIMIT = 56 * 1024 * 1024


def _cparams(sem, vmem=VMEM_LIMIT):
    return pltpu.CompilerParams(dimension_semantics=sem, vmem_limit_bytes=vmem)


def _rms(x, gain):
    return x * lax.rsqrt(jnp.mean(x * x, axis=-1, keepdims=True) + NORM_EPS) * gain


def _dot(a, b, precision=None):
    return jnp.dot(a, b, precision=precision, preferred_element_type=F32)


def _dot_nt(a, b):
    return lax.dot_general(a, b, (((1,), (1,)), ((), ())), preferred_element_type=F32)


def _dot_tn(a, b):
    return lax.dot_general(a, b, (((0,), (0,)), ((), ())), preferred_element_type=F32)


def _mm(a, b, precise, dims=None):
    if precise:
        a, b, prec = a.astype(F32), b.astype(F32), HI
    else:
        a, b, prec = a.astype(BF16), b.astype(BF16), None
    if dims is None:
        return jnp.dot(a, b, precision=prec, preferred_element_type=F32)
    return lax.dot_general(a, b, (dims, ((), ())), precision=prec, preferred_element_type=F32)


_NT = ((1,), (1,))
_TN = ((0,), (0,))


def _load_rows(ref, rows):
    return jnp.concatenate([ref[pl.ds(s, rows, stride=ROW_TILES), :] for s in range(ROW_TILES)], axis=1)


def _store_rows(ref, val, rows):
    for s in range(ROW_TILES):
        ref[pl.ds(s, rows, stride=ROW_TILES), :] = val[:, s * LANES:(s + 1) * LANES]


def _ada_body(c_ref, w_ref, b_ref, o_ref):
    c = c_ref[...]
    a = c * jax.nn.sigmoid(c)
    o_ref[0] = _dot(a, w_ref[0], HI) + b_ref[0]


def _adaln(c_all, w_ada, b_ada):
    depth, d, n = w_ada.shape
    nb = c_all.shape[0]
    tn = 768
    return pl.pallas_call(
        _ada_body,
        grid=(depth, n // tn),
        in_specs=[pl.BlockSpec((nb, d), lambda l, j: (0, 0)),
                  pl.BlockSpec((1, d, tn), lambda l, j: (l, 0, j)),
                  pl.BlockSpec((1, 1, tn), lambda l, j: (l, 0, j))],
        out_specs=pl.BlockSpec((1, nb, tn), lambda l, j: (l, 0, j)),
        out_shape=jax.ShapeDtypeStruct((depth, nb, n), F32),
        compiler_params=_cparams(("arbitrary", "arbitrary")),
        name="adaln",
    )(c_all, w_ada, b_ada.reshape(depth, 1, n))


def _mod_spec(mods, tiles_per_seq):
    r = mods.shape[2]
    if r == 1:
        return pl.BlockSpec((6, 1, 1, D_MODEL), lambda i: (0, i // tiles_per_seq, 0, 0))
    return pl.BlockSpec((6, 1, r, D_MODEL), lambda i: (0, 0, 0, 0))


def _inproj_ret_body(x_ref, m_ref, g_ref, w_ref, cos_ref, sin_ref, o_ref, *, precise):
    h = _rms(x_ref[...], g_ref[...]) * (1.0 + m_ref[1, 0]) + m_ref[0, 0]
    hb = h if precise else h.astype(BF16)
    cos = cos_ref[...]
    sin = sin_ref[...]
    half = RET_DK // 2
    for c in range(2 * RET_HEADS):
        lo = c * RET_DK
        r = _mm(hb, w_ref[:, lo:lo + RET_DK], precise)
        x1 = r[:, :half]
        x2 = r[:, half:]
        sc = RET_DK ** -0.5 if c < RET_HEADS else 1.0
        o_ref[:, lo:lo + half] = (x1 * cos - x2 * sin) * sc
        o_ref[:, lo + half:lo + RET_DK] = (x1 * sin + x2 * cos) * sc
    base = 2 * RET_HEADS * RET_DK
    for c in range(2 * RET_HEADS):
        lo = base + c * RET_DV
        o_ref[:, lo:lo + RET_DV] = _mm(hb, w_ref[:, lo:lo + RET_DV], precise)


def _inproj_ret(x, mods, gain, w, cos, sin, tiles_per_seq):
    t = x.shape[0]
    n = w.shape[1]
    w_bf = w
    tm = min(TM, t)
    pos_tiles = cos.shape[0] // tm
    return pl.pallas_call(
        functools.partial(_inproj_ret_body, precise=w.dtype == F32),
        grid=(t // tm,),
        in_specs=[pl.BlockSpec((tm, D_MODEL), lambda i: (i, 0)),
                  _mod_spec(mods, tiles_per_seq),
                  pl.BlockSpec((1, D_MODEL), lambda i: (0, 0)),
                  pl.BlockSpec((D_MODEL, n), lambda i: (0, 0), pipeline_mode=pl.Buffered(1)),
                  pl.BlockSpec((tm, RET_DK // 2), lambda i: (i % pos_tiles, 0)),
                  pl.BlockSpec((tm, RET_DK // 2), lambda i: (i % pos_tiles, 0))],
        out_specs=pl.BlockSpec((tm, n), lambda i: (i, 0)),
        out_shape=jax.ShapeDtypeStruct((t, n), F32),
        compiler_params=_cparams(("arbitrary",)),
        name="inproj_ret",
    )(x, mods, gain, w_bf, cos, sin)


def _split3(c):
    hi = c.astype(BF16).astype(F32)
    r = c - hi
    mid = r.astype(BF16).astype(F32)
    return hi, mid, r - mid


def _emit_attention_operands(o_ref, cs, qa_ref, ka_ref, va_ref):
    tm = cs.shape[0]
    fw = FOX_HEADS * FOX_HD
    lane = lax.broadcasted_iota(I32, (tm, LANES), 1)
    low = lane < FOX_HD
    ones_q = jnp.where((lane >= FOX_HD + 3) & (lane < FOX_HD + 6), 1.0, 0.0)
    ones_k = jnp.where((lane >= FOX_HD) & (lane < FOX_HD + 3), 1.0, 0.0)
    ones_v = jnp.where(lane == FOX_HD, 1.0, 0.0)
    for p in range(FOX_HEADS // 2):
        pairs = [o_ref[:, s * fw + p * LANES:s * fw + (p + 1) * LANES] for s in range(3)]
        pairs[0] = pairs[0] * (FOX_HD ** -0.5)
        for a in range(2):
            hh = 2 * p + a
            qh, kh, vh = pairs if a == 0 else [pltpu.roll(x, FOX_HD, 1) for x in pairs]
            c = jnp.sum(jnp.where(lane == hh, cs, 0.0), axis=1, keepdims=True)
            hi, mid, lo = _split3(c)
            fq = jnp.where(lane == FOX_HD, hi, jnp.where(lane == FOX_HD + 1, mid,
                                                         jnp.where(lane == FOX_HD + 2, lo, ones_q)))
            fk = jnp.where(lane == FOX_HD + 3, -hi, jnp.where(lane == FOX_HD + 4, -mid,
                                                              jnp.where(lane == FOX_HD + 5, -lo, ones_k)))
            cols = slice(hh * LANES, (hh + 1) * LANES)
            qa_ref[:, cols] = jnp.where(low, qh, fq).astype(BF16)
            ka_ref[:, cols] = jnp.where(low, kh, fk).astype(BF16)
            va_ref[:, cols] = jnp.where(low, vh, ones_v).astype(BF16)


def _inproj_fox_body(x_ref, m_ref, g_ref, w_ref, wfl_ref, bf_ref, tri_ref, o_ref, lf_ref, cum_ref, *rest,
                     tiles_per_seq, use_carry, emit_operands):
    carry = rest[-1]
    i = pl.program_id(0)
    tm = x_ref.shape[0]
    h = _rms(x_ref[...], g_ref[...]) * (1.0 + m_ref[1, 0]) + m_ref[0, 0]
    hb = h.astype(BF16)
    n = o_ref.shape[1]
    cw = 512
    for c in range(n // cw):
        o_ref[:, c * cw:(c + 1) * cw] = _dot(hb, w_ref[:, c * cw:(c + 1) * cw])
    fl = _dot(h, wfl_ref[...], HI) + bf_ref[...]
    lf = jnp.minimum(fl, 0.0) - jnp.log1p(jnp.exp(-jnp.abs(fl)))
    lf_ref[...] = lf
    cs = _dot(tri_ref[...], lf, HI)
    if use_carry:
        @pl.when(i % tiles_per_seq == 0)
        def _():
            carry[...] = jnp.zeros_like(carry)
        cs = cs + carry[...]
        carry[...] = cs[tm - 1:tm, :]
    cum_ref[...] = cs
    if emit_operands:
        _emit_attention_operands(o_ref, cs, *rest[:3])


def _inproj_fox(x, mods, gain, w_bf, w_fl, b_fl, tri, tiles_per_seq, use_carry, emit_operands):
    t = x.shape[0]
    n = w_bf.shape[1]
    tm = min(TM, t)
    body = functools.partial(_inproj_fox_body, tiles_per_seq=tiles_per_seq, use_carry=use_carry,
                             emit_operands=emit_operands)
    wide = FOX_HEADS * LANES
    extra_specs = [pl.BlockSpec((tm, wide), lambda i: (i, 0))] * 3 if emit_operands else []
    extra_shapes = [jax.ShapeDtypeStruct((t, wide), BF16)] * 3 if emit_operands else []
    return pl.pallas_call(
        body,
        grid=(t // tm,),
        in_specs=[pl.BlockSpec((tm, D_MODEL), lambda i: (i, 0)),
                  _mod_spec(mods, tiles_per_seq),
                  pl.BlockSpec((1, D_MODEL), lambda i: (0, 0)),
                  pl.BlockSpec((D_MODEL, n), lambda i: (0, 0), pipeline_mode=pl.Buffered(1)),
                  pl.BlockSpec((D_MODEL, LANES), lambda i: (0, 0)),
                  pl.BlockSpec((1, LANES), lambda i: (0, 0)),
                  pl.BlockSpec((tm, tm), lambda i: (0, 0))],
        out_specs=[pl.BlockSpec((tm, n), lambda i: (i, 0)),
                   pl.BlockSpec((tm, LANES), lambda i: (i, 0)),
                   pl.BlockSpec((tm, LANES), lambda i: (i, 0))] + extra_specs,
        out_shape=[jax.ShapeDtypeStruct((t, n), F32),
                   jax.ShapeDtypeStruct((t, LANES), F32),
                   jax.ShapeDtypeStruct((t, LANES), F32)] + extra_shapes,
        scratch_shapes=[pltpu.VMEM((1, LANES), F32)],
        compiler_params=_cparams(("arbitrary",)),
        name="inproj_fox",
    )(x, mods, gain, w_bf, w_fl, b_fl, tri)


def _retention_body(q_ref, k_ref, v_ref, g_ref, s0_ref, din_ref, dq_ref, dk_ref, dc_ref, gn_ref,
                    u_ref, so_ref, state, *, n_chunks, kv_rows, precise):
    c = pl.program_id(2)

    @pl.when(c == 0)
    def _():
        state[...] = s0_ref[0, 0]

    q = q_ref[...]
    k = k_ref[...]
    v = v_ref[...]
    rows = k.shape[0]
    if rows < kv_rows:
        k = jnp.concatenate([k, jnp.zeros((kv_rows - rows, k.shape[1]), F32)], axis=0)
        v = jnp.concatenate([v, jnp.zeros((kv_rows - rows, v.shape[1]), F32)], axis=0)
    kd = k * dk_ref[0]
    if not precise:
        q, k, v = q.astype(BF16), k.astype(BF16), v.astype(BF16)
    s_old = state[...]
    inner = _mm(q, k, precise, _NT) * din_ref[0]
    o = _mm(inner, v, precise) + _mm(q, s_old, precise) * dq_ref[0]
    s_new = s_old * dc_ref[0] + _mm(kd, v, precise, _TN)
    state[...] = s_new
    y = _rms(o, gn_ref[...])
    g = g_ref[...]
    u_ref[...] = y * (g * jax.nn.sigmoid(g))

    @pl.when(c == n_chunks - 1)
    def _():
        so_ref[0, 0] = s_new


def _retention(proj, s0, din, dq, dk, dc, gn_gain, n_seq, n_chunks, q_rows, precise):
    t = proj.shape[0]
    kv_rows = din.shape[2]
    h = RET_HEADS
    body = functools.partial(_retention_body, n_chunks=n_chunks, kv_rows=kv_rows, precise=precise)
    row = lambda b, hh, c: b * n_chunks + c
    return pl.pallas_call(
        body,
        grid=(n_seq, h, n_chunks),
        in_specs=[pl.BlockSpec((q_rows, RET_DK), lambda b, hh, c: (row(b, hh, c), hh)),
                  pl.BlockSpec((q_rows, RET_DK), lambda b, hh, c: (row(b, hh, c), h + hh)),
                  pl.BlockSpec((q_rows, RET_DV), lambda b, hh, c: (row(b, hh, c), h + hh)),
                  pl.BlockSpec((q_rows, RET_DV), lambda b, hh, c: (row(b, hh, c), 2 * h + hh)),
                  pl.BlockSpec((1, 1, RET_DK, RET_DV), lambda b, hh, c: (b, hh, 0, 0)),
                  pl.BlockSpec((1, q_rows, kv_rows), lambda b, hh, c: (hh, 0, 0)),
                  pl.BlockSpec((1, q_rows, 1), lambda b, hh, c: (hh, 0, 0)),
                  pl.BlockSpec((1, kv_rows, 1), lambda b, hh, c: (hh, 0, 0)),
                  pl.BlockSpec((1, 1, 1), lambda b, hh, c: (hh, 0, 0)),
                  pl.BlockSpec((1, RET_DV), lambda b, hh, c: (0, hh))],
        out_specs=[pl.BlockSpec((q_rows, RET_DV), lambda b, hh, c: (row(b, hh, c), hh)),
                   pl.BlockSpec((1, 1, RET_DK, RET_DV), lambda b, hh, c: (b, hh, 0, 0))],
        out_shape=[jax.ShapeDtypeStruct((t, h * RET_DV), F32),
                   jax.ShapeDtypeStruct((n_seq, h, RET_DK, RET_DV), F32)],
        scratch_shapes=[pltpu.VMEM((RET_DK, RET_DV), F32)],
        compiler_params=_cparams(("arbitrary", "arbitrary", "arbitrary")),
        name="retention",
    )(proj, proj, proj, proj, s0, din, dq, dk, dc, gn_gain)


def _ret_decay_tables(c, q_rows, kv_rows):
    lg = jnp.log(1.0 - 2.0 ** (-5.0 - jnp.arange(RET_HEADS, dtype=F32)))
    idx = jnp.arange(c, dtype=F32)
    rel = idx[:, None] - idx[None, :]
    din = jnp.where(rel[None] >= 0, jnp.exp(lg[:, None, None] * jnp.maximum(rel, 0.0)[None]), 0.0)
    dq = jnp.exp(lg[:, None] * (idx[None, :] + 1.0))
    dk = jnp.exp(lg[:, None] * (c - 1.0 - idx[None, :]))
    dc = jnp.exp(lg * c)
    din = jnp.pad(din, ((0, 0), (0, q_rows - c), (0, kv_rows - c)))
    dq = jnp.pad(dq, ((0, 0), (0, q_rows - c)))[..., None]
    dk = jnp.pad(dk, ((0, 0), (0, kv_rows - c)))[..., None]
    return din, dq, dk, dc[:, None, None]


def _outproj_body(u_ref, w_ref, x_ref, m_ref, gpost_ref, gpre_ref, xn_ref, h2_ref):
    tm = x_ref.shape[0]
    y = _mm(u_ref[...], w_ref[...], w_ref.dtype == F32)
    xn = x_ref[...] + m_ref[2, 0] * _rms(y, gpost_ref[...])
    xn_ref[...] = xn
    h2 = _rms(xn, gpre_ref[...]) * (1.0 + m_ref[4, 0]) + m_ref[3, 0]
    _store_rows(h2_ref, h2, tm)


def _outproj(u, w_bf, x, mods, gpost, gpre, tiles_per_seq):
    t = x.shape[0]
    din = u.shape[1]
    tm = min(TM, t)
    return pl.pallas_call(
        _outproj_body,
        grid=(t // tm,),
        in_specs=[pl.BlockSpec((tm, din), lambda i: (i, 0)),
                  pl.BlockSpec((din, D_MODEL), lambda i: (0, 0), pipeline_mode=pl.Buffered(1)),
                  pl.BlockSpec((tm, D_MODEL), lambda i: (i, 0)),
                  _mod_spec(mods, tiles_per_seq),
                  pl.BlockSpec((1, D_MODEL), lambda i: (0, 0)),
                  pl.BlockSpec((1, D_MODEL), lambda i: (0, 0))],
        out_specs=[pl.BlockSpec((tm, D_MODEL), lambda i: (i, 0)),
                   pl.BlockSpec((tm * ROW_TILES, LANES), lambda i: (i, 0))],
        out_shape=[jax.ShapeDtypeStruct((t, D_MODEL), F32),
                   jax.ShapeDtypeStruct((t * ROW_TILES, LANES), F32)],
        compiler_params=_cparams(("arbitrary",)),
        name="outproj",
    )(u, w_bf, x, mods, gpost, gpre)


def _router_body(h_ref, w_ref, b_ref, tri_ref, idx_ref, wgt_ref, rank_ref, cnt_ref, carry):
    i = pl.program_id(0)
    tm = idx_ref.shape[0]

    @pl.when(i == 0)
    def _():
        carry[...] = jnp.zeros_like(carry)

    h = _load_rows(h_ref, tm)
    logits = _dot(h, w_ref[...], HI) + b_ref[...]
    lane = lax.broadcasted_iota(I32, (tm, LANES), 1)
    lane_f = lane.astype(F32)
    work = logits
    vals, ids, hots = [], [], []
    for _ in range(TOP_K):
        mx = jnp.max(work, axis=1, keepdims=True)
        ik = jnp.min(jnp.where(work == mx, lane_f, float(LANES)), axis=1, keepdims=True)
        hot = lane_f == ik
        vals.append(mx)
        ids.append(ik.astype(I32))
        hots.append(hot)
        work = jnp.where(hot, -jnp.inf, work)
    ex = [jnp.exp(v - vals[0]) for v in vals]
    den = ex[0] + ex[1] + ex[2] + ex[3]
    chosen = hots[0] | hots[1] | hots[2] | hots[3]
    before = _dot(tri_ref[...], chosen.astype(BF16)) + carry[...]
    idx_o = jnp.zeros((tm, LANES), I32)
    wgt_o = jnp.zeros((tm, LANES), F32)
    rank_o = jnp.zeros((tm, LANES), I32)
    for kk in range(TOP_K):
        rk = jnp.sum(jnp.where(hots[kk], before, 0.0), axis=1, keepdims=True).astype(I32)
        idx_o = jnp.where(lane == kk, ids[kk], idx_o)
        wgt_o = jnp.where(lane == kk, ex[kk] / den, wgt_o)
        rank_o = jnp.where(lane == kk, rk, rank_o)
    idx_ref[...] = idx_o
    wgt_ref[...] = wgt_o
    rank_ref[...] = rank_o
    carry[...] = carry[...] + jnp.sum(chosen.astype(F32), axis=0, keepdims=True)
    cnt_ref[...] = carry[...]


def _router(h2, w_r, b_r, tri):
    t = h2.shape[0] // ROW_TILES
    tm = TM
    return pl.pallas_call(
        _router_body,
        grid=(t // tm,),
        in_specs=[pl.BlockSpec((tm * ROW_TILES, LANES), lambda i: (i, 0)),
                  pl.BlockSpec((D_MODEL, LANES), lambda i: (0, 0)),
                  pl.BlockSpec((1, LANES), lambda i: (0, 0)),
                  pl.BlockSpec((tm, tm), lambda i: (0, 0))],
        out_specs=[pl.BlockSpec((tm, LANES), lambda i: (i, 0)),
                   pl.BlockSpec((tm, LANES), lambda i: (i, 0)),
                   pl.BlockSpec((tm, LANES), lambda i: (i, 0)),
                   pl.BlockSpec((1, LANES), lambda i: (0, 0))],
        out_shape=[jax.ShapeDtypeStruct((t, LANES), I32),
                   jax.ShapeDtypeStruct((t, LANES), F32),
                   jax.ShapeDtypeStruct((t, LANES), I32),
                   jax.ShapeDtypeStruct((1, LANES), F32)],
        scratch_shapes=[pltpu.VMEM((1, LANES), F32)],
        compiler_params=_cparams(("arbitrary",)),
        name="router",
    )(h2, w_r, b_r, tri)


def _token_copy(src, src_row, dst, dst_row, sem):
    return pltpu.make_async_copy(src.at[pl.ds(pl.multiple_of(src_row * ROW_TILES, ROW_TILES), ROW_TILES), :],
                                 dst.at[pl.ds(pl.multiple_of(dst_row * ROW_TILES, ROW_TILES), ROW_TILES), :], sem)


def _dispatch_body(dest_ref, h_ref, xs_in_ref, xs_ref, sem):
    del xs_in_ref
    tm = h_ref.shape[0] // ROW_TILES

    def issue(r, carry):
        for kk in range(TOP_K):
            _token_copy(h_ref, r, xs_ref, dest_ref[0, 0, r * TOP_K + kk], sem).start()
        return carry

    lax.fori_loop(0, tm, issue, 0)

    def drain(r, carry):
        for kk in range(TOP_K):
            _token_copy(h_ref, 0, xs_ref, 0, sem).wait()
        return carry

    lax.fori_loop(0, tm, drain, 0)


def _dispatch(dest3, h2, xs_init):
    n_tiles, _, per = dest3.shape
    tm = per // TOP_K
    return pl.pallas_call(
        _dispatch_body,
        grid=(n_tiles,),
        in_specs=[pl.BlockSpec((1, 1, per), lambda i: (i, 0, 0), memory_space=pltpu.SMEM),
                  pl.BlockSpec((tm * ROW_TILES, LANES), lambda i: (i, 0)),
                  pl.BlockSpec(memory_space=pl.ANY)],
        out_specs=pl.BlockSpec(memory_space=pl.ANY),
        out_shape=jax.ShapeDtypeStruct(xs_init.shape, F32),
        scratch_shapes=[pltpu.SemaphoreType.DMA(())],
        input_output_aliases={2: 0},
        compiler_params=_cparams(("arbitrary",)),
        name="dispatch",
    )(dest3, h2, xs_init)


def _experts_body(te_ref, na_ref, xs_ref, wgu_ref, bgu_ref, wdn_ref, bdn_ref, ys_ref, wgu_bf, wdn_bf):
    i = pl.program_id(0)
    tm = xs_ref.shape[0] // ROW_TILES
    active = i < na_ref[0]

    @pl.when(active)
    def _():
        prev = te_ref[jnp.maximum(i - 1, 0)]

        @pl.when((i == 0) | (te_ref[i] != prev))
        def _():
            wgu_bf[...] = wgu_ref[0, 0].astype(BF16)
            wdn_bf[...] = wdn_ref[0, 0].astype(BF16)

        x = _load_rows(xs_ref, tm).astype(BF16)
        gu = _dot(x, wgu_bf[...]) + bgu_ref[0, 0]
        gate = jnp.minimum(gu[:, :D_FF], SWIGLU_LIMIT)
        up = jnp.clip(gu[:, D_FF:], -SWIGLU_LIMIT, SWIGLU_LIMIT)
        act = (up + 1.0) * gate * jax.nn.sigmoid(SWIGLU_ALPHA * gate)
        y = _dot(act.astype(BF16), wdn_bf[...]) + bdn_ref[0, 0]
        _store_rows(ys_ref, y, tm)

    @pl.when(jnp.logical_not(active))
    def _():
        ys_ref[...] = jnp.zeros_like(ys_ref)


def _experts(layer, tile_expert, n_active, xs, w_gu, b_gu, w_dn, b_dn):
    n_rows = xs.shape[0] // ROW_TILES
    tm = TM_FFN
    b_gu4 = b_gu.reshape(b_gu.shape[0], N_EXPERTS, 1, 2 * D_FF)
    b_dn4 = b_dn.reshape(b_dn.shape[0], N_EXPERTS, 1, D_MODEL)
    grid_spec = pltpu.PrefetchScalarGridSpec(
        num_scalar_prefetch=2,
        grid=(n_rows // tm,),
        in_specs=[pl.BlockSpec((tm * ROW_TILES, LANES), lambda i, te, na: (i, 0)),
                  pl.BlockSpec((1, 1, D_MODEL, 2 * D_FF), lambda i, te, na: (layer, te[i], 0, 0)),
                  pl.BlockSpec((1, 1, 1, 2 * D_FF), lambda i, te, na: (layer, te[i], 0, 0)),
                  pl.BlockSpec((1, 1, D_FF, D_MODEL), lambda i, te, na: (layer, te[i], 0, 0)),
                  pl.BlockSpec((1, 1, 1, D_MODEL), lambda i, te, na: (layer, te[i], 0, 0))],
        out_specs=pl.BlockSpec((tm * ROW_TILES, LANES), lambda i, te, na: (i, 0)),
        scratch_shapes=[pltpu.VMEM((D_MODEL, 2 * D_FF), BF16), pltpu.VMEM((D_FF, D_MODEL), BF16)],
    )
    return pl.pallas_call(
        _experts_body,
        grid_spec=grid_spec,
        out_shape=jax.ShapeDtypeStruct(xs.shape, F32),
        compiler_params=_cparams(("arbitrary",)),
        name="experts",
    )(tile_expert, n_active, xs, w_gu, b_gu4, w_dn, b_dn4)


def _combine_body(dest_ref, wgt_ref, x_ref, m_ref, g_ref, ys_ref, xo_ref, buf, sem):
    tm = x_ref.shape[0]

    def issue(r, carry):
        for kk in range(TOP_K):
            _token_copy(ys_ref, dest_ref[0, 0, r * TOP_K + kk], buf.at[kk], r, sem).start()
        return carry

    lax.fori_loop(0, tm, issue, 0)

    def drain(r, carry):
        for kk in range(TOP_K):
            _token_copy(ys_ref, 0, buf.at[kk], 0, sem).wait()
        return carry

    lax.fori_loop(0, tm, drain, 0)
    w = wgt_ref[...]
    f = jnp.zeros((tm, D_MODEL), F32)
    for kk in range(TOP_K):
        f = f + w[:, kk:kk + 1] * _load_rows(buf.at[kk], tm)
    xo_ref[...] = x_ref[...] + m_ref[5, 0] * _rms(f, g_ref[...])


def _combine(dest3, wgt, x, mods, gpost, ys, tiles_per_seq):
    t = x.shape[0]
    tm = TM_COMB
    if mods.shape[2] == 1:
        mspec = pl.BlockSpec((6, 1, 1, D_MODEL), lambda i: (0, i // tiles_per_seq, 0, 0))
    else:
        mspec = pl.BlockSpec((6, 1, tm, D_MODEL), lambda i: (0, 0, i, 0))
    return pl.pallas_call(
        _combine_body,
        grid=(t // tm,),
        in_specs=[pl.BlockSpec((1, 1, tm * TOP_K), lambda i: (i, 0, 0), memory_space=pltpu.SMEM),
                  pl.BlockSpec((tm, LANES), lambda i: (i, 0)),
                  pl.BlockSpec((tm, D_MODEL), lambda i: (i, 0)),
                  mspec,
                  pl.BlockSpec((1, D_MODEL), lambda i: (0, 0)),
                  pl.BlockSpec(memory_space=pl.ANY)],
        out_specs=pl.BlockSpec((tm, D_MODEL), lambda i: (i, 0)),
        out_shape=jax.ShapeDtypeStruct((t, D_MODEL), F32),
        scratch_shapes=[pltpu.VMEM((TOP_K, tm * ROW_TILES, LANES), F32), pltpu.SemaphoreType.DMA(())],
        compiler_params=_cparams(("arbitrary",)),
        name="combine",
    )(dest3, wgt, x, mods, gpost, ys)


def _moe(layer, h2_all, t_prompt, x_p, x_s, mods_p, mods_s, gpost, w_r, b_r, tri_strict,
         w_gu, b_gu, w_dn, b_dn, tiles_per_seq_comb):
    t_all = h2_all.shape[0] // ROW_TILES
    idx_w, wgt_w, rank_w, cnt_w = _router(h2_all, w_r, b_r, tri_strict)
    idx = idx_w[:, :TOP_K]
    rank = rank_w[:, :TOP_K]
    cnt = cnt_w[0, :N_EXPERTS].astype(I32)
    padded = ((cnt + TM_FFN - 1) // TM_FFN) * TM_FFN
    ends = jnp.cumsum(padded)
    offs = ends - padded
    dest = jnp.sum(jnp.where(idx[..., None] == jnp.arange(N_EXPERTS, dtype=I32), offs, 0), axis=-1) + rank
    n_tiles = (t_all * TOP_K) // TM_FFN + N_EXPERTS
    tile_start = jnp.arange(n_tiles, dtype=I32) * TM_FFN
    tile_expert = jnp.minimum(jnp.sum((tile_start[:, None] >= ends[None, :]).astype(I32), axis=1), N_EXPERTS - 1)
    n_active = (ends[-1] // TM_FFN).astype(I32).reshape(1)

    dest_flat = dest.reshape(-1)
    xs_init = jnp.zeros((n_tiles * TM_FFN * ROW_TILES, LANES), F32)
    xs = _dispatch(dest_flat.reshape(t_all // TM, 1, TM * TOP_K), h2_all, xs_init)
    ys = _experts(layer, tile_expert, n_active, xs, w_gu, b_gu, w_dn, b_dn)

    dest_p = dest_flat[:t_prompt * TOP_K].reshape(t_prompt // TM_COMB, 1, TM_COMB * TOP_K)
    dest_s = dest_flat[t_prompt * TOP_K:].reshape(-1, 1, TM_COMB * TOP_K)
    xo_p = _combine(dest_p, wgt_w[:t_prompt], x_p, mods_p, gpost, ys, tiles_per_seq_comb)
    xo_s = _combine(dest_s, wgt_w[t_prompt:], x_s, mods_s, gpost, ys, 1)
    return xo_p, xo_s


def _fox_prompt_body(qa_ref, ka_ref, va_ref, g_ref, o_ref, m_s, acc_s):
    i = pl.program_id(2)
    tq = qa_ref.shape[0]
    m_s[...] = jnp.full_like(m_s, -jnp.inf)
    acc_s[...] = jnp.zeros_like(acc_s)

    def block(start, diagonal):
        if diagonal:
            causal = lax.broadcasted_iota(I32, (tq, tq), 1) <= lax.broadcasted_iota(I32, (tq, tq), 0)
        for a in range(2):
            cols = slice(a * LANES, (a + 1) * LANES)
            s = _dot_nt(qa_ref[:, cols], ka_ref[pl.ds(start, tq), cols])
            if diagonal:
                s = jnp.where(causal, s, -jnp.inf)
            m_old = m_s[a]
            m_new = jnp.maximum(m_old, jnp.max(s, axis=1, keepdims=True))
            p = jnp.exp((s - m_new).astype(BF16))
            acc_s[a] = jnp.exp(m_old - m_new) * acc_s[a] + _dot(p, va_ref[pl.ds(start, tq), cols])
            m_s[a] = m_new

    def body(j, carry):
        block(pl.multiple_of(j * tq, tq), False)
        return carry

    lax.fori_loop(0, i, body, 0)
    block(pl.multiple_of(i * tq, tq), True)
    lane = lax.broadcasted_iota(I32, (tq, LANES), 1)
    outs = [acc_s[a] / acc_s[a][:, FOX_HD:FOX_HD + 1] for a in range(2)]
    o = jnp.where(lane < FOX_HD, outs[0], pltpu.roll(outs[1], FOX_HD, 1))
    o_ref[...] = o * jax.nn.sigmoid(g_ref[...])


def _fox_prompt_attn(qa, ka, va, proj, n_seq, seq):
    t = proj.shape[0]
    nq = seq // TQ
    pairs = FOX_HEADS // 2
    return pl.pallas_call(
        _fox_prompt_body,
        grid=(n_seq, pairs, nq),
        in_specs=[pl.BlockSpec((TQ, 2 * LANES), lambda b, hp, i: (b * nq + i, hp)),
                  pl.BlockSpec((seq, 2 * LANES), lambda b, hp, i: (b, hp)),
                  pl.BlockSpec((seq, 2 * LANES), lambda b, hp, i: (b, hp)),
                  pl.BlockSpec((TQ, LANES), lambda b, hp, i: (b * nq + i, 3 * pairs + hp))],
        out_specs=pl.BlockSpec((TQ, LANES), lambda b, hp, i: (b * nq + i, hp)),
        out_shape=jax.ShapeDtypeStruct((t, FOX_HEADS * FOX_HD), F32),
        scratch_shapes=[pltpu.VMEM((2, TQ, 1), F32), pltpu.VMEM((2, TQ, LANES), F32)],
        compiler_params=_cparams(("arbitrary", "arbitrary", "arbitrary")),
        name="fox_prompt_attn",
    )(qa, ka, va, proj)


def _fox_sample_body(pt_ref, q_ref, kn_ref, vn_ref, g_ref, cncol_ref, cnrow_ref, *rest):
    npg = PAGES_PER_STEP
    k_refs = rest[:npg]
    v_refs = rest[npg:2 * npg]
    f_refs = rest[2 * npg:3 * npg]
    u_ref, e_ref, o_ref, qbd_s, m_s, l_s, acc_s, car_s = rest[3 * npg:]
    del pt_ref
    j = pl.program_id(1)
    nj = pl.num_programs(1)
    rows = FOX_HEADS * TOK_PAD
    wide = FOX_HEADS * FOX_HD
    own = ((lax.broadcasted_iota(I32, (rows, wide), 0) // TOK_PAD)
           == (lax.broadcasted_iota(I32, (rows, wide), 1) // FOX_HD))

    @pl.when(j == 0)
    def _():
        m_s[...] = jnp.full_like(m_s, -jnp.inf)
        l_s[...] = jnp.zeros_like(l_s)
        acc_s[...] = jnp.zeros_like(acc_s)
        car_s[...] = jnp.zeros_like(car_s)
        q = q_ref[...] * (FOX_HD ** -0.5)
        qbd_s[...] = jnp.where(own, jnp.concatenate([q] * FOX_HEADS, axis=0), 0.0).astype(BF16)

    qbd = qbd_s[...]
    cn = cncol_ref[0]

    def update(s, pv_of):
        m_old = m_s[...]
        m_new = jnp.maximum(m_old, jnp.max(s, axis=1, keepdims=True))
        alpha = jnp.exp(m_old - m_new)
        p = jnp.exp(s - m_new)
        l_s[...] = alpha * l_s[...] + jnp.sum(p, axis=1, keepdims=True)
        acc_s[...] = alpha * acc_s[...] + pv_of(p.astype(BF16))
        m_s[...] = m_new

    suffixes = []
    for r in range(npg):
        ft = f_refs[r][0]
        suffixes.append(_dot(ft, u_ref[...], HI) + car_s[...])
        car_s[...] = car_s[...] + jnp.sum(ft, axis=1, keepdims=True)
    bias = _dot(e_ref[...], jnp.concatenate(suffixes, axis=1), HI) + cn
    kt = jnp.concatenate([k_refs[r][0].astype(BF16) for r in range(npg)], axis=1)
    vt = jnp.concatenate([v_refs[r][0].astype(BF16) for r in range(npg)], axis=1)
    update(_dot(qbd, kt) + bias, lambda pb: _dot_nt(pb, vt))

    @pl.when(j == nj - 1)
    def _():
        zpad = jnp.zeros((PAGE_SIZE - TOK_PAD, wide), F32)
        kn = jnp.concatenate([kn_ref[...], zpad], axis=0).astype(BF16)
        vn = jnp.concatenate([vn_ref[...], zpad], axis=0).astype(BF16)
        trow = lax.broadcasted_iota(I32, (rows, PAGE_SIZE), 0) & (TOK_PAD - 1)
        tcol = lax.broadcasted_iota(I32, (rows, PAGE_SIZE), 1)
        s = jnp.where(tcol <= trow, _dot_nt(qbd, kn) + cn - cnrow_ref[0], -jnp.inf)
        update(s, lambda pb: _dot(pb, vn))
        o = jnp.where(own, acc_s[...] / l_s[...], 0.0)
        o8 = o[0:TOK_PAD, :]
        for h in range(1, FOX_HEADS):
            o8 = o8 + o[h * TOK_PAD:(h + 1) * TOK_PAD, :]
        o_ref[...] = o8 * jax.nn.sigmoid(g_ref[...])


def _fox_sample_attn(proj, cncol, cnrow, cache_k, cache_v, cache_ft, page_table, u_mat, e_mat):
    n_seq, n_pages = page_table.shape
    npg = PAGES_PER_STEP
    steps = n_pages // npg
    wide = FOX_HEADS * FOX_HD
    rows = FOX_HEADS * TOK_PAD

    def page(r):
        return lambda b, j, pt: pt[b * n_pages + (n_pages - 1 - (j * npg + r))]

    kv_specs = [pl.BlockSpec((1, wide, PAGE_SIZE), (lambda b, j, pt, r=r: (page(r)(b, j, pt), 0, 0)))
                for r in range(npg)]
    f_specs = [pl.BlockSpec((1, FOX_HEADS, PAGE_SIZE), (lambda b, j, pt, r=r: (page(r)(b, j, pt), 0, 0)))
               for r in range(npg)]
    grid_spec = pltpu.PrefetchScalarGridSpec(
        num_scalar_prefetch=1,
        grid=(n_seq, steps),
        in_specs=[pl.BlockSpec((TOK_PAD, wide), lambda b, j, pt: (b, 0)),
                  pl.BlockSpec((TOK_PAD, wide), lambda b, j, pt: (b, 1)),
                  pl.BlockSpec((TOK_PAD, wide), lambda b, j, pt: (b, 2)),
                  pl.BlockSpec((TOK_PAD, wide), lambda b, j, pt: (b, 3)),
                  pl.BlockSpec((1, rows, 1), lambda b, j, pt: (b, 0, 0)),
                  pl.BlockSpec((1, rows, PAGE_SIZE), lambda b, j, pt: (b, 0, 0))]
                 + kv_specs + kv_specs + f_specs
                 + [pl.BlockSpec((PAGE_SIZE, PAGE_SIZE), lambda b, j, pt: (0, 0)),
                    pl.BlockSpec((rows, FOX_HEADS), lambda b, j, pt: (0, 0))],
        out_specs=pl.BlockSpec((TOK_PAD, wide), lambda b, j, pt: (b, 0)),
        scratch_shapes=[pltpu.VMEM((rows, wide), BF16), pltpu.VMEM((rows, 1), F32), pltpu.VMEM((rows, 1), F32),
                        pltpu.VMEM((rows, wide), F32), pltpu.VMEM((FOX_HEADS, 1), F32)],
    )
    return pl.pallas_call(
        _fox_sample_body,
        grid_spec=grid_spec,
        out_shape=jax.ShapeDtypeStruct((n_seq * TOK_PAD, wide), F32),
        compiler_params=_cparams(("arbitrary", "arbitrary")),
        name="fox_sample_attn",
    )(page_table.reshape(-1), proj, proj, proj, proj, cncol, cnrow,
      *([cache_k] * npg), *([cache_v] * npg), *([cache_ft] * npg), u_mat, e_mat)


def _rope_tables(pos):
    half = RET_DK // 2
    inv = ROPE_BASE ** (-jnp.arange(half, dtype=F32) / half)
    ang = pos.astype(F32)[:, None] * inv[None, :]
    return jnp.cos(ang), jnp.sin(ang)


def _mods_prompt(m):
    b = m.shape[0]
    return m.reshape(b, 6, D_MODEL).transpose(1, 0, 2)[:, :, None, :]


def _mods_sample(m):
    b = m.shape[0]
    mm = jnp.repeat(m.reshape(b, 6, D_MODEL), TOK_PAD, axis=0)
    return mm.transpose(1, 0, 2)[:, None, :, :]


def kernel(x_prompt, x_sample, c_prompt, c_sample, state_ret, cache_fox_k, cache_fox_v, cache_fox_logf, page_table, w_ada, b_ada, norm_pre_mix, norm_post_mix, norm_pre_ffn, norm_post_ffn, ret_w_in, ret_gn_gain, ret_w_out, fox_w_in, fox_b_f, fox_w_out, moe_w_router, moe_b_router, moe_w_gate_up, moe_b_gate_up, moe_w_down, moe_b_down):
    n_seq, seq, d = x_prompt.shape
    n_dec, dec_seq, _ = x_sample.shape
    t_p = n_seq * seq
    t_s = n_dec * TOK_PAD
    tps = seq // TM

    xp = x_prompt.reshape(t_p, d)
    xs = jnp.pad(x_sample, ((0, 0), (0, TOK_PAD - dec_seq), (0, 0))).reshape(t_s, d)

    m_all = _adaln(jnp.concatenate([c_prompt, c_sample], axis=0), w_ada, b_ada)

    ar = jnp.arange(TM)
    tri_incl = (ar[None, :] <= ar[:, None]).astype(F32)
    tri_strict = (ar[None, :] < ar[:, None]).astype(BF16)
    tri_group = ((ar[None, :] <= ar[:, None]) & (ar[None, :] // TOK_PAD == ar[:, None] // TOK_PAD)).astype(F32)

    mods_p = _mods_prompt(m_all[0, :n_seq])
    mods_s = _mods_sample(m_all[0, n_seq:])
    w_in = ret_w_in[0].astype(BF16)
    cos_p, sin_p = _rope_tables(jnp.arange(seq))
    pos_s = jnp.tile(PAST_LEN + jnp.arange(TOK_PAD), n_dec)
    cos_s, sin_s = _rope_tables(pos_s)
    g_pre = norm_pre_mix[0:1]
    proj_p = _inproj_ret(xp, mods_p, g_pre, w_in, cos_p, sin_p, tps)
    proj_s = _inproj_ret(xs, mods_s, g_pre, ret_w_in[0], cos_s, sin_s, 1)

    n_chunks = seq // RET_CHUNK
    din, dq, dk, dc = _ret_decay_tables(RET_CHUNK, RET_CHUNK, RET_CHUNK)
    s0p = jnp.zeros((n_seq, RET_HEADS, RET_DK, RET_DV), F32)
    u_p, st_p = _retention(proj_p, s0p, din, dq, dk, dc, ret_gn_gain, n_seq, n_chunks, RET_CHUNK, False)
    din, dq, dk, dc = _ret_decay_tables(dec_seq, TOK_PAD, RET_CHUNK)
    u_s, st_s = _retention(proj_s, state_ret[0], din, dq, dk, dc, ret_gn_gain, n_dec, 1, TOK_PAD, True)

    w_out = ret_w_out[0].astype(BF16)
    xp, h2p = _outproj(u_p, w_out, xp, mods_p, norm_post_mix[0:1], norm_pre_ffn[0:1], tps)
    xs, h2s = _outproj(u_s, ret_w_out[0], xs, mods_s, norm_post_mix[0:1], norm_pre_ffn[0:1], 1)

    def router_params(i):
        w_r = jnp.pad(moe_w_router[i], ((0, 0), (0, LANES - N_EXPERTS)))
        b_r = jnp.pad(moe_b_router[i], (0, LANES - N_EXPERTS), constant_values=-1e30)[None, :]
        return w_r, b_r

    w_r, b_r = router_params(0)
    xp, xs = _moe(0, jnp.concatenate([h2p, h2s], axis=0), t_p, xp, xs, mods_p, mods_s, norm_post_ffn[0:1],
                  w_r, b_r, tri_strict, moe_w_gate_up, moe_b_gate_up, moe_w_down, moe_b_down, seq // TM_COMB)

    mods_p = _mods_prompt(m_all[1, :n_seq])
    mods_s = _mods_sample(m_all[1, n_seq:])
    fw = FOX_HEADS * FOX_HD
    w_in = fox_w_in[0, :, :4 * fw].astype(BF16)
    w_fl = jnp.pad(fox_w_in[0, :, 4 * fw:], ((0, 0), (0, LANES - FOX_HEADS)))
    b_fl = jnp.pad(fox_b_f[0], (0, LANES - FOX_HEADS))[None, :]
    g_pre = norm_pre_mix[1:2]
    fproj_p, lf_p, _, qa, ka, va = _inproj_fox(xp, mods_p, g_pre, w_in, w_fl, b_fl, tri_incl, tps, True, True)
    fproj_s, lf_s, cum_s = _inproj_fox(xs, mods_s, g_pre, w_in, w_fl, b_fl, tri_group, 1, False, False)
    o_p = _fox_prompt_attn(qa, ka, va, fproj_p, n_seq, seq)

    cn = cum_s[:, :FOX_HEADS].reshape(n_dec, TOK_PAD, FOX_HEADS).transpose(0, 2, 1)
    cncol = cn.reshape(n_dec, FOX_HEADS * TOK_PAD, 1)
    cnrow = jnp.repeat(cn, TOK_PAD, axis=1)
    cnrow = jnp.pad(cnrow, ((0, 0), (0, 0), (0, PAGE_SIZE - TOK_PAD)))
    pr = jnp.arange(PAGE_SIZE)
    u_mat = (pr[:, None] > pr[None, :]).astype(F32)
    e_mat = (jnp.arange(FOX_HEADS * TOK_PAD)[:, None] // TOK_PAD == jnp.arange(FOX_HEADS)[None, :]).astype(F32)
    n_pool = cache_fox_k.shape[1]
    cache_kt = cache_fox_k[0].transpose(0, 2, 3, 1).reshape(n_pool, fw, PAGE_SIZE)
    cache_vt = cache_fox_v[0].transpose(0, 2, 3, 1).reshape(n_pool, fw, PAGE_SIZE)
    cache_ft = cache_fox_logf[0].transpose(0, 2, 1)
    o_s = _fox_sample_attn(fproj_s, cncol, cnrow, cache_kt, cache_vt, cache_ft, page_table, u_mat, e_mat)

    w_out = fox_w_out[0].astype(BF16)
    xp, h2p = _outproj(o_p, w_out, xp, mods_p, norm_post_mix[1:2], norm_pre_ffn[1:2], tps)
    xs, h2s = _outproj(o_s, w_out, xs, mods_s, norm_post_mix[1:2], norm_pre_ffn[1:2], 1)
    w_r, b_r = router_params(1)
    xp, xs = _moe(1, jnp.concatenate([h2p, h2s], axis=0), t_p, xp, xs, mods_p, mods_s, norm_post_ffn[1:2],
                  w_r, b_r, tri_strict, moe_w_gate_up, moe_b_gate_up, moe_w_down, moe_b_down, seq // TM_COMB)

    y_prompt = xp.reshape(n_seq, seq, d)
    y_sample = xs.reshape(n_dec, TOK_PAD, d)[:, :dec_seq]
    kv_shape_p = (1, n_seq, seq, FOX_HEADS, FOX_HD)
    k_prompt = fproj_p[:, fw:2 * fw].reshape(kv_shape_p)
    v_prompt = fproj_p[:, 2 * fw:3 * fw].reshape(kv_shape_p)
    logf_prompt = lf_p[:, :FOX_HEADS].reshape(1, n_seq, seq, FOX_HEADS)
    fs = fproj_s.reshape(n_dec, TOK_PAD, 4 * fw)[:, :dec_seq]
    kv_shape_s = (1, n_dec, dec_seq, FOX_HEADS, FOX_HD)
    k_sample = fs[..., fw:2 * fw].reshape(kv_shape_s)
    v_sample = fs[..., 2 * fw:3 * fw].reshape(kv_shape_s)
    logf_sample = lf_s.reshape(n_dec, TOK_PAD, LANES)[:, :dec_seq, :FOX_HEADS][None]
    return (y_prompt, y_sample, st_p[None], st_s[None], k_prompt, v_prompt, logf_prompt,
            k_sample, v_sample, logf_sample)
```

```python
import functools

import jax
import jax.numpy as jnp
from jax import lax
from jax.experimental import pallas as pl
from jax.experimental.pallas import tpu as pltpu

F32 = jnp.float32
BF16 = jnp.bfloat16
I32 = jnp.int32
HI = lax.Precision.HIGHEST

D_MODEL = 1024
PAST_LEN = 8192
PAGE_SIZE = 128
RET_HEADS = 4
RET_DK = D_MODEL // RET_HEADS
RET_DV = 2 * RET_DK
RET_CHUNK = 128
ROPE_BASE = 10000.0
FOX_HEADS = 16
FOX_HD = D_MODEL // FOX_HEADS
N_EXPERTS = 32
TOP_K = 4
D_FF = D_MODEL
SWIGLU_LIMIT = 7.0
SWIGLU_ALPHA = 1.702
NORM_EPS = 1e-6

LANES = 128
SUBLANES = 8
ROW_TILES = D_MODEL // LANES
TOK_PAD = SUBLANES
TM = 256
TM_FFN = 512
TM_COMB = 128
TQ = 512
PAGES_PER_STEP = 8
VMEM_LIMIT = 56 * 1024 * 1024


def _cparams(sem, vmem=VMEM_LIMIT):
    return pltpu.CompilerParams(dimension_semantics=sem, vmem_limit_bytes=vmem)


def _rms(x, gain):
    return x * lax.rsqrt(jnp.mean(x * x, axis=-1, keepdims=True) + NORM_EPS) * gain


def _dot(a, b, precision=None):
    return jnp.dot(a, b, precision=precision, preferred_element_type=F32)


def _dot_nt(a, b):
    return lax.dot_general(a, b, (((1,), (1,)), ((), ())), preferred_element_type=F32)


def _mm(a, b, precise, dims=None):
    if precise:
        a, b, prec = a.astype(F32), b.astype(F32), HI
    else:
        a, b, prec = a.astype(BF16), b.astype(BF16), None
    if dims is None:
        return jnp.dot(a, b, precision=prec, preferred_element_type=F32)
    return lax.dot_general(a, b, (dims, ((), ())), precision=prec, preferred_element_type=F32)


_NT = ((1,), (1,))
_TN = ((0,), (0,))


def _load_rows(ref, rows):
    return jnp.concatenate([ref[pl.ds(s, rows, stride=ROW_TILES), :] for s in range(ROW_TILES)], axis=1)


def _store_rows(ref, val, rows):
    for s in range(ROW_TILES):
        ref[pl.ds(s, rows, stride=ROW_TILES), :] = val[:, s * LANES:(s + 1) * LANES]


def _ada_body(c_ref, w_ref, b_ref, o_ref):
    c = c_ref[...]
    a = c * jax.nn.sigmoid(c)
    o_ref[0] = _dot(a, w_ref[0], HI) + b_ref[0]


def _adaln(c_all, w_ada, b_ada):
    depth, d, n = w_ada.shape
    nb = c_all.shape[0]
    tn = 768
    return pl.pallas_call(
        _ada_body,
        grid=(depth, n // tn),
        in_specs=[pl.BlockSpec((nb, d), lambda l, j: (0, 0)),
                  pl.BlockSpec((1, d, tn), lambda l, j: (l, 0, j)),
                  pl.BlockSpec((1, 1, tn), lambda l, j: (l, 0, j))],
        out_specs=pl.BlockSpec((1, nb, tn), lambda l, j: (l, 0, j)),
        out_shape=jax.ShapeDtypeStruct((depth, nb, n), F32),
        compiler_params=_cparams(("arbitrary", "arbitrary")),
        name="adaln",
    )(c_all, w_ada, b_ada.reshape(depth, 1, n))


def _mod_spec(mods, tiles_per_seq):
    r = mods.shape[2]
    if r == 1:
        return pl.BlockSpec((6, 1, 1, D_MODEL), lambda i: (0, i // tiles_per_seq, 0, 0))
    return pl.BlockSpec((6, 1, r, D_MODEL), lambda i: (0, 0, 0, 0))


def _inproj_ret_body(x_ref, m_ref, g_ref, w_ref, cos_ref, sin_ref, o_ref, *, precise):
    h = _rms(x_ref[...], g_ref[...]) * (1.0 + m_ref[1, 0]) + m_ref[0, 0]
    hb = h if precise else h.astype(BF16)
    cos = cos_ref[...]
    sin = sin_ref[...]
    half = RET_DK // 2
    for c in range(2 * RET_HEADS):
        lo = c * RET_DK
        r = _mm(hb, w_ref[:, lo:lo + RET_DK], precise)
        x1 = r[:, :half]
        x2 = r[:, half:]
        sc = RET_DK ** -0.5 if c < RET_HEADS else 1.0
        o_ref[:, lo:lo + half] = (x1 * cos - x2 * sin) * sc
        o_ref[:, lo + half:lo + RET_DK] = (x1 * sin + x2 * cos) * sc
    base = 2 * RET_HEADS * RET_DK
    for c in range(2 * RET_HEADS):
        lo = base + c * RET_DV
        o_ref[:, lo:lo + RET_DV] = _mm(hb, w_ref[:, lo:lo + RET_DV], precise)


def _inproj_ret(x, mods, gain, w, cos, sin, tiles_per_seq):
    t = x.shape[0]
    n = w.shape[1]
    tm = min(TM, t)
    pos_tiles = cos.shape[0] // tm
    return pl.pallas_call(
        functools.partial(_inproj_ret_body, precise=w.dtype == F32),
        grid=(t // tm,),
        in_specs=[pl.BlockSpec((tm, D_MODEL), lambda i: (i, 0)),
                  _mod_spec(mods, tiles_per_seq),
                  pl.BlockSpec((1, D_MODEL), lambda i: (0, 0)),
                  pl.BlockSpec((D_MODEL, n), lambda i: (0, 0), pipeline_mode=pl.Buffered(1)),
                  pl.BlockSpec((tm, RET_DK // 2), lambda i: (i % pos_tiles, 0)),
                  pl.BlockSpec((tm, RET_DK // 2), lambda i: (i % pos_tiles, 0))],
        out_specs=pl.BlockSpec((tm, n), lambda i: (i, 0)),
        out_shape=jax.ShapeDtypeStruct((t, n), F32),
        compiler_params=_cparams(("arbitrary",)),
        name="inproj_ret",
    )(x, mods, gain, w, cos, sin)


def _split3(c):
    hi = c.astype(BF16).astype(F32)
    r = c - hi
    mid = r.astype(BF16).astype(F32)
    return hi, mid, r - mid


def _emit_attention_operands(o_ref, cs, qa_ref, ka_ref, va_ref):
    tm = cs.shape[0]
    fw = FOX_HEADS * FOX_HD
    lane = lax.broadcasted_iota(I32, (tm, LANES), 1)
    low = lane < FOX_HD
    ones_q = jnp.where((lane >= FOX_HD + 3) & (lane < FOX_HD + 6), 1.0, 0.0)
    ones_k = jnp.where((lane >= FOX_HD) & (lane < FOX_HD + 3), 1.0, 0.0)
    ones_v = jnp.where(lane == FOX_HD, 1.0, 0.0)
    for p in range(FOX_HEADS // 2):
        pairs = [o_ref[:, s * fw + p * LANES:s * fw + (p + 1) * LANES] for s in range(3)]
        pairs[0] = pairs[0] * (FOX_HD ** -0.5)
        for a in range(2):
            hh = 2 * p + a
            qh, kh, vh = pairs if a == 0 else [pltpu.roll(x, FOX_HD, 1) for x in pairs]
            c = jnp.sum(jnp.where(lane == hh, cs, 0.0), axis=1, keepdims=True)
            hi, mid, lo = _split3(c)
            fq = jnp.where(lane == FOX_HD, hi, jnp.where(lane == FOX_HD + 1, mid,
                                                         jnp.where(lane == FOX_HD + 2, lo, ones_q)))
            fk = jnp.where(lane == FOX_HD + 3, -hi, jnp.where(lane == FOX_HD + 4, -mid,
                                                              jnp.where(lane == FOX_HD + 5, -lo, ones_k)))
            cols = slice(hh * LANES, (hh + 1) * LANES)
            qa_ref[:, cols] = jnp.where(low, qh, fq).astype(BF16)
            ka_ref[:, cols] = jnp.where(low, kh, fk).astype(BF16)
            va_ref[:, cols] = jnp.where(low, vh, ones_v).astype(BF16)


def _inproj_fox_body(x_ref, m_ref, g_ref, w_ref, wfl_ref, bf_ref, tri_ref, *rest, tiles_per_seq, prompt):
    if prompt:
        wkt_ref, wvt_ref, gate_ref, lf_ref, qa_ref, ka_ref, va_ref, kt_ref, vt_ref, stage, carry = rest
    else:
        stage, lf_ref, cum_ref = rest
    i = pl.program_id(0)
    tm = x_ref.shape[0]
    fw = FOX_HEADS * FOX_HD
    h = _rms(x_ref[...], g_ref[...]) * (1.0 + m_ref[1, 0]) + m_ref[0, 0]
    hb = h.astype(BF16)
    cw = 512
    for c in range(4 * fw // cw):
        r = _dot(hb, w_ref[:, c * cw:(c + 1) * cw])
        if prompt and c * cw >= 3 * fw:
            gate_ref[:, c * cw - 3 * fw:(c + 1) * cw - 3 * fw] = r
        else:
            stage[:, c * cw:(c + 1) * cw] = r
    fl = _dot(h, wfl_ref[...], HI) + bf_ref[...]
    lf = jnp.minimum(fl, 0.0) - jnp.log1p(jnp.exp(-jnp.abs(fl)))
    lf_ref[...] = lf
    cs = _dot(tri_ref[...], lf, HI)
    if not prompt:
        cum_ref[...] = cs
        return

    @pl.when(i % tiles_per_seq == 0)
    def _():
        carry[...] = jnp.zeros_like(carry)
    cs = cs + carry[...]
    carry[...] = cs[tm - 1:tm, :]
    _emit_attention_operands(stage, cs, qa_ref, ka_ref, va_ref)
    kt_ref[0] = _dot_nt(wkt_ref[...], hb)
    vt_ref[0] = _dot_nt(wvt_ref[...], hb)


def _inproj_fox(x, mods, gain, w_bf, w_fl, b_fl, tri, tiles_per_seq, wkt=None, wvt=None):
    t = x.shape[0]
    n = w_bf.shape[1]
    tm = min(TM, t)
    prompt = wkt is not None
    body = functools.partial(_inproj_fox_body, tiles_per_seq=tiles_per_seq, prompt=prompt)
    fw = FOX_HEADS * FOX_HD
    wide = FOX_HEADS * LANES
    row = lambda width: pl.BlockSpec((tm, width), lambda i: (i, 0))
    in_specs = [row(D_MODEL),
                _mod_spec(mods, tiles_per_seq),
                pl.BlockSpec((1, D_MODEL), lambda i: (0, 0)),
                pl.BlockSpec((D_MODEL, n), lambda i: (0, 0), pipeline_mode=pl.Buffered(1)),
                pl.BlockSpec((D_MODEL, LANES), lambda i: (0, 0)),
                pl.BlockSpec((1, LANES), lambda i: (0, 0)),
                pl.BlockSpec((tm, tm), lambda i: (0, 0))]
    args = [x, mods, gain, w_bf, w_fl, b_fl, tri]
    if prompt:
        n_seq = t // (tiles_per_seq * tm)
        tspec = pl.BlockSpec((1, fw, tm), lambda i: (i // tiles_per_seq, 0, i % tiles_per_seq))
        in_specs += [pl.BlockSpec((fw, D_MODEL), lambda i: (0, 0), pipeline_mode=pl.Buffered(1))] * 2
        args += [wkt, wvt]
        out_specs = [row(fw), row(LANES), row(wide), row(wide), row(wide), tspec, tspec]
        out_shape = ([jax.ShapeDtypeStruct((t, fw), F32), jax.ShapeDtypeStruct((t, LANES), F32)]
                     + [jax.ShapeDtypeStruct((t, wide), BF16)] * 3
                     + [jax.ShapeDtypeStruct((n_seq, fw, tiles_per_seq * tm), F32)] * 2)
        scratch = [pltpu.VMEM((tm, 3 * fw), F32), pltpu.VMEM((1, LANES), F32)]
    else:
        out_specs = [row(n), row(LANES), row(LANES)]
        out_shape = [jax.ShapeDtypeStruct((t, n), F32), jax.ShapeDtypeStruct((t, LANES), F32),
                     jax.ShapeDtypeStruct((t, LANES), F32)]
        scratch = []
    return pl.pallas_call(
        body,
        grid=(t // tm,),
        in_specs=in_specs,
        out_specs=out_specs,
        out_shape=out_shape,
        scratch_shapes=scratch,
        compiler_params=_cparams(("arbitrary",)),
        name="inproj_fox",
    )(*args)


def _retention_body(q_ref, k_ref, v_ref, g_ref, s0_ref, din_ref, dq_ref, dk_ref, dc_ref, gn_ref,
                    u_ref, so_ref, state, *, n_chunks, kv_rows, precise):
    c = pl.program_id(1)

    @pl.when(c == 0)
    def _():
        state[...] = s0_ref[0]

    for hh in range(RET_HEADS):
        qk = slice(hh * RET_DK, (hh + 1) * RET_DK)
        vg = slice(hh * RET_DV, (hh + 1) * RET_DV)
        q = q_ref[:, qk]
        k = k_ref[:, qk]
        v = v_ref[:, vg]
        rows = k.shape[0]
        if rows < kv_rows:
            k = jnp.concatenate([k, jnp.zeros((kv_rows - rows, RET_DK), F32)], axis=0)
            v = jnp.concatenate([v, jnp.zeros((kv_rows - rows, RET_DV), F32)], axis=0)
        kd = k * dk_ref[hh]
        s_old = state[hh]
        inner = _mm(q, k, precise, _NT) * din_ref[hh]
        o = _mm(inner, v, precise) + _mm(q, s_old, precise) * dq_ref[hh]
        s_new = s_old * dc_ref[hh] + _mm(kd, v, False, _TN)
        state[hh] = s_new
        g = g_ref[:, vg]
        u_ref[:, vg] = _rms(o, gn_ref[:, vg]) * (g * jax.nn.sigmoid(g))

        @pl.when(c == n_chunks - 1)
        def _(hh=hh, s_new=s_new):
            so_ref[0, hh] = s_new


def _retention(proj, s0, din, dq, dk, dc, gn_gain, n_seq, n_chunks, q_rows, precise):
    t = proj.shape[0]
    kv_rows = din.shape[2]
    h = RET_HEADS
    hk, hv = h * RET_DK, h * RET_DV
    body = functools.partial(_retention_body, n_chunks=n_chunks, kv_rows=kv_rows, precise=precise)
    row = lambda b, c: b * n_chunks + c
    whole = lambda shape: pl.BlockSpec(shape, lambda b, c: (0,) * len(shape))
    return pl.pallas_call(
        body,
        grid=(n_seq, n_chunks),
        in_specs=[pl.BlockSpec((q_rows, hk), lambda b, c: (row(b, c), 0)),
                  pl.BlockSpec((q_rows, hk), lambda b, c: (row(b, c), 1)),
                  pl.BlockSpec((q_rows, hv), lambda b, c: (row(b, c), 2 * hk // hv)),
                  pl.BlockSpec((q_rows, hv), lambda b, c: (row(b, c), 2 * hk // hv + 1)),
                  pl.BlockSpec((1, h, RET_DK, RET_DV), lambda b, c: (b, 0, 0, 0)),
                  whole((h, q_rows, kv_rows)), whole((h, q_rows, 1)), whole((h, kv_rows, 1)), whole((h, 1, 1)),
                  whole((1, hv))],
        out_specs=[pl.BlockSpec((q_rows, hv), lambda b, c: (row(b, c), 0)),
                   pl.BlockSpec((1, h, RET_DK, RET_DV), lambda b, c: (b, 0, 0, 0))],
        out_shape=[jax.ShapeDtypeStruct((t, hv), F32),
                   jax.ShapeDtypeStruct((n_seq, h, RET_DK, RET_DV), F32)],
        scratch_shapes=[pltpu.VMEM((h, RET_DK, RET_DV), F32)],
        compiler_params=_cparams(("arbitrary", "arbitrary")),
        name="retention",
    )(proj, proj, proj, proj, s0, din, dq, dk, dc, gn_gain)


def _ret_decay_tables(c, q_rows, kv_rows):
    lg = jnp.log(1.0 - 2.0 ** (-5.0 - jnp.arange(RET_HEADS, dtype=F32)))
    idx = jnp.arange(c, dtype=F32)
    rel = idx[:, None] - idx[None, :]
    din = jnp.where(rel[None] >= 0, jnp.exp(lg[:, None, None] * jnp.maximum(rel, 0.0)[None]), 0.0)
    dq = jnp.exp(lg[:, None] * (idx[None, :] + 1.0))
    dk = jnp.exp(lg[:, None] * (c - 1.0 - idx[None, :]))
    dc = jnp.exp(lg * c)
    din = jnp.pad(din, ((0, 0), (0, q_rows - c), (0, kv_rows - c)))
    dq = jnp.pad(dq, ((0, 0), (0, q_rows - c)))[..., None]
    dk = jnp.pad(dk, ((0, 0), (0, kv_rows - c)))[..., None]
    return din, dq, dk, dc[:, None, None]


def _outproj_body(u_ref, w_ref, x_ref, m_ref, gpost_ref, gpre_ref, xn_ref, h2_ref):
    tm = x_ref.shape[0]
    y = _mm(u_ref[...], w_ref[...], w_ref.dtype == F32)
    xn = x_ref[...] + m_ref[2, 0] * _rms(y, gpost_ref[...])
    xn_ref[...] = xn
    h2 = _rms(xn, gpre_ref[...]) * (1.0 + m_ref[4, 0]) + m_ref[3, 0]
    _store_rows(h2_ref, h2, tm)


def _outproj(u, w, x, mods, gpost, gpre, tiles_per_seq):
    t = x.shape[0]
    din = u.shape[1]
    tm = min(TM, t)
    return pl.pallas_call(
        _outproj_body,
        grid=(t // tm,),
        in_specs=[pl.BlockSpec((tm, din), lambda i: (i, 0)),
                  pl.BlockSpec((din, D_MODEL), lambda i: (0, 0), pipeline_mode=pl.Buffered(1)),
                  pl.BlockSpec((tm, D_MODEL), lambda i: (i, 0)),
                  _mod_spec(mods, tiles_per_seq),
                  pl.BlockSpec((1, D_MODEL), lambda i: (0, 0)),
                  pl.BlockSpec((1, D_MODEL), lambda i: (0, 0))],
        out_specs=[pl.BlockSpec((tm, D_MODEL), lambda i: (i, 0)),
                   pl.BlockSpec((tm * ROW_TILES, LANES), lambda i: (i, 0))],
        out_shape=[jax.ShapeDtypeStruct((t, D_MODEL), F32),
                   jax.ShapeDtypeStruct((t * ROW_TILES, LANES), F32)],
        compiler_params=_cparams(("arbitrary",)),
        name="outproj",
    )(u, w, x, mods, gpost, gpre)


def _router_body(h_ref, w_ref, b_ref, tri_ref, idx_ref, wgt_ref, rank_ref, cnt_ref, carry):
    i = pl.program_id(0)
    tm = idx_ref.shape[0]

    @pl.when(i == 0)
    def _():
        carry[...] = jnp.zeros_like(carry)

    h = _load_rows(h_ref, tm)
    logits = _dot(h, w_ref[...], HI) + b_ref[...]
    lane = lax.broadcasted_iota(I32, (tm, LANES), 1)
    lane_f = lane.astype(F32)
    work = logits
    vals, ids, hots = [], [], []
    for _ in range(TOP_K):
        mx = jnp.max(work, axis=1, keepdims=True)
        ik = jnp.min(jnp.where(work == mx, lane_f, float(LANES)), axis=1, keepdims=True)
        hot = lane_f == ik
        vals.append(mx)
        ids.append(ik.astype(I32))
        hots.append(hot)
        work = jnp.where(hot, -jnp.inf, work)
    ex = [jnp.exp(v - vals[0]) for v in vals]
    den = ex[0] + ex[1] + ex[2] + ex[3]
    chosen = hots[0] | hots[1] | hots[2] | hots[3]
    before = _dot(tri_ref[...], chosen.astype(BF16)) + carry[...]
    idx_o = jnp.zeros((tm, LANES), I32)
    wgt_o = jnp.zeros((tm, LANES), F32)
    rank_o = jnp.zeros((tm, LANES), I32)
    for kk in range(TOP_K):
        rk = jnp.sum(jnp.where(hots[kk], before, 0.0), axis=1, keepdims=True).astype(I32)
        idx_o = jnp.where(lane == kk, ids[kk], idx_o)
        wgt_o = jnp.where(lane == kk, ex[kk] / den, wgt_o)
        rank_o = jnp.where(lane == kk, rk, rank_o)
    idx_ref[...] = idx_o
    wgt_ref[...] = wgt_o
    rank_ref[...] = rank_o
    carry[...] = carry[...] + jnp.sum(chosen.astype(F32), axis=0, keepdims=True)
    cnt_ref[...] = carry[...]


def _router(h2, w_r, b_r, tri):
    t = h2.shape[0] // ROW_TILES
    tm = TM
    return pl.pallas_call(
        _router_body,
        grid=(t // tm,),
        in_specs=[pl.BlockSpec((tm * ROW_TILES, LANES), lambda i: (i, 0)),
                  pl.BlockSpec((D_MODEL, LANES), lambda i: (0, 0)),
                  pl.BlockSpec((1, LANES), lambda i: (0, 0)),
                  pl.BlockSpec((tm, tm), lambda i: (0, 0))],
        out_specs=[pl.BlockSpec((tm, LANES), lambda i: (i, 0)),
                   pl.BlockSpec((tm, LANES), lambda i: (i, 0)),
                   pl.BlockSpec((tm, LANES), lambda i: (i, 0)),
                   pl.BlockSpec((1, LANES), lambda i: (0, 0))],
        out_shape=[jax.ShapeDtypeStruct((t, LANES), I32),
                   jax.ShapeDtypeStruct((t, LANES), F32),
                   jax.ShapeDtypeStruct((t, LANES), I32),
                   jax.ShapeDtypeStruct((1, LANES), F32)],
        scratch_shapes=[pltpu.VMEM((1, LANES), F32)],
        compiler_params=_cparams(("arbitrary",)),
        name="router",
    )(h2, w_r, b_r, tri)


def _token_copy(src, src_row, dst, dst_row, sem):
    return pltpu.make_async_copy(src.at[pl.ds(pl.multiple_of(src_row * ROW_TILES, ROW_TILES), ROW_TILES), :],
                                 dst.at[pl.ds(pl.multiple_of(dst_row * ROW_TILES, ROW_TILES), ROW_TILES), :], sem)


def _rows_copy(src, dst, n_rows, sem):
    return pltpu.make_async_copy(src.at[pl.ds(0, n_rows * ROW_TILES), :], dst.at[pl.ds(0, n_rows * ROW_TILES), :], sem)


INVERT_UNROLL = 8


def _invert_body(dest_ref, lo_ref, hi_ref, inv_ref):
    def fill_group(g, carry):
        def fill(r, c):
            inv_ref[r] = 0
            return c
        return lax.fori_loop(lo_ref[g], hi_ref[g], fill, carry)

    lax.fori_loop(0, lo_ref.shape[0], fill_group, 0)

    def step(g, carry):
        for u in range(INVERT_UNROLL):
            p = g * INVERT_UNROLL + u
            inv_ref[dest_ref[p]] = p
        return carry

    lax.fori_loop(0, dest_ref.shape[0] // INVERT_UNROLL, step, 0)


def _invert(dest_flat, pad_lo, pad_hi, n_rows):
    smem = pl.BlockSpec(memory_space=pltpu.SMEM)
    return pl.pallas_call(
        _invert_body,
        in_specs=[smem, smem, smem],
        out_specs=smem,
        out_shape=jax.ShapeDtypeStruct((n_rows,), I32),
        name="invert",
    )(dest_flat, pad_lo, pad_hi)


def _experts_body(te_ref, nv_ref, na_ref, inv_ref, invn_ref, h_ref, wgu_ref, bgu_ref, wdn_ref, bdn_ref, y_ref,
                  xbuf, ybuf, wgu_bf, wdn_bf, gsem, ssem):
    i = pl.program_id(0)
    tm = TM_FFN
    n_act = na_ref[0]
    slot = i % 2

    def gather(tile_inv, n, s):
        def issue(r, carry):
            _token_copy(h_ref, tile_inv[0, 0, r] >> 2, xbuf.at[s], r, gsem.at[s]).start()
            return carry
        lax.fori_loop(0, n, issue, 0)

    @pl.when(i == 0)
    def _():
        xbuf[...] = jnp.zeros_like(xbuf)
        gather(inv_ref, nv_ref[0], 0)

    @pl.when(i < n_act)
    def _():
        prev = te_ref[jnp.maximum(i - 1, 0)]

        @pl.when((i == 0) | (te_ref[i] != prev))
        def _():
            wgu_bf[...] = wgu_ref[0, 0].astype(BF16)
            wdn_bf[...] = wdn_ref[0, 0].astype(BF16)

        _rows_copy(h_ref, xbuf.at[slot], nv_ref[i], gsem.at[slot]).wait()

        @pl.when(i + 1 < n_act)
        def _():
            gather(invn_ref, nv_ref[i + 1], 1 - slot)

        @pl.when(i >= 2)
        def _():
            _rows_copy(ybuf.at[slot], y_ref, nv_ref[jnp.maximum(i - 2, 0)], ssem.at[slot]).wait()

        x = _load_rows(xbuf.at[slot], tm).astype(BF16)
        gu = _dot(x, wgu_bf[...]) + bgu_ref[0, 0]
        gate = jnp.minimum(gu[:, :D_FF], SWIGLU_LIMIT)
        up = jnp.clip(gu[:, D_FF:], -SWIGLU_LIMIT, SWIGLU_LIMIT)
        act = (up + 1.0) * gate * jax.nn.sigmoid(SWIGLU_ALPHA * gate)
        y = _dot(act.astype(BF16), wdn_bf[...]) + bdn_ref[0, 0]
        _store_rows(ybuf.at[slot], y, tm)

        def scatter(r, carry):
            _token_copy(ybuf.at[slot], r, y_ref, inv_ref[0, 0, r], ssem.at[slot]).start()
            return carry
        lax.fori_loop(0, nv_ref[i], scatter, 0)

        @pl.when(i == n_act - 1)
        def _():
            _rows_copy(ybuf.at[slot], y_ref, nv_ref[i], ssem.at[slot]).wait()

            @pl.when(i >= 1)
            def _():
                _rows_copy(ybuf.at[1 - slot], y_ref, nv_ref[jnp.maximum(i - 1, 0)], ssem.at[1 - slot]).wait()


def _experts(layer, tile_expert, n_valid, n_active, inv3, h2, w_gu, b_gu, w_dn, b_dn):
    n_tiles = inv3.shape[0]
    tm = TM_FFN
    n_pairs = (h2.shape[0] // ROW_TILES) * TOP_K
    b_gu4 = b_gu.reshape(b_gu.shape[0], N_EXPERTS, 1, 2 * D_FF)
    b_dn4 = b_dn.reshape(b_dn.shape[0], N_EXPERTS, 1, D_MODEL)
    grid_spec = pltpu.PrefetchScalarGridSpec(
        num_scalar_prefetch=3,
        grid=(n_tiles,),
        in_specs=[pl.BlockSpec((1, 1, tm), lambda i, te, nv, na: (i, 0, 0), memory_space=pltpu.SMEM),
                  pl.BlockSpec((1, 1, tm), lambda i, te, nv, na: (jnp.minimum(i + 1, n_tiles - 1), 0, 0),
                               memory_space=pltpu.SMEM),
                  pl.BlockSpec(memory_space=pl.ANY),
                  pl.BlockSpec((1, 1, D_MODEL, 2 * D_FF), lambda i, te, nv, na: (layer, te[i], 0, 0)),
                  pl.BlockSpec((1, 1, 1, 2 * D_FF), lambda i, te, nv, na: (layer, te[i], 0, 0)),
                  pl.BlockSpec((1, 1, D_FF, D_MODEL), lambda i, te, nv, na: (layer, te[i], 0, 0)),
                  pl.BlockSpec((1, 1, 1, D_MODEL), lambda i, te, nv, na: (layer, te[i], 0, 0))],
        out_specs=pl.BlockSpec(memory_space=pl.ANY),
        scratch_shapes=[pltpu.VMEM((2, tm * ROW_TILES, LANES), F32), pltpu.VMEM((2, tm * ROW_TILES, LANES), F32),
                        pltpu.VMEM((D_MODEL, 2 * D_FF), BF16), pltpu.VMEM((D_FF, D_MODEL), BF16),
                        pltpu.SemaphoreType.DMA((2,)), pltpu.SemaphoreType.DMA((2,))],
    )
    return pl.pallas_call(
        _experts_body,
        grid_spec=grid_spec,
        out_shape=jax.ShapeDtypeStruct((n_pairs * ROW_TILES, LANES), F32),
        compiler_params=_cparams(("arbitrary",)),
        name="experts",
    )(tile_expert, n_valid, n_active, inv3, inv3, h2, w_gu, b_gu4, w_dn, b_dn4)


def _combine_body(y_ref, wgt_ref, x_ref, m_ref, g_ref, xo_ref):
    tm = x_ref.shape[0]
    w = wgt_ref[...]
    f = jnp.zeros((tm, D_MODEL), F32)
    for kk in range(TOP_K):
        yk = jnp.concatenate([y_ref[pl.ds(kk * ROW_TILES + s, tm, stride=TOP_K * ROW_TILES), :]
                              for s in range(ROW_TILES)], axis=1)
        f = f + w[:, kk:kk + 1] * yk
    xo_ref[...] = x_ref[...] + m_ref[5, 0] * _rms(f, g_ref[...])


def _combine(y, first_tile, wgt, x, mods, gpost, tiles_per_seq):
    t = x.shape[0]
    tm = TM_COMB
    if mods.shape[2] == 1:
        mspec = pl.BlockSpec((6, 1, 1, D_MODEL), lambda i: (0, i // tiles_per_seq, 0, 0))
    else:
        mspec = pl.BlockSpec((6, 1, tm, D_MODEL), lambda i: (0, 0, i, 0))
    return pl.pallas_call(
        _combine_body,
        grid=(t // tm,),
        in_specs=[pl.BlockSpec((tm * TOP_K * ROW_TILES, LANES), lambda i: (first_tile + i, 0)),
                  pl.BlockSpec((tm, LANES), lambda i: (i, 0)),
                  pl.BlockSpec((tm, D_MODEL), lambda i: (i, 0)),
                  mspec,
                  pl.BlockSpec((1, D_MODEL), lambda i: (0, 0))],
        out_specs=pl.BlockSpec((tm, D_MODEL), lambda i: (i, 0)),
        out_shape=jax.ShapeDtypeStruct((t, D_MODEL), F32),
        compiler_params=_cparams(("arbitrary",)),
        name="combine",
    )(y, wgt, x, mods, gpost)


def _moe(layer, h2_all, t_prompt, x_p, x_s, mods_p, mods_s, gpost, w_r, b_r, tri_strict,
         w_gu, b_gu, w_dn, b_dn, tiles_per_seq_comb):
    t_all = h2_all.shape[0] // ROW_TILES
    idx_w, wgt_w, rank_w, cnt_w = _router(h2_all, w_r, b_r, tri_strict)
    idx = idx_w[:, :TOP_K]
    rank = rank_w[:, :TOP_K]
    cnt = cnt_w[0, :N_EXPERTS].astype(I32)
    padded = ((cnt + TM_FFN - 1) // TM_FFN) * TM_FFN
    ends = jnp.cumsum(padded)
    offs = ends - padded
    dest = jnp.sum(jnp.where(idx[..., None] == jnp.arange(N_EXPERTS, dtype=I32), offs, 0), axis=-1) + rank
    n_tiles = (t_all * TOP_K) // TM_FFN + N_EXPERTS
    tile_start = jnp.arange(n_tiles, dtype=I32) * TM_FFN
    tile_expert = jnp.minimum(jnp.sum((tile_start[:, None] >= ends[None, :]).astype(I32), axis=1), N_EXPERTS - 1)
    n_valid = jnp.clip(offs[tile_expert] + cnt[tile_expert] - tile_start, 0, TM_FFN).astype(I32)
    n_active = (ends[-1] // TM_FFN).astype(I32).reshape(1)

    n_rows = n_tiles * TM_FFN
    pad_lo = jnp.concatenate([offs + cnt, ends[-1:]]).astype(I32)
    pad_hi = jnp.concatenate([ends, jnp.full((1,), n_rows, I32)]).astype(I32)
    inv = _invert(dest.reshape(-1), pad_lo, pad_hi, n_rows)
    y = _experts(layer, tile_expert, n_valid, n_active, inv.reshape(n_tiles, 1, TM_FFN), h2_all,
                 w_gu, b_gu, w_dn, b_dn)
    xo_p = _combine(y, 0, wgt_w[:t_prompt], x_p, mods_p, gpost, tiles_per_seq_comb)
    xo_s = _combine(y, t_prompt // TM_COMB, wgt_w[t_prompt:], x_s, mods_s, gpost, 1)
    return xo_p, xo_s


def _fox_prompt_body(qa_ref, ka_ref, va_ref, g_ref, o_ref, m_s, acc_s):
    i = pl.program_id(2)
    tq = qa_ref.shape[0]
    m_s[...] = jnp.full_like(m_s, -jnp.inf)
    acc_s[...] = jnp.zeros_like(acc_s)

    def block(start, diagonal):
        if diagonal:
            causal = lax.broadcasted_iota(I32, (tq, tq), 1) <= lax.broadcasted_iota(I32, (tq, tq), 0)
        for a in range(2):
            cols = slice(a * LANES, (a + 1) * LANES)
            s = _dot_nt(qa_ref[:, cols], ka_ref[pl.ds(start, tq), cols])
            if diagonal:
                s = jnp.where(causal, s, -jnp.inf)
            m_old = m_s[a]
            m_new = jnp.maximum(m_old, jnp.max(s, axis=1, keepdims=True))
            p = jnp.exp((s - m_new).astype(BF16))
            acc_s[a] = jnp.exp(m_old - m_new) * acc_s[a] + _dot(p, va_ref[pl.ds(start, tq), cols])
            m_s[a] = m_new

    def body(j, carry):
        block(pl.multiple_of(j * tq, tq), False)
        return carry

    lax.fori_loop(0, i, body, 0)
    block(pl.multiple_of(i * tq, tq), True)
    lane = lax.broadcasted_iota(I32, (tq, LANES), 1)
    outs = [acc_s[a] / acc_s[a][:, FOX_HD:FOX_HD + 1] for a in range(2)]
    o = jnp.where(lane < FOX_HD, outs[0], pltpu.roll(outs[1], FOX_HD, 1))
    o_ref[...] = o * jax.nn.sigmoid(g_ref[...])


def _fox_prompt_attn(qa, ka, va, gate, n_seq, seq):
    t = gate.shape[0]
    nq = seq // TQ
    pairs = FOX_HEADS // 2
    return pl.pallas_call(
        _fox_prompt_body,
        grid=(n_seq, pairs, nq),
        in_specs=[pl.BlockSpec((TQ, 2 * LANES), lambda b, hp, i: (b * nq + i, hp)),
                  pl.BlockSpec((seq, 2 * LANES), lambda b, hp, i: (b, hp)),
                  pl.BlockSpec((seq, 2 * LANES), lambda b, hp, i: (b, hp)),
                  pl.BlockSpec((TQ, LANES), lambda b, hp, i: (b * nq + i, hp))],
        out_specs=pl.BlockSpec((TQ, LANES), lambda b, hp, i: (b * nq + i, hp)),
        out_shape=jax.ShapeDtypeStruct((t, FOX_HEADS * FOX_HD), F32),
        scratch_shapes=[pltpu.VMEM((2, TQ, 1), F32), pltpu.VMEM((2, TQ, LANES), F32)],
        compiler_params=_cparams(("arbitrary", "arbitrary", "arbitrary")),
        name="fox_prompt_attn",
    )(qa, ka, va, gate)


def _fox_sample_body(pt_ref, q_ref, kn_ref, vn_ref, g_ref, cncol_ref, cnrow_ref, *rest):
    npg = PAGES_PER_STEP
    k_refs = rest[:npg]
    v_refs = rest[npg:2 * npg]
    f_refs = rest[2 * npg:3 * npg]
    u_ref, e_ref, o_ref, qbd_s, m_s, l_s, acc_s, car_s = rest[3 * npg:]
    del pt_ref
    j = pl.program_id(1)
    nj = pl.num_programs(1)
    rows = FOX_HEADS * TOK_PAD
    wide = FOX_HEADS * FOX_HD
    own = ((lax.broadcasted_iota(I32, (rows, wide), 0) // TOK_PAD)
           == (lax.broadcasted_iota(I32, (rows, wide), 1) // FOX_HD))

    @pl.when(j == 0)
    def _():
        m_s[...] = jnp.full_like(m_s, -jnp.inf)
        l_s[...] = jnp.zeros_like(l_s)
        acc_s[...] = jnp.zeros_like(acc_s)
        car_s[...] = jnp.zeros_like(car_s)
        q = q_ref[...] * (FOX_HD ** -0.5)
        qbd_s[...] = jnp.where(own, jnp.concatenate([q] * FOX_HEADS, axis=0), 0.0).astype(BF16)

    qbd = qbd_s[...]
    cn = cncol_ref[0]

    def update(s, pv_of):
        m_old = m_s[...]
        m_new = jnp.maximum(m_old, jnp.max(s, axis=1, keepdims=True))
        alpha = jnp.exp(m_old - m_new)
        p = jnp.exp(s - m_new)
        l_s[...] = alpha * l_s[...] + jnp.sum(p, axis=1, keepdims=True)
        acc_s[...] = alpha * acc_s[...] + pv_of(p.astype(BF16))
        m_s[...] = m_new

    suffixes = []
    for r in range(npg):
        ft = f_refs[r][0]
        suffixes.append(_dot(ft, u_ref[...], HI) + car_s[...])
        car_s[...] = car_s[...] + jnp.sum(ft, axis=1, keepdims=True)
    bias = _dot(e_ref[...], jnp.concatenate(suffixes, axis=1), HI) + cn
    kt = jnp.concatenate([k_refs[r][0].astype(BF16) for r in range(npg)], axis=1)
    vt = jnp.concatenate([v_refs[r][0].astype(BF16) for r in range(npg)], axis=1)
    update(_dot(qbd, kt) + bias, lambda pb: _dot_nt(pb, vt))

    @pl.when(j == nj - 1)
    def _():
        zpad = jnp.zeros((PAGE_SIZE - TOK_PAD, wide), F32)
        kn = jnp.concatenate([kn_ref[...], zpad], axis=0).astype(BF16)
        vn = jnp.concatenate([vn_ref[...], zpad], axis=0).astype(BF16)
        trow = lax.broadcasted_iota(I32, (rows, PAGE_SIZE), 0) & (TOK_PAD - 1)
        tcol = lax.broadcasted_iota(I32, (rows, PAGE_SIZE), 1)
        s = jnp.where(tcol <= trow, _dot_nt(qbd, kn) + cn - cnrow_ref[0], -jnp.inf)
        update(s, lambda pb: _dot(pb, vn))
        o = jnp.where(own, acc_s[...] / l_s[...], 0.0)
        o8 = o[0:TOK_PAD, :]
        for h in range(1, FOX_HEADS):
            o8 = o8 + o[h * TOK_PAD:(h + 1) * TOK_PAD, :]
        o_ref[...] = o8 * jax.nn.sigmoid(g_ref[...])


def _fox_sample_attn(proj, cncol, cnrow, cache_k, cache_v, cache_ft, page_table, u_mat, e_mat):
    n_seq, n_pages = page_table.shape
    npg = PAGES_PER_STEP
    steps = n_pages // npg
    wide = FOX_HEADS * FOX_HD
    rows = FOX_HEADS * TOK_PAD

    def page(r):
        return lambda b, j, pt: pt[b * n_pages + (n_pages - 1 - (j * npg + r))]

    kv_specs = [pl.BlockSpec((1, wide, PAGE_SIZE), (lambda b, j, pt, r=r: (page(r)(b, j, pt), 0, 0)))
                for r in range(npg)]
    f_specs = [pl.BlockSpec((1, FOX_HEADS, PAGE_SIZE), (lambda b, j, pt, r=r: (page(r)(b, j, pt), 0, 0)))
               for r in range(npg)]
    grid_spec = pltpu.PrefetchScalarGridSpec(
        num_scalar_prefetch=1,
        grid=(n_seq, steps),
        in_specs=[pl.BlockSpec((TOK_PAD, wide), lambda b, j, pt: (b, 0)),
                  pl.BlockSpec((TOK_PAD, wide), lambda b, j, pt: (b, 1)),
                  pl.BlockSpec((TOK_PAD, wide), lambda b, j, pt: (b, 2)),
                  pl.BlockSpec((TOK_PAD, wide), lambda b, j, pt: (b, 3)),
                  pl.BlockSpec((1, rows, 1), lambda b, j, pt: (b, 0, 0)),
                  pl.BlockSpec((1, rows, PAGE_SIZE), lambda b, j, pt: (b, 0, 0))]
                 + kv_specs + kv_specs + f_specs
                 + [pl.BlockSpec((PAGE_SIZE, PAGE_SIZE), lambda b, j, pt: (0, 0)),
                    pl.BlockSpec((rows, FOX_HEADS), lambda b, j, pt: (0, 0))],
        out_specs=pl.BlockSpec((TOK_PAD, wide), lambda b, j, pt: (b, 0)),
        scratch_shapes=[pltpu.VMEM((rows, wide), BF16), pltpu.VMEM((rows, 1), F32), pltpu.VMEM((rows, 1), F32),
                        pltpu.VMEM((rows, wide), F32), pltpu.VMEM((FOX_HEADS, 1), F32)],
    )
    return pl.pallas_call(
        _fox_sample_body,
        grid_spec=grid_spec,
        out_shape=jax.ShapeDtypeStruct((n_seq * TOK_PAD, wide), F32),
        compiler_params=_cparams(("arbitrary", "arbitrary")),
        name="fox_sample_attn",
    )(page_table.reshape(-1), proj, proj, proj, proj, cncol, cnrow,
      *([cache_k] * npg), *([cache_v] * npg), *([cache_ft] * npg), u_mat, e_mat)


def _rope_tables(pos):
    half = RET_DK // 2
    inv = ROPE_BASE ** (-jnp.arange(half, dtype=F32) / half)
    ang = pos.astype(F32)[:, None] * inv[None, :]
    return jnp.cos(ang), jnp.sin(ang)


def _mods_prompt(m):
    b = m.shape[0]
    return m.reshape(b, 6, D_MODEL).transpose(1, 0, 2)[:, :, None, :]


def _mods_sample(m):
    b = m.shape[0]
    mm = jnp.repeat(m.reshape(b, 6, D_MODEL), TOK_PAD, axis=0)
    return mm.transpose(1, 0, 2)[:, None, :, :]


def kernel(x_prompt, x_sample, c_prompt, c_sample, state_ret, cache_fox_k, cache_fox_v, cache_fox_logf, page_table, w_ada, b_ada, norm_pre_mix, norm_post_mix, norm_pre_ffn, norm_post_ffn, ret_w_in, ret_gn_gain, ret_w_out, fox_w_in, fox_b_f, fox_w_out, moe_w_router, moe_b_router, moe_w_gate_up, moe_b_gate_up, moe_w_down, moe_b_down):
    n_seq, seq, d = x_prompt.shape
    n_dec, dec_seq, _ = x_sample.shape
    t_p = n_seq * seq
    t_s = n_dec * TOK_PAD
    tps = seq // TM

    xp = x_prompt.reshape(t_p, d)
    xs = jnp.pad(x_sample, ((0, 0), (0, TOK_PAD - dec_seq), (0, 0))).reshape(t_s, d)

    m_all = _adaln(jnp.concatenate([c_prompt, c_sample], axis=0), w_ada, b_ada)

    ar = jnp.arange(TM)
    tri_incl = (ar[None, :] <= ar[:, None]).astype(F32)
    tri_strict = (ar[None, :] < ar[:, None]).astype(BF16)
    tri_group = ((ar[None, :] <= ar[:, None]) & (ar[None, :] // TOK_PAD == ar[:, None] // TOK_PAD)).astype(F32)

    mods_p = _mods_prompt(m_all[0, :n_seq])
    mods_s = _mods_sample(m_all[0, n_seq:])
    w_in = ret_w_in[0].astype(BF16)
    cos_p, sin_p = _rope_tables(jnp.arange(seq))
    pos_s = jnp.tile(PAST_LEN + jnp.arange(TOK_PAD), n_dec)
    cos_s, sin_s = _rope_tables(pos_s)
    g_pre = norm_pre_mix[0:1]
    proj_p = _inproj_ret(xp, mods_p, g_pre, w_in, cos_p, sin_p, tps)
    proj_s = _inproj_ret(xs, mods_s, g_pre, ret_w_in[0], cos_s, sin_s, 1)

    n_chunks = seq // RET_CHUNK
    din, dq, dk, dc = _ret_decay_tables(RET_CHUNK, RET_CHUNK, RET_CHUNK)
    s0p = jnp.zeros((n_seq, RET_HEADS, RET_DK, RET_DV), F32)
    u_p, st_p = _retention(proj_p, s0p, din, dq, dk, dc, ret_gn_gain, n_seq, n_chunks, RET_CHUNK, False)
    din, dq, dk, dc = _ret_decay_tables(dec_seq, TOK_PAD, RET_CHUNK)
    u_s, st_s = _retention(proj_s, state_ret[0], din, dq, dk, dc, ret_gn_gain, n_dec, 1, TOK_PAD, True)

    w_out = ret_w_out[0].astype(BF16)
    xp, h2p = _outproj(u_p, w_out, xp, mods_p, norm_post_mix[0:1], norm_pre_ffn[0:1], tps)
    xs, h2s = _outproj(u_s, ret_w_out[0], xs, mods_s, norm_post_mix[0:1], norm_pre_ffn[0:1], 1)

    def router_params(i):
        w_r = jnp.pad(moe_w_router[i], ((0, 0), (0, LANES - N_EXPERTS)))
        b_r = jnp.pad(moe_b_router[i], (0, LANES - N_EXPERTS), constant_values=-1e30)[None, :]
        return w_r, b_r

    w_r, b_r = router_params(0)
    xp, xs = _moe(0, jnp.concatenate([h2p, h2s], axis=0), t_p, xp, xs, mods_p, mods_s, norm_post_ffn[0:1],
                  w_r, b_r, tri_strict, moe_w_gate_up, moe_b_gate_up, moe_w_down, moe_b_down, seq // TM_COMB)

    mods_p = _mods_prompt(m_all[1, :n_seq])
    mods_s = _mods_sample(m_all[1, n_seq:])
    fw = FOX_HEADS * FOX_HD
    w_in = fox_w_in[0, :, :4 * fw].astype(BF16)
    w_fl = jnp.pad(fox_w_in[0, :, 4 * fw:], ((0, 0), (0, LANES - FOX_HEADS)))
    b_fl = jnp.pad(fox_b_f[0], (0, LANES - FOX_HEADS))[None, :]
    g_pre = norm_pre_mix[1:2]
    wkt = fox_w_in[0, :, fw:2 * fw].T.astype(BF16)
    wvt = fox_w_in[0, :, 2 * fw:3 * fw].T.astype(BF16)
    gate_p, lf_p, qa, ka, va, kt_p, vt_p = _inproj_fox(xp, mods_p, g_pre, w_in, w_fl, b_fl, tri_incl, tps, wkt, wvt)
    fproj_s, lf_s, cum_s = _inproj_fox(xs, mods_s, g_pre, w_in, w_fl, b_fl, tri_group, 1)
    o_p = _fox_prompt_attn(qa, ka, va, gate_p, n_seq, seq)

    cn = cum_s[:, :FOX_HEADS].reshape(n_dec, TOK_PAD, FOX_HEADS).transpose(0, 2, 1)
    cncol = cn.reshape(n_dec, FOX_HEADS * TOK_PAD, 1)
    cnrow = jnp.repeat(cn, TOK_PAD, axis=1)
    cnrow = jnp.pad(cnrow, ((0, 0), (0, 0), (0, PAGE_SIZE - TOK_PAD)))
    pr = jnp.arange(PAGE_SIZE)
    u_mat = (pr[:, None] > pr[None, :]).astype(F32)
    e_mat = (jnp.arange(FOX_HEADS * TOK_PAD)[:, None] // TOK_PAD == jnp.arange(FOX_HEADS)[None, :]).astype(F32)
    n_pool = cache_fox_k.shape[1]
    cache_kt = cache_fox_k[0].transpose(0, 2, 3, 1).reshape(n_pool, fw, PAGE_SIZE)
    cache_vt = cache_fox_v[0].transpose(0, 2, 3, 1).reshape(n_pool, fw, PAGE_SIZE)
    cache_ft = cache_fox_logf[0].transpose(0, 2, 1)
    o_s = _fox_sample_attn(fproj_s, cncol, cnrow, cache_kt, cache_vt, cache_ft, page_table, u_mat, e_mat)

    w_out = fox_w_out[0].astype(BF16)
    xp, h2p = _outproj(o_p, w_out, xp, mods_p, norm_post_mix[1:2], norm_pre_ffn[1:2], tps)
    xs, h2s = _outproj(o_s, w_out, xs, mods_s, norm_post_mix[1:2], norm_pre_ffn[1:2], 1)
    w_r, b_r = router_params(1)
    xp, xs = _moe(1, jnp.concatenate([h2p, h2s], axis=0), t_p, xp, xs, mods_p, mods_s, norm_post_ffn[1:2],
                  w_r, b_r, tri_strict, moe_w_gate_up, moe_b_gate_up, moe_w_down, moe_b_down, seq // TM_COMB)

    y_prompt = xp.reshape(n_seq, seq, d)
    y_sample = xs.reshape(n_dec, TOK_PAD, d)[:, :dec_seq]
    k_prompt = kt_p.reshape(n_seq, FOX_HEADS, FOX_HD, seq).transpose(0, 3, 1, 2)[None]
    v_prompt = vt_p.reshape(n_seq, FOX_HEADS, FOX_HD, seq).transpose(0, 3, 1, 2)[None]
    logf_prompt = lf_p[:, :FOX_HEADS].reshape(1, n_seq, seq, FOX_HEADS)
    fs = fproj_s.reshape(n_dec, TOK_PAD, 4 * fw)[:, :dec_seq]
    kv_shape_s = (1, n_dec, dec_seq, FOX_HEADS, FOX_HD)
    k_sample = fs[..., fw:2 * fw].reshape(kv_shape_s)
    v_sample = fs[..., 2 * fw:3 * fw].reshape(kv_shape_s)
    logf_sample = lf_s.reshape(n_dec, TOK_PAD, LANES)[:, :dec_seq, :FOX_HEADS][None]
    return (y_prompt, y_sample, st_p[None], st_s[None], k_prompt, v_prompt, logf_prompt,
            k_sample, v_sample, logf_sample)
```

```python
import functools

import jax
import jax.numpy as jnp
from jax import lax
from jax.experimental import pallas as pl
from jax.experimental.pallas import tpu as pltpu

F32 = jnp.float32
BF16 = jnp.bfloat16
I32 = jnp.int32
HI = lax.Precision.HIGHEST

D_MODEL = 1024
PAST_LEN = 8192
PAGE_SIZE = 128
RET_HEADS = 4
RET_DK = D_MODEL // RET_HEADS
RET_DV = 2 * RET_DK
RET_CHUNK = 128
ROPE_BASE = 10000.0
FOX_HEADS = 16
FOX_HD = D_MODEL // FOX_HEADS
N_EXPERTS = 32
TOP_K = 4
D_FF = D_MODEL
SWIGLU_LIMIT = 7.0
SWIGLU_ALPHA = 1.702
NORM_EPS = 1e-6

LANES = 128
SUBLANES = 8
ROW_TILES = D_MODEL // LANES
TOK_PAD = SUBLANES
TM = 256
TM_FFN = 512
TM_COMB = 128
TQ = 512
PAGES_PER_STEP = 8
VMEM_LIMIT = 56 * 1024 * 1024


def _cparams(sem, vmem=VMEM_LIMIT):
    return pltpu.CompilerParams(dimension_semantics=sem, vmem_limit_bytes=vmem)


def _rms(x, gain):
    return x * lax.rsqrt(jnp.mean(x * x, axis=-1, keepdims=True) + NORM_EPS) * gain


def _dot(a, b, precision=None):
    return jnp.dot(a, b, precision=precision, preferred_element_type=F32)


def _dot_nt(a, b):
    return lax.dot_general(a, b, (((1,), (1,)), ((), ())), preferred_element_type=F32)


def _mm(a, b, precise, dims=None):
    if precise:
        a, b, prec = a.astype(F32), b.astype(F32), HI
    else:
        a, b, prec = a.astype(BF16), b.astype(BF16), None
    if dims is None:
        return jnp.dot(a, b, precision=prec, preferred_element_type=F32)
    return lax.dot_general(a, b, (dims, ((), ())), precision=prec, preferred_element_type=F32)


_NT = ((1,), (1,))
_TN = ((0,), (0,))


def _load_rows(ref, rows):
    return jnp.concatenate([ref[pl.ds(s, rows, stride=ROW_TILES), :] for s in range(ROW_TILES)], axis=1)


def _store_rows(ref, val, rows):
    for s in range(ROW_TILES):
        ref[pl.ds(s, rows, stride=ROW_TILES), :] = val[:, s * LANES:(s + 1) * LANES]


def _ada_body(c_ref, w_ref, b_ref, o_ref):
    c = c_ref[...]
    a = c * jax.nn.sigmoid(c)
    o_ref[0] = _dot(a, w_ref[0], HI) + b_ref[0]


def _adaln(c_all, w_ada, b_ada):
    depth, d, n = w_ada.shape
    nb = c_all.shape[0]
    tn = 768
    return pl.pallas_call(
        _ada_body,
        grid=(depth, n // tn),
        in_specs=[pl.BlockSpec((nb, d), lambda l, j: (0, 0)),
                  pl.BlockSpec((1, d, tn), lambda l, j: (l, 0, j)),
                  pl.BlockSpec((1, 1, tn), lambda l, j: (l, 0, j))],
        out_specs=pl.BlockSpec((1, nb, tn), lambda l, j: (l, 0, j)),
        out_shape=jax.ShapeDtypeStruct((depth, nb, n), F32),
        compiler_params=_cparams(("arbitrary", "arbitrary")),
        name="adaln",
    )(c_all, w_ada, b_ada.reshape(depth, 1, n))


def _mod_spec(mods, tiles_per_seq):
    r = mods.shape[2]
    if r == 1:
        return pl.BlockSpec((6, 1, 1, D_MODEL), lambda i: (0, i // tiles_per_seq, 0, 0))
    return pl.BlockSpec((6, 1, r, D_MODEL), lambda i: (0, 0, 0, 0))


def _inproj_ret_body(x_ref, m_ref, g_ref, w_ref, cos_ref, sin_ref, o_ref, *, precise):
    h = _rms(x_ref[...], g_ref[...]) * (1.0 + m_ref[1, 0]) + m_ref[0, 0]
    hb = h if precise else h.astype(BF16)
    cos = cos_ref[...]
    sin = sin_ref[...]
    half = RET_DK // 2
    for c in range(2 * RET_HEADS):
        lo = c * RET_DK
        r = _mm(hb, w_ref[:, lo:lo + RET_DK], precise)
        x1 = r[:, :half]
        x2 = r[:, half:]
        sc = RET_DK ** -0.5 if c < RET_HEADS else 1.0
        o_ref[:, lo:lo + half] = (x1 * cos - x2 * sin) * sc
        o_ref[:, lo + half:lo + RET_DK] = (x1 * sin + x2 * cos) * sc
    base = 2 * RET_HEADS * RET_DK
    for c in range(2 * RET_HEADS):
        lo = base + c * RET_DV
        o_ref[:, lo:lo + RET_DV] = _mm(hb, w_ref[:, lo:lo + RET_DV], precise)


def _inproj_ret(x, mods, gain, w, cos, sin, tiles_per_seq):
    t = x.shape[0]
    n = w.shape[1]
    tm = min(TM, t)
    pos_tiles = cos.shape[0] // tm
    return pl.pallas_call(
        functools.partial(_inproj_ret_body, precise=w.dtype == F32),
        grid=(t // tm,),
        in_specs=[pl.BlockSpec((tm, D_MODEL), lambda i: (i, 0)),
                  _mod_spec(mods, tiles_per_seq),
                  pl.BlockSpec((1, D_MODEL), lambda i: (0, 0)),
                  pl.BlockSpec((D_MODEL, n), lambda i: (0, 0), pipeline_mode=pl.Buffered(1)),
                  pl.BlockSpec((tm, RET_DK // 2), lambda i: (i % pos_tiles, 0)),
                  pl.BlockSpec((tm, RET_DK // 2), lambda i: (i % pos_tiles, 0))],
        out_specs=pl.BlockSpec((tm, n), lambda i: (i, 0)),
        out_shape=jax.ShapeDtypeStruct((t, n), F32),
        compiler_params=_cparams(("arbitrary",)),
        name="inproj_ret",
    )(x, mods, gain, w, cos, sin)


def _split3(c):
    hi = c.astype(BF16).astype(F32)
    r = c - hi
    mid = r.astype(BF16).astype(F32)
    return hi, mid, r - mid


def _emit_attention_operands(o_ref, cs, qa_ref, ka_ref, va_ref):
    tm = cs.shape[0]
    fw = FOX_HEADS * FOX_HD
    lane = lax.broadcasted_iota(I32, (tm, LANES), 1)
    low = lane < FOX_HD
    ones_q = jnp.where((lane >= FOX_HD + 3) & (lane < FOX_HD + 6), 1.0, 0.0)
    ones_k = jnp.where((lane >= FOX_HD) & (lane < FOX_HD + 3), 1.0, 0.0)
    ones_v = jnp.where(lane == FOX_HD, 1.0, 0.0)
    for p in range(FOX_HEADS // 2):
        pairs = [o_ref[:, s * fw + p * LANES:s * fw + (p + 1) * LANES] for s in range(3)]
        pairs[0] = pairs[0] * (FOX_HD ** -0.5)
        for a in range(2):
            hh = 2 * p + a
            qh, kh, vh = pairs if a == 0 else [pltpu.roll(x, FOX_HD, 1) for x in pairs]
            c = jnp.sum(jnp.where(lane == hh, cs, 0.0), axis=1, keepdims=True)
            hi, mid, lo = _split3(c)
            fq = jnp.where(lane == FOX_HD, hi, jnp.where(lane == FOX_HD + 1, mid,
                                                         jnp.where(lane == FOX_HD + 2, lo, ones_q)))
            fk = jnp.where(lane == FOX_HD + 3, -hi, jnp.where(lane == FOX_HD + 4, -mid,
                                                              jnp.where(lane == FOX_HD + 5, -lo, ones_k)))
            cols = slice(hh * LANES, (hh + 1) * LANES)
            qa_ref[:, cols] = jnp.where(low, qh, fq).astype(BF16)
            ka_ref[:, cols] = jnp.where(low, kh, fk).astype(BF16)
            va_ref[:, cols] = jnp.where(low, vh, ones_v).astype(BF16)


def _inproj_fox_body(x_ref, m_ref, g_ref, w_ref, wfl_ref, bf_ref, tri_ref, *rest, tiles_per_seq, prompt):
    if prompt:
        wkt_ref, wvt_ref, gate_ref, lf_ref, qa_ref, ka_ref, va_ref, kt_ref, vt_ref, stage, carry = rest
    else:
        stage, lf_ref, cum_ref = rest
    i = pl.program_id(0)
    tm = x_ref.shape[0]
    fw = FOX_HEADS * FOX_HD
    h = _rms(x_ref[...], g_ref[...]) * (1.0 + m_ref[1, 0]) + m_ref[0, 0]
    hb = h.astype(BF16)
    cw = 512
    for c in range(4 * fw // cw):
        r = _dot(hb, w_ref[:, c * cw:(c + 1) * cw])
        if prompt and c * cw >= 3 * fw:
            gate_ref[:, c * cw - 3 * fw:(c + 1) * cw - 3 * fw] = r
        else:
            stage[:, c * cw:(c + 1) * cw] = r
    fl = _dot(h, wfl_ref[...], HI) + bf_ref[...]
    lf = jnp.minimum(fl, 0.0) - jnp.log1p(jnp.exp(-jnp.abs(fl)))
    lf_ref[...] = lf
    cs = _dot(tri_ref[...], lf, HI)
    if not prompt:
        cum_ref[...] = cs
        return

    @pl.when(i % tiles_per_seq == 0)
    def _():
        carry[...] = jnp.zeros_like(carry)
    cs = cs + carry[...]
    carry[...] = cs[tm - 1:tm, :]
    _emit_attention_operands(stage, cs, qa_ref, ka_ref, va_ref)
    kt_ref[0] = _dot_nt(wkt_ref[...], hb)
    vt_ref[0] = _dot_nt(wvt_ref[...], hb)


def _inproj_fox(x, mods, gain, w_bf, w_fl, b_fl, tri, tiles_per_seq, wkt=None, wvt=None):
    t = x.shape[0]
    n = w_bf.shape[1]
    tm = min(TM, t)
    prompt = wkt is not None
    body = functools.partial(_inproj_fox_body, tiles_per_seq=tiles_per_seq, prompt=prompt)
    fw = FOX_HEADS * FOX_HD
    wide = FOX_HEADS * LANES
    row = lambda width: pl.BlockSpec((tm, width), lambda i: (i, 0))
    in_specs = [row(D_MODEL),
                _mod_spec(mods, tiles_per_seq),
                pl.BlockSpec((1, D_MODEL), lambda i: (0, 0)),
                pl.BlockSpec((D_MODEL, n), lambda i: (0, 0), pipeline_mode=pl.Buffered(1)),
                pl.BlockSpec((D_MODEL, LANES), lambda i: (0, 0)),
                pl.BlockSpec((1, LANES), lambda i: (0, 0)),
                pl.BlockSpec((tm, tm), lambda i: (0, 0))]
    args = [x, mods, gain, w_bf, w_fl, b_fl, tri]
    if prompt:
        n_seq = t // (tiles_per_seq * tm)
        tspec = pl.BlockSpec((1, fw, tm), lambda i: (i // tiles_per_seq, 0, i % tiles_per_seq))
        in_specs += [pl.BlockSpec((fw, D_MODEL), lambda i: (0, 0), pipeline_mode=pl.Buffered(1))] * 2
        args += [wkt, wvt]
        out_specs = [row(fw), row(LANES), row(wide), row(wide), row(wide), tspec, tspec]
        out_shape = ([jax.ShapeDtypeStruct((t, fw), F32), jax.ShapeDtypeStruct((t, LANES), F32)]
                     + [jax.ShapeDtypeStruct((t, wide), BF16)] * 3
                     + [jax.ShapeDtypeStruct((n_seq, fw, tiles_per_seq * tm), F32)] * 2)
        scratch = [pltpu.VMEM((tm, 3 * fw), F32), pltpu.VMEM((1, LANES), F32)]
    else:
        out_specs = [row(n), row(LANES), row(LANES)]
        out_shape = [jax.ShapeDtypeStruct((t, n), F32), jax.ShapeDtypeStruct((t, LANES), F32),
                     jax.ShapeDtypeStruct((t, LANES), F32)]
        scratch = []
    return pl.pallas_call(
        body,
        grid=(t // tm,),
        in_specs=in_specs,
        out_specs=out_specs,
        out_shape=out_shape,
        scratch_shapes=scratch,
        compiler_params=_cparams(("arbitrary",)),
        name="inproj_fox",
    )(*args)


def _retention_body(q_ref, k_ref, v_ref, g_ref, s0_ref, din_ref, dq_ref, dk_ref, dc_ref, gn_ref,
                    u_ref, so_ref, state, *, n_chunks, kv_rows, precise):
    c = pl.program_id(1)

    @pl.when(c == 0)
    def _():
        state[...] = s0_ref[0]

    for hh in range(RET_HEADS):
        qk = slice(hh * RET_DK, (hh + 1) * RET_DK)
        vg = slice(hh * RET_DV, (hh + 1) * RET_DV)
        q = q_ref[:, qk]
        k = k_ref[:, qk]
        v = v_ref[:, vg]
        rows = k.shape[0]
        if rows < kv_rows:
            k = jnp.concatenate([k, jnp.zeros((kv_rows - rows, RET_DK), F32)], axis=0)
            v = jnp.concatenate([v, jnp.zeros((kv_rows - rows, RET_DV), F32)], axis=0)
        kd = k * dk_ref[hh]
        s_old = state[hh]
        inner = _mm(q, k, precise, _NT) * din_ref[hh]
        o = _mm(inner, v, precise) + _mm(q, s_old, precise) * dq_ref[hh]
        s_new = s_old * dc_ref[hh] + _mm(kd, v, False, _TN)
        state[hh] = s_new
        g = g_ref[:, vg]
        u_ref[:, vg] = _rms(o, gn_ref[:, vg]) * (g * jax.nn.sigmoid(g))

        @pl.when(c == n_chunks - 1)
        def _(hh=hh, s_new=s_new):
            so_ref[0, hh] = s_new


def _retention(proj, s0, din, dq, dk, dc, gn_gain, n_seq, n_chunks, q_rows, precise):
    t = proj.shape[0]
    kv_rows = din.shape[2]
    h = RET_HEADS
    hk, hv = h * RET_DK, h * RET_DV
    body = functools.partial(_retention_body, n_chunks=n_chunks, kv_rows=kv_rows, precise=precise)
    row = lambda b, c: b * n_chunks + c
    whole = lambda shape: pl.BlockSpec(shape, lambda b, c: (0,) * len(shape))
    return pl.pallas_call(
        body,
        grid=(n_seq, n_chunks),
        in_specs=[pl.BlockSpec((q_rows, hk), lambda b, c: (row(b, c), 0)),
                  pl.BlockSpec((q_rows, hk), lambda b, c: (row(b, c), 1)),
                  pl.BlockSpec((q_rows, hv), lambda b, c: (row(b, c), 2 * hk // hv)),
                  pl.BlockSpec((q_rows, hv), lambda b, c: (row(b, c), 2 * hk // hv + 1)),
                  pl.BlockSpec((1, h, RET_DK, RET_DV), lambda b, c: (b, 0, 0, 0)),
                  whole((h, q_rows, kv_rows)), whole((h, q_rows, 1)), whole((h, kv_rows, 1)), whole((h, 1, 1)),
                  whole((1, hv))],
        out_specs=[pl.BlockSpec((q_rows, hv), lambda b, c: (row(b, c), 0)),
                   pl.BlockSpec((1, h, RET_DK, RET_DV), lambda b, c: (b, 0, 0, 0))],
        out_shape=[jax.ShapeDtypeStruct((t, hv), F32),
                   jax.ShapeDtypeStruct((n_seq, h, RET_DK, RET_DV), F32)],
        scratch_shapes=[pltpu.VMEM((h, RET_DK, RET_DV), F32)],
        compiler_params=_cparams(("arbitrary", "arbitrary")),
        name="retention",
    )(proj, proj, proj, proj, s0, din, dq, dk, dc, gn_gain)


def _ret_decay_tables(c, q_rows, kv_rows):
    lg = jnp.log(1.0 - 2.0 ** (-5.0 - jnp.arange(RET_HEADS, dtype=F32)))
    idx = jnp.arange(c, dtype=F32)
    rel = idx[:, None] - idx[None, :]
    din = jnp.where(rel[None] >= 0, jnp.exp(lg[:, None, None] * jnp.maximum(rel, 0.0)[None]), 0.0)
    dq = jnp.exp(lg[:, None] * (idx[None, :] + 1.0))
    dk = jnp.exp(lg[:, None] * (c - 1.0 - idx[None, :]))
    dc = jnp.exp(lg * c)
    din = jnp.pad(din, ((0, 0), (0, q_rows - c), (0, kv_rows - c)))
    dq = jnp.pad(dq, ((0, 0), (0, q_rows - c)))[..., None]
    dk = jnp.pad(dk, ((0, 0), (0, kv_rows - c)))[..., None]
    return din, dq, dk, dc[:, None, None]


def _outproj_body(u_ref, w_ref, x_ref, m_ref, gpost_ref, gpre_ref, xn_ref, h2_ref):
    tm = x_ref.shape[0]
    y = _mm(u_ref[...], w_ref[...], w_ref.dtype == F32)
    xn = x_ref[...] + m_ref[2, 0] * _rms(y, gpost_ref[...])
    xn_ref[...] = xn
    h2 = _rms(xn, gpre_ref[...]) * (1.0 + m_ref[4, 0]) + m_ref[3, 0]
    _store_rows(h2_ref, h2, tm)


def _outproj(u, w, x, mods, gpost, gpre, tiles_per_seq):
    t = x.shape[0]
    din = u.shape[1]
    tm = min(TM, t)
    return pl.pallas_call(
        _outproj_body,
        grid=(t // tm,),
        in_specs=[pl.BlockSpec((tm, din), lambda i: (i, 0)),
                  pl.BlockSpec((din, D_MODEL), lambda i: (0, 0), pipeline_mode=pl.Buffered(1)),
                  pl.BlockSpec((tm, D_MODEL), lambda i: (i, 0)),
                  _mod_spec(mods, tiles_per_seq),
                  pl.BlockSpec((1, D_MODEL), lambda i: (0, 0)),
                  pl.BlockSpec((1, D_MODEL), lambda i: (0, 0))],
        out_specs=[pl.BlockSpec((tm, D_MODEL), lambda i: (i, 0)),
                   pl.BlockSpec((tm * ROW_TILES, LANES), lambda i: (i, 0))],
        out_shape=[jax.ShapeDtypeStruct((t, D_MODEL), F32),
                   jax.ShapeDtypeStruct((t * ROW_TILES, LANES), F32)],
        compiler_params=_cparams(("arbitrary",)),
        name="outproj",
    )(u, w, x, mods, gpost, gpre)


def _router_body(h_ref, w_ref, b_ref, tri_ref, idx_ref, wgt_ref, rank_ref, cnt_ref, carry):
    i = pl.program_id(0)
    tm = idx_ref.shape[0]

    @pl.when(i == 0)
    def _():
        carry[...] = jnp.zeros_like(carry)

    h = _load_rows(h_ref, tm)
    logits = _dot(h, w_ref[...], HI) + b_ref[...]
    lane = lax.broadcasted_iota(I32, (tm, LANES), 1)
    lane_f = lane.astype(F32)
    work = logits
    vals, ids, hots = [], [], []
    for _ in range(TOP_K):
        mx = jnp.max(work, axis=1, keepdims=True)
        ik = jnp.min(jnp.where(work == mx, lane_f, float(LANES)), axis=1, keepdims=True)
        hot = lane_f == ik
        vals.append(mx)
        ids.append(ik.astype(I32))
        hots.append(hot)
        work = jnp.where(hot, -jnp.inf, work)
    ex = [jnp.exp(v - vals[0]) for v in vals]
    den = ex[0] + ex[1] + ex[2] + ex[3]
    chosen = hots[0] | hots[1] | hots[2] | hots[3]
    before = _dot(tri_ref[...], chosen.astype(BF16)) + carry[...]
    idx_o = jnp.zeros((tm, LANES), I32)
    wgt_o = jnp.zeros((tm, LANES), F32)
    rank_o = jnp.zeros((tm, LANES), I32)
    for kk in range(TOP_K):
        rk = jnp.sum(jnp.where(hots[kk], before, 0.0), axis=1, keepdims=True).astype(I32)
        idx_o = jnp.where(lane == kk, ids[kk], idx_o)
        wgt_o = jnp.where(lane == kk, ex[kk] / den, wgt_o)
        rank_o = jnp.where(lane == kk, rk, rank_o)
    idx_ref[...] = idx_o
    wgt_ref[...] = wgt_o
    rank_ref[...] = rank_o
    carry[...] = carry[...] + jnp.sum(chosen.astype(F32), axis=0, keepdims=True)
    cnt_ref[...] = carry[...]


def _router(h2, w_r, b_r, tri):
    t = h2.shape[0] // ROW_TILES
    tm = TM
    return pl.pallas_call(
        _router_body,
        grid=(t // tm,),
        in_specs=[pl.BlockSpec((tm * ROW_TILES, LANES), lambda i: (i, 0)),
                  pl.BlockSpec((D_MODEL, LANES), lambda i: (0, 0)),
                  pl.BlockSpec((1, LANES), lambda i: (0, 0)),
                  pl.BlockSpec((tm, tm), lambda i: (0, 0))],
        out_specs=[pl.BlockSpec((tm, LANES), lambda i: (i, 0)),
                   pl.BlockSpec((tm, LANES), lambda i: (i, 0)),
                   pl.BlockSpec((tm, LANES), lambda i: (i, 0)),
                   pl.BlockSpec((1, LANES), lambda i: (0, 0))],
        out_shape=[jax.ShapeDtypeStruct((t, LANES), I32),
                   jax.ShapeDtypeStruct((t, LANES), F32),
                   jax.ShapeDtypeStruct((t, LANES), I32),
                   jax.ShapeDtypeStruct((1, LANES), F32)],
        scratch_shapes=[pltpu.VMEM((1, LANES), F32)],
        compiler_params=_cparams(("arbitrary",)),
        name="router",
    )(h2, w_r, b_r, tri)


def _token_copy(src, src_row, dst, dst_row, sem):
    return pltpu.make_async_copy(src.at[pl.ds(pl.multiple_of(src_row * ROW_TILES, ROW_TILES), ROW_TILES), :],
                                 dst.at[pl.ds(pl.multiple_of(dst_row * ROW_TILES, ROW_TILES), ROW_TILES), :], sem)


def _rows_copy(src, dst, n_rows, sem):
    return pltpu.make_async_copy(src.at[pl.ds(0, n_rows * ROW_TILES), :], dst.at[pl.ds(0, n_rows * ROW_TILES), :], sem)


INVERT_UNROLL = 16
SCATTER_BLOCK = 64


def _invert_body(dest_ref, lo_ref, hi_ref, inv_ref, *, n_pairs):
    def fill_group(g, carry):
        def fill(r, c):
            inv_ref[r] = n_pairs + r
            return c
        return lax.fori_loop(lo_ref[g], hi_ref[g], fill, carry)

    lax.fori_loop(0, lo_ref.shape[0], fill_group, 0)

    def step(g, carry):
        base = g * INVERT_UNROLL
        rows = [dest_ref[base + u] for u in range(INVERT_UNROLL)]
        for u in range(INVERT_UNROLL):
            inv_ref[rows[u]] = base + u
        return carry

    lax.fori_loop(0, dest_ref.shape[0] // INVERT_UNROLL, step, 0)


def _invert(dest_flat, pad_lo, pad_hi, n_rows):
    smem = pl.BlockSpec(memory_space=pltpu.SMEM)
    return pl.pallas_call(
        functools.partial(_invert_body, n_pairs=dest_flat.shape[0]),
        in_specs=[smem, smem, smem],
        out_specs=smem,
        out_shape=jax.ShapeDtypeStruct((n_rows,), I32),
        name="invert",
    )(dest_flat, pad_lo, pad_hi)


def _experts_body(te_ref, nv_ref, na_ref, invp_ref, inv_ref, invn_ref, h_ref, wgu_ref, bgu_ref, wdn_ref, bdn_ref,
                  y_ref, xbuf, ybuf, wgu_bf, wdn_bf, gsem, ssem, *, n_tokens):
    i = pl.program_id(0)
    tm = TM_FFN
    n_act = na_ref[0]
    slot = i % 2
    other = 1 - slot

    def gather(tile_inv, s):
        for r in range(tm):
            tok = jnp.minimum(tile_inv[0, 0, r] >> 2, n_tokens - 1)
            _token_copy(h_ref, tok, xbuf.at[s], r, gsem.at[s]).start()

    def scatter(tile_inv, s, n):
        def send(r):
            _token_copy(ybuf.at[s], r, y_ref, tile_inv[0, 0, r], ssem.at[s]).start()

        for b in range(tm // SCATTER_BLOCK):
            @pl.when((b + 1) * SCATTER_BLOCK <= n)
            def _(b=b):
                for r in range(b * SCATTER_BLOCK, (b + 1) * SCATTER_BLOCK):
                    send(r)

        def tail(r, carry):
            send(r)
            return carry
        lax.fori_loop((n // SCATTER_BLOCK) * SCATTER_BLOCK, n, tail, 0)

    @pl.when(i == 0)
    def _():
        gather(inv_ref, 0)

    @pl.when(i < n_act)
    def _():
        prev = te_ref[jnp.maximum(i - 1, 0)]

        @pl.when((i == 0) | (te_ref[i] != prev))
        def _():
            wgu_bf[...] = wgu_ref[0, 0].astype(BF16)
            wdn_bf[...] = wdn_ref[0, 0].astype(BF16)

        @pl.when(i >= 2)
        def _():
            _rows_copy(ybuf.at[slot], y_ref, nv_ref[jnp.maximum(i - 2, 0)], ssem.at[slot]).wait()

        _rows_copy(h_ref, xbuf.at[slot], tm, gsem.at[slot]).wait()
        gather(invn_ref, other)
        scatter(invp_ref, other, jnp.where(i == 0, 0, nv_ref[jnp.maximum(i - 1, 0)]))

        x = _load_rows(xbuf.at[slot], tm).astype(BF16)
        gu = _dot(x, wgu_bf[...]) + bgu_ref[0, 0]
        gate = jnp.minimum(gu[:, :D_FF], SWIGLU_LIMIT)
        up = jnp.clip(gu[:, D_FF:], -SWIGLU_LIMIT, SWIGLU_LIMIT)
        act = (up + 1.0) * gate * jax.nn.sigmoid(SWIGLU_ALPHA * gate)
        y = _dot(act.astype(BF16), wdn_bf[...]) + bdn_ref[0, 0]
        _store_rows(ybuf.at[slot], y, tm)

        @pl.when(i == n_act - 1)
        def _():
            scatter(inv_ref, slot, nv_ref[i])
            _rows_copy(h_ref, xbuf.at[other], tm, gsem.at[other]).wait()
            _rows_copy(ybuf.at[slot], y_ref, nv_ref[i], ssem.at[slot]).wait()

            @pl.when(i >= 1)
            def _():
                _rows_copy(ybuf.at[other], y_ref, nv_ref[jnp.maximum(i - 1, 0)], ssem.at[other]).wait()


def _experts(layer, tile_expert, n_valid, n_active, inv3, h2, w_gu, b_gu, w_dn, b_dn):
    n_tiles = inv3.shape[0]
    tm = TM_FFN
    n_tokens = h2.shape[0] // ROW_TILES
    n_pairs = n_tokens * TOP_K
    b_gu4 = b_gu.reshape(b_gu.shape[0], N_EXPERTS, 1, 2 * D_FF)
    b_dn4 = b_dn.reshape(b_dn.shape[0], N_EXPERTS, 1, D_MODEL)
    inv_spec = lambda shift: pl.BlockSpec(
        (1, 1, tm), lambda i, te, nv, na: (jnp.clip(i + shift, 0, n_tiles - 1), 0, 0), memory_space=pltpu.SMEM)
    grid_spec = pltpu.PrefetchScalarGridSpec(
        num_scalar_prefetch=3,
        grid=(n_tiles,),
        in_specs=[inv_spec(-1), inv_spec(0), inv_spec(1),
                  pl.BlockSpec(memory_space=pl.ANY),
                  pl.BlockSpec((1, 1, D_MODEL, 2 * D_FF), lambda i, te, nv, na: (layer, te[i], 0, 0)),
                  pl.BlockSpec((1, 1, 1, 2 * D_FF), lambda i, te, nv, na: (layer, te[i], 0, 0)),
                  pl.BlockSpec((1, 1, D_FF, D_MODEL), lambda i, te, nv, na: (layer, te[i], 0, 0)),
                  pl.BlockSpec((1, 1, 1, D_MODEL), lambda i, te, nv, na: (layer, te[i], 0, 0))],
        out_specs=pl.BlockSpec(memory_space=pl.ANY),
        scratch_shapes=[pltpu.VMEM((2, tm * ROW_TILES, LANES), F32), pltpu.VMEM((2, tm * ROW_TILES, LANES), F32),
                        pltpu.VMEM((D_MODEL, 2 * D_FF), BF16), pltpu.VMEM((D_FF, D_MODEL), BF16),
                        pltpu.SemaphoreType.DMA((2,)), pltpu.SemaphoreType.DMA((2,))],
    )
    return pl.pallas_call(
        functools.partial(_experts_body, n_tokens=n_tokens),
        grid_spec=grid_spec,
        out_shape=jax.ShapeDtypeStruct((n_pairs * ROW_TILES, LANES), F32),
        compiler_params=_cparams(("arbitrary",)),
        name="experts",
    )(tile_expert, n_valid, n_active, inv3, inv3, inv3, h2, w_gu, b_gu4, w_dn, b_dn4)


def _combine_body(y_ref, wgt_ref, x_ref, m_ref, g_ref, xo_ref):
    tm = x_ref.shape[0]
    w = wgt_ref[...]
    f = jnp.zeros((tm, D_MODEL), F32)
    for kk in range(TOP_K):
        yk = jnp.concatenate([y_ref[pl.ds(kk * ROW_TILES + s, tm, stride=TOP_K * ROW_TILES), :]
                              for s in range(ROW_TILES)], axis=1)
        f = f + w[:, kk:kk + 1] * yk
    xo_ref[...] = x_ref[...] + m_ref[5, 0] * _rms(f, g_ref[...])


def _combine(y, first_tile, wgt, x, mods, gpost, tiles_per_seq):
    t = x.shape[0]
    tm = TM_COMB
    if mods.shape[2] == 1:
        mspec = pl.BlockSpec((6, 1, 1, D_MODEL), lambda i: (0, i // tiles_per_seq, 0, 0))
    else:
        mspec = pl.BlockSpec((6, 1, tm, D_MODEL), lambda i: (0, 0, i, 0))
    return pl.pallas_call(
        _combine_body,
        grid=(t // tm,),
        in_specs=[pl.BlockSpec((tm * TOP_K * ROW_TILES, LANES), lambda i: (first_tile + i, 0)),
                  pl.BlockSpec((tm, LANES), lambda i: (i, 0)),
                  pl.BlockSpec((tm, D_MODEL), lambda i: (i, 0)),
                  mspec,
                  pl.BlockSpec((1, D_MODEL), lambda i: (0, 0))],
        out_specs=pl.BlockSpec((tm, D_MODEL), lambda i: (i, 0)),
        out_shape=jax.ShapeDtypeStruct((t, D_MODEL), F32),
        compiler_params=_cparams(("arbitrary",)),
        name="combine",
    )(y, wgt, x, mods, gpost)


def _moe(layer, h2_all, t_prompt, x_p, x_s, mods_p, mods_s, gpost, w_r, b_r, tri_strict,
         w_gu, b_gu, w_dn, b_dn, tiles_per_seq_comb):
    t_all = h2_all.shape[0] // ROW_TILES
    idx_w, wgt_w, rank_w, cnt_w = _router(h2_all, w_r, b_r, tri_strict)
    idx = idx_w[:, :TOP_K]
    rank = rank_w[:, :TOP_K]
    cnt = cnt_w[0, :N_EXPERTS].astype(I32)
    padded = ((cnt + TM_FFN - 1) // TM_FFN) * TM_FFN
    ends = jnp.cumsum(padded)
    offs = ends - padded
    dest = jnp.sum(jnp.where(idx[..., None] == jnp.arange(N_EXPERTS, dtype=I32), offs, 0), axis=-1) + rank
    n_tiles = (t_all * TOP_K) // TM_FFN + N_EXPERTS
    tile_start = jnp.arange(n_tiles, dtype=I32) * TM_FFN
    tile_expert = jnp.minimum(jnp.sum((tile_start[:, None] >= ends[None, :]).astype(I32), axis=1), N_EXPERTS - 1)
    n_active = (ends[-1] // TM_FFN).astype(I32).reshape(1)

    n_rows = n_tiles * TM_FFN
    pad_lo = jnp.concatenate([offs + cnt, ends[-1:]]).astype(I32)
    pad_hi = jnp.concatenate([ends, jnp.full((1,), n_rows, I32)]).astype(I32)
    inv = _invert(dest.reshape(-1), pad_lo, pad_hi, n_rows)
    n_valid = jnp.clip(offs[tile_expert] + cnt[tile_expert] - tile_start, 0, TM_FFN).astype(I32)
    y = _experts(layer, tile_expert, n_valid, n_active, inv.reshape(n_tiles, 1, TM_FFN), h2_all,
                 w_gu, b_gu, w_dn, b_dn)
    xo_p = _combine(y, 0, wgt_w[:t_prompt], x_p, mods_p, gpost, tiles_per_seq_comb)
    xo_s = _combine(y, t_prompt // TM_COMB, wgt_w[t_prompt:], x_s, mods_s, gpost, 1)
    return xo_p, xo_s


def _fox_prompt_body(qa_ref, ka_ref, va_ref, g_ref, o_ref, m_s, acc_s):
    i = pl.program_id(2)
    tq = qa_ref.shape[0]
    m_s[...] = jnp.full_like(m_s, -jnp.inf)
    acc_s[...] = jnp.zeros_like(acc_s)

    def block(start, diagonal):
        if diagonal:
            causal = lax.broadcasted_iota(I32, (tq, tq), 1) <= lax.broadcasted_iota(I32, (tq, tq), 0)
        for a in range(2):
            cols = slice(a * LANES, (a + 1) * LANES)
            s = _dot_nt(qa_ref[:, cols], ka_ref[pl.ds(start, tq), cols])
            if diagonal:
                s = jnp.where(causal, s, -jnp.inf)
            m_old = m_s[a]
            m_new = jnp.maximum(m_old, jnp.max(s, axis=1, keepdims=True))
            p = jnp.exp((s - m_new).astype(BF16))
            acc_s[a] = jnp.exp(m_old - m_new) * acc_s[a] + _dot(p, va_ref[pl.ds(start, tq), cols])
            m_s[a] = m_new

    def body(j, carry):
        block(pl.multiple_of(j * tq, tq), False)
        return carry

    lax.fori_loop(0, i, body, 0)
    block(pl.multiple_of(i * tq, tq), True)
    lane = lax.broadcasted_iota(I32, (tq, LANES), 1)
    outs = [acc_s[a] / acc_s[a][:, FOX_HD:FOX_HD + 1] for a in range(2)]
    o = jnp.where(lane < FOX_HD, outs[0], pltpu.roll(outs[1], FOX_HD, 1))
    o_ref[...] = o * jax.nn.sigmoid(g_ref[...])


def _fox_prompt_attn(qa, ka, va, gate, n_seq, seq):
    t = gate.shape[0]
    nq = seq // TQ
    pairs = FOX_HEADS // 2
    return pl.pallas_call(
        _fox_prompt_body,
        grid=(n_seq, pairs, nq),
        in_specs=[pl.BlockSpec((TQ, 2 * LANES), lambda b, hp, i: (b * nq + i, hp)),
                  pl.BlockSpec((seq, 2 * LANES), lambda b, hp, i: (b, hp)),
                  pl.BlockSpec((seq, 2 * LANES), lambda b, hp, i: (b, hp)),
                  pl.BlockSpec((TQ, LANES), lambda b, hp, i: (b * nq + i, hp))],
        out_specs=pl.BlockSpec((TQ, LANES), lambda b, hp, i: (b * nq + i, hp)),
        out_shape=jax.ShapeDtypeStruct((t, FOX_HEADS * FOX_HD), F32),
        scratch_shapes=[pltpu.VMEM((2, TQ, 1), F32), pltpu.VMEM((2, TQ, LANES), F32)],
        compiler_params=_cparams(("arbitrary", "arbitrary", "arbitrary")),
        name="fox_prompt_attn",
    )(qa, ka, va, gate)


def _fox_sample_body(pt_ref, q_ref, kn_ref, vn_ref, g_ref, cncol_ref, cnrow_ref, *rest):
    npg = PAGES_PER_STEP
    k_refs = rest[:npg]
    v_refs = rest[npg:2 * npg]
    f_refs = rest[2 * npg:3 * npg]
    u_ref, e_ref, o_ref, qbd_s, m_s, l_s, acc_s, car_s = rest[3 * npg:]
    del pt_ref
    j = pl.program_id(1)
    nj = pl.num_programs(1)
    rows = FOX_HEADS * TOK_PAD
    wide = FOX_HEADS * FOX_HD
    own = ((lax.broadcasted_iota(I32, (rows, wide), 0) // TOK_PAD)
           == (lax.broadcasted_iota(I32, (rows, wide), 1) // FOX_HD))

    @pl.when(j == 0)
    def _():
        m_s[...] = jnp.full_like(m_s, -jnp.inf)
        l_s[...] = jnp.zeros_like(l_s)
        acc_s[...] = jnp.zeros_like(acc_s)
        car_s[...] = jnp.zeros_like(car_s)
        q = q_ref[...] * (FOX_HD ** -0.5)
        qbd_s[...] = jnp.where(own, jnp.concatenate([q] * FOX_HEADS, axis=0), 0.0).astype(BF16)

    qbd = qbd_s[...]
    cn = cncol_ref[0]

    def update(s, pv_of):
        m_old = m_s[...]
        m_new = jnp.maximum(m_old, jnp.max(s, axis=1, keepdims=True))
        alpha = jnp.exp(m_old - m_new)
        p = jnp.exp(s - m_new)
        l_s[...] = alpha * l_s[...] + jnp.sum(p, axis=1, keepdims=True)
        acc_s[...] = alpha * acc_s[...] + pv_of(p.astype(BF16))
        m_s[...] = m_new

    suffixes = []
    for r in range(npg):
        ft = f_refs[r][0]
        suffixes.append(_dot(ft, u_ref[...], HI) + car_s[...])
        car_s[...] = car_s[...] + jnp.sum(ft, axis=1, keepdims=True)
    bias = _dot(e_ref[...], jnp.concatenate(suffixes, axis=1), HI) + cn
    kt = jnp.concatenate([k_refs[r][0].astype(BF16) for r in range(npg)], axis=1)
    vt = jnp.concatenate([v_refs[r][0].astype(BF16) for r in range(npg)], axis=1)
    update(_dot(qbd, kt) + bias, lambda pb: _dot_nt(pb, vt))

    @pl.when(j == nj - 1)
    def _():
        zpad = jnp.zeros((PAGE_SIZE - TOK_PAD, wide), F32)
        kn = jnp.concatenate([kn_ref[...], zpad], axis=0).astype(BF16)
        vn = jnp.concatenate([vn_ref[...], zpad], axis=0).astype(BF16)
        trow = lax.broadcasted_iota(I32, (rows, PAGE_SIZE), 0) & (TOK_PAD - 1)
        tcol = lax.broadcasted_iota(I32, (rows, PAGE_SIZE), 1)
        s = jnp.where(tcol <= trow, _dot_nt(qbd, kn) + cn - cnrow_ref[0], -jnp.inf)
        update(s, lambda pb: _dot(pb, vn))
        o = jnp.where(own, acc_s[...] / l_s[...], 0.0)
        o8 = o[0:TOK_PAD, :]
        for h in range(1, FOX_HEADS):
            o8 = o8 + o[h * TOK_PAD:(h + 1) * TOK_PAD, :]
        o_ref[...] = o8 * jax.nn.sigmoid(g_ref[...])


def _fox_sample_attn(proj, cncol, cnrow, cache_k, cache_v, cache_ft, page_table, u_mat, e_mat):
    n_seq, n_pages = page_table.shape
    npg = PAGES_PER_STEP
    steps = n_pages // npg
    wide = FOX_HEADS * FOX_HD
    rows = FOX_HEADS * TOK_PAD

    def page(r):
        return lambda b, j, pt: pt[b * n_pages + (n_pages - 1 - (j * npg + r))]

    kv_specs = [pl.BlockSpec((1, wide, PAGE_SIZE), (lambda b, j, pt, r=r: (page(r)(b, j, pt), 0, 0)))
                for r in range(npg)]
    f_specs = [pl.BlockSpec((1, FOX_HEADS, PAGE_SIZE), (lambda b, j, pt, r=r: (page(r)(b, j, pt), 0, 0)))
               for r in range(npg)]
    grid_spec = pltpu.PrefetchScalarGridSpec(
        num_scalar_prefetch=1,
        grid=(n_seq, steps),
        in_specs=[pl.BlockSpec((TOK_PAD, wide), lambda b, j, pt: (b, 0)),
                  pl.BlockSpec((TOK_PAD, wide), lambda b, j, pt: (b, 1)),
                  pl.BlockSpec((TOK_PAD, wide), lambda b, j, pt: (b, 2)),
                  pl.BlockSpec((TOK_PAD, wide), lambda b, j, pt: (b, 3)),
                  pl.BlockSpec((1, rows, 1), lambda b, j, pt: (b, 0, 0)),
                  pl.BlockSpec((1, rows, PAGE_SIZE), lambda b, j, pt: (b, 0, 0))]
                 + kv_specs + kv_specs + f_specs
                 + [pl.BlockSpec((PAGE_SIZE, PAGE_SIZE), lambda b, j, pt: (0, 0)),
                    pl.BlockSpec((rows, FOX_HEADS), lambda b, j, pt: (0, 0))],
        out_specs=pl.BlockSpec((TOK_PAD, wide), lambda b, j, pt: (b, 0)),
        scratch_shapes=[pltpu.VMEM((rows, wide), BF16), pltpu.VMEM((rows, 1), F32), pltpu.VMEM((rows, 1), F32),
                        pltpu.VMEM((rows, wide), F32), pltpu.VMEM((FOX_HEADS, 1), F32)],
    )
    return pl.pallas_call(
        _fox_sample_body,
        grid_spec=grid_spec,
        out_shape=jax.ShapeDtypeStruct((n_seq * TOK_PAD, wide), F32),
        compiler_params=_cparams(("arbitrary", "arbitrary")),
        name="fox_sample_attn",
    )(page_table.reshape(-1), proj, proj, proj, proj, cncol, cnrow,
      *([cache_k] * npg), *([cache_v] * npg), *([cache_ft] * npg), u_mat, e_mat)


def _rope_tables(pos):
    half = RET_DK // 2
    inv = ROPE_BASE ** (-jnp.arange(half, dtype=F32) / half)
    ang = pos.astype(F32)[:, None] * inv[None, :]
    return jnp.cos(ang), jnp.sin(ang)


def _mods_prompt(m):
    b = m.shape[0]
    return m.reshape(b, 6, D_MODEL).transpose(1, 0, 2)[:, :, None, :]


def _mods_sample(m):
    b = m.shape[0]
    mm = jnp.repeat(m.reshape(b, 6, D_MODEL), TOK_PAD, axis=0)
    return mm.transpose(1, 0, 2)[:, None, :, :]


def kernel(x_prompt, x_sample, c_prompt, c_sample, state_ret, cache_fox_k, cache_fox_v, cache_fox_logf, page_table, w_ada, b_ada, norm_pre_mix, norm_post_mix, norm_pre_ffn, norm_post_ffn, ret_w_in, ret_gn_gain, ret_w_out, fox_w_in, fox_b_f, fox_w_out, moe_w_router, moe_b_router, moe_w_gate_up, moe_b_gate_up, moe_w_down, moe_b_down):
    n_seq, seq, d = x_prompt.shape
    n_dec, dec_seq, _ = x_sample.shape
    t_p = n_seq * seq
    t_s = n_dec * TOK_PAD
    tps = seq // TM

    xp = x_prompt.reshape(t_p, d)
    xs = jnp.pad(x_sample, ((0, 0), (0, TOK_PAD - dec_seq), (0, 0))).reshape(t_s, d)

    m_all = _adaln(jnp.concatenate([c_prompt, c_sample], axis=0), w_ada, b_ada)

    ar = jnp.arange(TM)
    tri_incl = (ar[None, :] <= ar[:, None]).astype(F32)
    tri_strict = (ar[None, :] < ar[:, None]).astype(BF16)
    tri_group = ((ar[None, :] <= ar[:, None]) & (ar[None, :] // TOK_PAD == ar[:, None] // TOK_PAD)).astype(F32)

    mods_p = _mods_prompt(m_all[0, :n_seq])
    mods_s = _mods_sample(m_all[0, n_seq:])
    w_in = ret_w_in[0].astype(BF16)
    cos_p, sin_p = _rope_tables(jnp.arange(seq))
    pos_s = jnp.tile(PAST_LEN + jnp.arange(TOK_PAD), n_dec)
    cos_s, sin_s = _rope_tables(pos_s)
    g_pre = norm_pre_mix[0:1]
    proj_p = _inproj_ret(xp, mods_p, g_pre, w_in, cos_p, sin_p, tps)
    proj_s = _inproj_ret(xs, mods_s, g_pre, ret_w_in[0], cos_s, sin_s, 1)

    n_chunks = seq // RET_CHUNK
    din, dq, dk, dc = _ret_decay_tables(RET_CHUNK, RET_CHUNK, RET_CHUNK)
    s0p = jnp.zeros((n_seq, RET_HEADS, RET_DK, RET_DV), F32)
    u_p, st_p = _retention(proj_p, s0p, din, dq, dk, dc, ret_gn_gain, n_seq, n_chunks, RET_CHUNK, False)
    din, dq, dk, dc = _ret_decay_tables(dec_seq, TOK_PAD, RET_CHUNK)
    u_s, st_s = _retention(proj_s, state_ret[0], din, dq, dk, dc, ret_gn_gain, n_dec, 1, TOK_PAD, True)

    w_out = ret_w_out[0].astype(BF16)
    xp, h2p = _outproj(u_p, w_out, xp, mods_p, norm_post_mix[0:1], norm_pre_ffn[0:1], tps)
    xs, h2s = _outproj(u_s, ret_w_out[0], xs, mods_s, norm_post_mix[0:1], norm_pre_ffn[0:1], 1)

    def router_params(i):
        w_r = jnp.pad(moe_w_router[i], ((0, 0), (0, LANES - N_EXPERTS)))
        b_r = jnp.pad(moe_b_router[i], (0, LANES - N_EXPERTS), constant_values=-1e30)[None, :]
        return w_r, b_r

    w_r, b_r = router_params(0)
    xp, xs = _moe(0, jnp.concatenate([h2p, h2s], axis=0), t_p, xp, xs, mods_p, mods_s, norm_post_ffn[0:1],
                  w_r, b_r, tri_strict, moe_w_gate_up, moe_b_gate_up, moe_w_down, moe_b_down, seq // TM_COMB)

    mods_p = _mods_prompt(m_all[1, :n_seq])
    mods_s = _mods_sample(m_all[1, n_seq:])
    fw = FOX_HEADS * FOX_HD
    w_in = fox_w_in[0, :, :4 * fw].astype(BF16)
    w_fl = jnp.pad(fox_w_in[0, :, 4 * fw:], ((0, 0), (0, LANES - FOX_HEADS)))
    b_fl = jnp.pad(fox_b_f[0], (0, LANES - FOX_HEADS))[None, :]
    g_pre = norm_pre_mix[1:2]
    wkt = fox_w_in[0, :, fw:2 * fw].T.astype(BF16)
    wvt = fox_w_in[0, :, 2 * fw:3 * fw].T.astype(BF16)
    gate_p, lf_p, qa, ka, va, kt_p, vt_p = _inproj_fox(xp, mods_p, g_pre, w_in, w_fl, b_fl, tri_incl, tps, wkt, wvt)
    fproj_s, lf_s, cum_s = _inproj_fox(xs, mods_s, g_pre, w_in, w_fl, b_fl, tri_group, 1)
    o_p = _fox_prompt_attn(qa, ka, va, gate_p, n_seq, seq)

    cn = cum_s[:, :FOX_HEADS].reshape(n_dec, TOK_PAD, FOX_HEADS).transpose(0, 2, 1)
    cncol = cn.reshape(n_dec, FOX_HEADS * TOK_PAD, 1)
    cnrow = jnp.repeat(cn, TOK_PAD, axis=1)
    cnrow = jnp.pad(cnrow, ((0, 0), (0, 0), (0, PAGE_SIZE - TOK_PAD)))
    pr = jnp.arange(PAGE_SIZE)
    u_mat = (pr[:, None] > pr[None, :]).astype(F32)
    e_mat = (jnp.arange(FOX_HEADS * TOK_PAD)[:, None] // TOK_PAD == jnp.arange(FOX_HEADS)[None, :]).astype(F32)
    n_pool = cache_fox_k.shape[1]
    cache_kt = cache_fox_k[0].transpose(0, 2, 3, 1).reshape(n_pool, fw, PAGE_SIZE)
    cache_vt = cache_fox_v[0].transpose(0, 2, 3, 1).reshape(n_pool, fw, PAGE_SIZE)
    cache_ft = cache_fox_logf[0].transpose(0, 2, 1)
    o_s = _fox_sample_attn(fproj_s, cncol, cnrow, cache_kt, cache_vt, cache_ft, page_table, u_mat, e_mat)

    w_out = fox_w_out[0].astype(BF16)
    xp, h2p = _outproj(o_p, w_out, xp, mods_p, norm_post_mix[1:2], norm_pre_ffn[1:2], tps)
    xs, h2s = _outproj(o_s, w_out, xs, mods_s, norm_post_mix[1:2], norm_pre_ffn[1:2], 1)
    w_r, b_r = router_params(1)
    xp, xs = _moe(1, jnp.concatenate([h2p, h2s], axis=0), t_p, xp, xs, mods_p, mods_s, norm_post_ffn[1:2],
                  w_r, b_r, tri_strict, moe_w_gate_up, moe_b_gate_up, moe_w_down, moe_b_down, seq // TM_COMB)

    y_prompt = xp.reshape(n_seq, seq, d)
    y_sample = xs.reshape(n_dec, TOK_PAD, d)[:, :dec_seq]
    k_prompt = kt_p.reshape(n_seq, FOX_HEADS, FOX_HD, seq).transpose(0, 3, 1, 2)[None]
    v_prompt = vt_p.reshape(n_seq, FOX_HEADS, FOX_HD, seq).transpose(0, 3, 1, 2)[None]
    logf_prompt = lf_p[:, :FOX_HEADS].reshape(1, n_seq, seq, FOX_HEADS)
    fs = fproj_s.reshape(n_dec, TOK_PAD, 4 * fw)[:, :dec_seq]
    kv_shape_s = (1, n_dec, dec_seq, FOX_HEADS, FOX_HD)
    k_sample = fs[..., fw:2 * fw].reshape(kv_shape_s)
    v_sample = fs[..., 2 * fw:3 * fw].reshape(kv_shape_s)
    logf_sample = lf_s.reshape(n_dec, TOK_PAD, LANES)[:, :dec_seq, :FOX_HEADS][None]
    return (y_prompt, y_sample, st_p[None], st_s[None], k_prompt, v_prompt, logf_prompt,
            k_sample, v_sample, logf_sample)
```

```python
import functools

import jax
import jax.numpy as jnp
from jax import lax
from jax.experimental import pallas as pl
from jax.experimental.pallas import tpu as pltpu

F32 = jnp.float32
BF16 = jnp.bfloat16
I32 = jnp.int32
HI = lax.Precision.HIGHEST

D_MODEL = 1024
PAST_LEN = 8192
PAGE_SIZE = 128
RET_HEADS = 4
RET_DK = D_MODEL // RET_HEADS
RET_DV = 2 * RET_DK
RET_CHUNK = 128
ROPE_BASE = 10000.0
FOX_HEADS = 16
FOX_HD = D_MODEL // FOX_HEADS
N_EXPERTS = 32
TOP_K = 4
D_FF = D_MODEL
SWIGLU_LIMIT = 7.0
SWIGLU_ALPHA = 1.702
NORM_EPS = 1e-6

LANES = 128
SUBLANES = 8
ROW_TILES = D_MODEL // LANES
TOK_PAD = SUBLANES
TM = 256
TM_FFN = 512
TM_COMB = 128
TQ = 512
PAGES_PER_STEP = 8
VMEM_LIMIT = 56 * 1024 * 1024


def _cparams(sem, vmem=VMEM_LIMIT):
    return pltpu.CompilerParams(dimension_semantics=sem, vmem_limit_bytes=vmem)


def _rms(x, gain):
    return x * lax.rsqrt(jnp.mean(x * x, axis=-1, keepdims=True) + NORM_EPS) * gain


def _dot(a, b, precision=None):
    return jnp.dot(a, b, precision=precision, preferred_element_type=F32)


def _dot_nt(a, b):
    return lax.dot_general(a, b, (((1,), (1,)), ((), ())), preferred_element_type=F32)


def _mm(a, b, precise, dims=None):
    if precise:
        a, b, prec = a.astype(F32), b.astype(F32), HI
    else:
        a, b, prec = a.astype(BF16), b.astype(BF16), None
    if dims is None:
        return jnp.dot(a, b, precision=prec, preferred_element_type=F32)
    return lax.dot_general(a, b, (dims, ((), ())), precision=prec, preferred_element_type=F32)


_NT = ((1,), (1,))
_TN = ((0,), (0,))


def _load_rows(ref, rows):
    return jnp.concatenate([ref[pl.ds(s, rows, stride=ROW_TILES), :] for s in range(ROW_TILES)], axis=1)


def _store_rows(ref, val, rows):
    for s in range(ROW_TILES):
        ref[pl.ds(s, rows, stride=ROW_TILES), :] = val[:, s * LANES:(s + 1) * LANES]


def _ada_body(c_ref, w_ref, b_ref, o_ref):
    c = c_ref[...]
    a = c * jax.nn.sigmoid(c)
    o_ref[0] = _dot(a, w_ref[0], HI) + b_ref[0]


def _adaln(c_all, w_ada, b_ada):
    depth, d, n = w_ada.shape
    nb = c_all.shape[0]
    tn = 768
    return pl.pallas_call(
        _ada_body,
        grid=(depth, n // tn),
        in_specs=[pl.BlockSpec((nb, d), lambda l, j: (0, 0)),
                  pl.BlockSpec((1, d, tn), lambda l, j: (l, 0, j)),
                  pl.BlockSpec((1, 1, tn), lambda l, j: (l, 0, j))],
        out_specs=pl.BlockSpec((1, nb, tn), lambda l, j: (l, 0, j)),
        out_shape=jax.ShapeDtypeStruct((depth, nb, n), F32),
        compiler_params=_cparams(("arbitrary", "arbitrary")),
        name="adaln",
    )(c_all, w_ada, b_ada.reshape(depth, 1, n))


def _mod_spec(mods, tiles_per_seq):
    r = mods.shape[2]
    if r == 1:
        return pl.BlockSpec((6, 1, 1, D_MODEL), lambda i: (0, i // tiles_per_seq, 0, 0))
    return pl.BlockSpec((6, 1, r, D_MODEL), lambda i: (0, 0, 0, 0))


def _inproj_ret_body(x_ref, m_ref, g_ref, w_ref, cos_ref, sin_ref, o_ref, *, precise):
    h = _rms(x_ref[...], g_ref[...]) * (1.0 + m_ref[1, 0]) + m_ref[0, 0]
    hb = h if precise else h.astype(BF16)
    cos = cos_ref[...]
    sin = sin_ref[...]
    half = RET_DK // 2
    for c in range(2 * RET_HEADS):
        lo = c * RET_DK
        r = _mm(hb, w_ref[:, lo:lo + RET_DK], precise)
        x1 = r[:, :half]
        x2 = r[:, half:]
        sc = RET_DK ** -0.5 if c < RET_HEADS else 1.0
        o_ref[:, lo:lo + half] = (x1 * cos - x2 * sin) * sc
        o_ref[:, lo + half:lo + RET_DK] = (x1 * sin + x2 * cos) * sc
    base = 2 * RET_HEADS * RET_DK
    for c in range(2 * RET_HEADS):
        lo = base + c * RET_DV
        o_ref[:, lo:lo + RET_DV] = _mm(hb, w_ref[:, lo:lo + RET_DV], precise)


def _inproj_ret(x, mods, gain, w, cos, sin, tiles_per_seq):
    t = x.shape[0]
    n = w.shape[1]
    tm = min(TM, t)
    pos_tiles = cos.shape[0] // tm
    return pl.pallas_call(
        functools.partial(_inproj_ret_body, precise=w.dtype == F32),
        grid=(t // tm,),
        in_specs=[pl.BlockSpec((tm, D_MODEL), lambda i: (i, 0)),
                  _mod_spec(mods, tiles_per_seq),
                  pl.BlockSpec((1, D_MODEL), lambda i: (0, 0)),
                  pl.BlockSpec((D_MODEL, n), lambda i: (0, 0), pipeline_mode=pl.Buffered(1)),
                  pl.BlockSpec((tm, RET_DK // 2), lambda i: (i % pos_tiles, 0)),
                  pl.BlockSpec((tm, RET_DK // 2), lambda i: (i % pos_tiles, 0))],
        out_specs=pl.BlockSpec((tm, n), lambda i: (i, 0)),
        out_shape=jax.ShapeDtypeStruct((t, n), F32),
        compiler_params=_cparams(("arbitrary",)),
        name="inproj_ret",
    )(x, mods, gain, w, cos, sin)


def _split3(c):
    hi = c.astype(BF16).astype(F32)
    r = c - hi
    mid = r.astype(BF16).astype(F32)
    return hi, mid, r - mid


def _emit_attention_operands(o_ref, cs, qa_ref, ka_ref):
    tm = cs.shape[0]
    fw = FOX_HEADS * FOX_HD
    lane = lax.broadcasted_iota(I32, (tm, LANES), 1)
    low = lane < FOX_HD
    ones_q = jnp.where((lane >= FOX_HD + 3) & (lane < FOX_HD + 6), 1.0, 0.0)
    ones_k = jnp.where((lane >= FOX_HD) & (lane < FOX_HD + 3), 1.0, 0.0)
    for p in range(FOX_HEADS // 2):
        pairs = [o_ref[:, s * fw + p * LANES:s * fw + (p + 1) * LANES] for s in range(2)]
        pairs[0] = pairs[0] * (FOX_HD ** -0.5)
        for a in range(2):
            hh = 2 * p + a
            qh, kh = pairs if a == 0 else [pltpu.roll(x, FOX_HD, 1) for x in pairs]
            c = jnp.sum(jnp.where(lane == hh, cs, 0.0), axis=1, keepdims=True)
            hi, mid, lo = _split3(c)
            fq = jnp.where(lane == FOX_HD, hi, jnp.where(lane == FOX_HD + 1, mid,
                                                         jnp.where(lane == FOX_HD + 2, lo, ones_q)))
            fk = jnp.where(lane == FOX_HD + 3, -hi, jnp.where(lane == FOX_HD + 4, -mid,
                                                              jnp.where(lane == FOX_HD + 5, -lo, ones_k)))
            cols = slice(hh * LANES, (hh + 1) * LANES)
            qa_ref[:, cols] = jnp.where(low, qh, fq).astype(BF16)
            ka_ref[:, cols] = jnp.where(low, kh, fk).astype(BF16)


def _inproj_fox_body(x_ref, m_ref, g_ref, w_ref, wfl_ref, bf_ref, tri_ref, *rest, tiles_per_seq, prompt):
    if prompt:
        wkt_ref, wvt_ref, gate_ref, lf_ref, qa_ref, ka_ref, vat_ref, kt_ref, vt_ref, stage, carry = rest
    else:
        stage, lf_ref, cum_ref = rest
    i = pl.program_id(0)
    tm = x_ref.shape[0]
    fw = FOX_HEADS * FOX_HD
    h = _rms(x_ref[...], g_ref[...]) * (1.0 + m_ref[1, 0]) + m_ref[0, 0]
    hb = h.astype(BF16)
    cw = 512
    for c in range(4 * fw // cw):
        if prompt and 2 * fw <= c * cw < 3 * fw:
            continue
        r = _dot(hb, w_ref[:, c * cw:(c + 1) * cw])
        if prompt and c * cw >= 3 * fw:
            gate_ref[:, c * cw - 3 * fw:(c + 1) * cw - 3 * fw] = r
        else:
            stage[:, c * cw:(c + 1) * cw] = r
    fl = _dot(h, wfl_ref[...], HI) + bf_ref[...]
    lf = jnp.minimum(fl, 0.0) - jnp.log1p(jnp.exp(-jnp.abs(fl)))
    lf_ref[...] = lf
    cs = _dot(tri_ref[...], lf, HI)
    if not prompt:
        cum_ref[...] = cs
        return

    @pl.when(i % tiles_per_seq == 0)
    def _():
        carry[...] = jnp.zeros_like(carry)
    cs = cs + carry[...]
    carry[...] = cs[tm - 1:tm, :]
    _emit_attention_operands(stage, cs, qa_ref, ka_ref)
    kt_ref[0] = _dot_nt(wkt_ref[...], hb)
    vt = _dot_nt(wvt_ref[...], hb)
    vt_ref[0] = vt
    ones_rows = jnp.where(lax.broadcasted_iota(I32, (LANES - FOX_HD, tm), 0) == 0, 1.0, 0.0)
    for hh in range(FOX_HEADS):
        vat_ref[0, hh * LANES:(hh + 1) * LANES, :] = jnp.concatenate(
            [vt[hh * FOX_HD:(hh + 1) * FOX_HD, :], ones_rows], axis=0).astype(BF16)


def _inproj_fox(x, mods, gain, w_bf, w_fl, b_fl, tri, tiles_per_seq, wkt=None, wvt=None):
    t = x.shape[0]
    n = w_bf.shape[1]
    tm = min(TM, t)
    prompt = wkt is not None
    body = functools.partial(_inproj_fox_body, tiles_per_seq=tiles_per_seq, prompt=prompt)
    fw = FOX_HEADS * FOX_HD
    wide = FOX_HEADS * LANES
    row = lambda width: pl.BlockSpec((tm, width), lambda i: (i, 0))
    in_specs = [row(D_MODEL),
                _mod_spec(mods, tiles_per_seq),
                pl.BlockSpec((1, D_MODEL), lambda i: (0, 0)),
                pl.BlockSpec((D_MODEL, n), lambda i: (0, 0), pipeline_mode=pl.Buffered(1)),
                pl.BlockSpec((D_MODEL, LANES), lambda i: (0, 0)),
                pl.BlockSpec((1, LANES), lambda i: (0, 0)),
                pl.BlockSpec((tm, tm), lambda i: (0, 0))]
    args = [x, mods, gain, w_bf, w_fl, b_fl, tri]
    if prompt:
        n_seq = t // (tiles_per_seq * tm)
        tspec = pl.BlockSpec((1, fw, tm), lambda i: (i // tiles_per_seq, 0, i % tiles_per_seq))
        vspec = pl.BlockSpec((1, wide, tm), lambda i: (i // tiles_per_seq, 0, i % tiles_per_seq))
        in_specs += [pl.BlockSpec((fw, D_MODEL), lambda i: (0, 0), pipeline_mode=pl.Buffered(1))] * 2
        args += [wkt, wvt]
        out_specs = [row(fw), row(LANES), row(wide), row(wide), vspec, tspec, tspec]
        out_shape = ([jax.ShapeDtypeStruct((t, fw), F32), jax.ShapeDtypeStruct((t, LANES), F32)]
                     + [jax.ShapeDtypeStruct((t, wide), BF16)] * 2
                     + [jax.ShapeDtypeStruct((n_seq, wide, tiles_per_seq * tm), BF16)]
                     + [jax.ShapeDtypeStruct((n_seq, fw, tiles_per_seq * tm), F32)] * 2)
        scratch = [pltpu.VMEM((tm, 2 * fw), F32), pltpu.VMEM((1, LANES), F32)]
    else:
        out_specs = [row(n), row(LANES), row(LANES)]
        out_shape = [jax.ShapeDtypeStruct((t, n), F32), jax.ShapeDtypeStruct((t, LANES), F32),
                     jax.ShapeDtypeStruct((t, LANES), F32)]
        scratch = []
    return pl.pallas_call(
        body,
        grid=(t // tm,),
        in_specs=in_specs,
        out_specs=out_specs,
        out_shape=out_shape,
        scratch_shapes=scratch,
        compiler_params=_cparams(("arbitrary",)),
        name="inproj_fox",
    )(*args)


def _retention_body(q_ref, k_ref, v_ref, g_ref, s0_ref, din_ref, dq_ref, dk_ref, dc_ref, gn_ref,
                    u_ref, so_ref, state, *, n_chunks, kv_rows, precise):
    c = pl.program_id(1)

    @pl.when(c == 0)
    def _():
        state[...] = s0_ref[0]

    for hh in range(RET_HEADS):
        qk = slice(hh * RET_DK, (hh + 1) * RET_DK)
        vg = slice(hh * RET_DV, (hh + 1) * RET_DV)
        q = q_ref[:, qk]
        k = k_ref[:, qk]
        v = v_ref[:, vg]
        rows = k.shape[0]
        if rows < kv_rows:
            k = jnp.concatenate([k, jnp.zeros((kv_rows - rows, RET_DK), F32)], axis=0)
            v = jnp.concatenate([v, jnp.zeros((kv_rows - rows, RET_DV), F32)], axis=0)
        kd = k * dk_ref[hh]
        s_old = state[hh]
        inner = _mm(q, k, precise, _NT) * din_ref[hh]
        o = _mm(inner, v, precise) + _mm(q, s_old, precise) * dq_ref[hh]
        s_new = s_old * dc_ref[hh] + _mm(kd, v, False, _TN)
        state[hh] = s_new
        g = g_ref[:, vg]
        u_ref[:, vg] = _rms(o, gn_ref[:, vg]) * (g * jax.nn.sigmoid(g))

        @pl.when(c == n_chunks - 1)
        def _(hh=hh, s_new=s_new):
            so_ref[0, hh] = s_new


def _retention(proj, s0, din, dq, dk, dc, gn_gain, n_seq, n_chunks, q_rows, precise):
    t = proj.shape[0]
    kv_rows = din.shape[2]
    h = RET_HEADS
    hk, hv = h * RET_DK, h * RET_DV
    body = functools.partial(_retention_body, n_chunks=n_chunks, kv_rows=kv_rows, precise=precise)
    row = lambda b, c: b * n_chunks + c
    whole = lambda shape: pl.BlockSpec(shape, lambda b, c: (0,) * len(shape))
    return pl.pallas_call(
        body,
        grid=(n_seq, n_chunks),
        in_specs=[pl.BlockSpec((q_rows, hk), lambda b, c: (row(b, c), 0)),
                  pl.BlockSpec((q_rows, hk), lambda b, c: (row(b, c), 1)),
                  pl.BlockSpec((q_rows, hv), lambda b, c: (row(b, c), 2 * hk // hv)),
                  pl.BlockSpec((q_rows, hv), lambda b, c: (row(b, c), 2 * hk // hv + 1)),
                  pl.BlockSpec((1, h, RET_DK, RET_DV), lambda b, c: (b, 0, 0, 0)),
                  whole((h, q_rows, kv_rows)), whole((h, q_rows, 1)), whole((h, kv_rows, 1)), whole((h, 1, 1)),
                  whole((1, hv))],
        out_specs=[pl.BlockSpec((q_rows, hv), lambda b, c: (row(b, c), 0)),
                   pl.BlockSpec((1, h, RET_DK, RET_DV), lambda b, c: (b, 0, 0, 0))],
        out_shape=[jax.ShapeDtypeStruct((t, hv), F32),
                   jax.ShapeDtypeStruct((n_seq, h, RET_DK, RET_DV), F32)],
        scratch_shapes=[pltpu.VMEM((h, RET_DK, RET_DV), F32)],
        compiler_params=_cparams(("arbitrary", "arbitrary")),
        name="retention",
    )(proj, proj, proj, proj, s0, din, dq, dk, dc, gn_gain)


def _ret_decay_tables(c, q_rows, kv_rows):
    lg = jnp.log(1.0 - 2.0 ** (-5.0 - jnp.arange(RET_HEADS, dtype=F32)))
    idx = jnp.arange(c, dtype=F32)
    rel = idx[:, None] - idx[None, :]
    din = jnp.where(rel[None] >= 0, jnp.exp(lg[:, None, None] * jnp.maximum(rel, 0.0)[None]), 0.0)
    dq = jnp.exp(lg[:, None] * (idx[None, :] + 1.0))
    dk = jnp.exp(lg[:, None] * (c - 1.0 - idx[None, :]))
    dc = jnp.exp(lg * c)
    din = jnp.pad(din, ((0, 0), (0, q_rows - c), (0, kv_rows - c)))
    dq = jnp.pad(dq, ((0, 0), (0, q_rows - c)))[..., None]
    dk = jnp.pad(dk, ((0, 0), (0, kv_rows - c)))[..., None]
    return din, dq, dk, dc[:, None, None]


def _outproj_body(u_ref, w_ref, x_ref, m_ref, gpost_ref, gpre_ref, xn_ref, h2_ref):
    tm = x_ref.shape[0]
    y = _mm(u_ref[...], w_ref[...], w_ref.dtype == F32)
    xn = x_ref[...] + m_ref[2, 0] * _rms(y, gpost_ref[...])
    xn_ref[...] = xn
    h2 = _rms(xn, gpre_ref[...]) * (1.0 + m_ref[4, 0]) + m_ref[3, 0]
    _store_rows(h2_ref, h2, tm)


def _outproj(u, w, x, mods, gpost, gpre, tiles_per_seq):
    t = x.shape[0]
    din = u.shape[1]
    tm = min(TM, t)
    return pl.pallas_call(
        _outproj_body,
        grid=(t // tm,),
        in_specs=[pl.BlockSpec((tm, din), lambda i: (i, 0)),
                  pl.BlockSpec((din, D_MODEL), lambda i: (0, 0), pipeline_mode=pl.Buffered(1)),
                  pl.BlockSpec((tm, D_MODEL), lambda i: (i, 0)),
                  _mod_spec(mods, tiles_per_seq),
                  pl.BlockSpec((1, D_MODEL), lambda i: (0, 0)),
                  pl.BlockSpec((1, D_MODEL), lambda i: (0, 0))],
        out_specs=[pl.BlockSpec((tm, D_MODEL), lambda i: (i, 0)),
                   pl.BlockSpec((tm * ROW_TILES, LANES), lambda i: (i, 0))],
        out_shape=[jax.ShapeDtypeStruct((t, D_MODEL), F32),
                   jax.ShapeDtypeStruct((t * ROW_TILES, LANES), F32)],
        compiler_params=_cparams(("arbitrary",)),
        name="outproj",
    )(u, w, x, mods, gpost, gpre)


def _router_body(h_ref, w_ref, b_ref, tri_ref, idx_ref, wgt_ref, rank_ref, cnt_ref, carry):
    i = pl.program_id(0)
    tm = idx_ref.shape[0]

    @pl.when(i == 0)
    def _():
        carry[...] = jnp.zeros_like(carry)

    h = _load_rows(h_ref, tm)
    logits = _dot(h, w_ref[...], HI) + b_ref[...]
    lane = lax.broadcasted_iota(I32, (tm, LANES), 1)
    lane_f = lane.astype(F32)
    work = logits
    vals, ids, hots = [], [], []
    for _ in range(TOP_K):
        mx = jnp.max(work, axis=1, keepdims=True)
        ik = jnp.min(jnp.where(work == mx, lane_f, float(LANES)), axis=1, keepdims=True)
        hot = lane_f == ik
        vals.append(mx)
        ids.append(ik.astype(I32))
        hots.append(hot)
        work = jnp.where(hot, -jnp.inf, work)
    ex = [jnp.exp(v - vals[0]) for v in vals]
    den = ex[0] + ex[1] + ex[2] + ex[3]
    chosen = hots[0] | hots[1] | hots[2] | hots[3]
    before = _dot(tri_ref[...], chosen.astype(BF16)) + carry[...]
    idx_o = jnp.zeros((tm, LANES), I32)
    wgt_o = jnp.zeros((tm, LANES), F32)
    rank_o = jnp.zeros((tm, LANES), I32)
    for kk in range(TOP_K):
        rk = jnp.sum(jnp.where(hots[kk], before, 0.0), axis=1, keepdims=True).astype(I32)
        idx_o = jnp.where(lane == kk, ids[kk], idx_o)
        wgt_o = jnp.where(lane == kk, ex[kk] / den, wgt_o)
        rank_o = jnp.where(lane == kk, rk, rank_o)
    idx_ref[...] = idx_o
    wgt_ref[...] = wgt_o
    rank_ref[...] = rank_o
    carry[...] = carry[...] + jnp.sum(chosen.astype(F32), axis=0, keepdims=True)
    cnt_ref[...] = carry[...]


def _router(h2, w_r, b_r, tri):
    t = h2.shape[0] // ROW_TILES
    tm = TM
    return pl.pallas_call(
        _router_body,
        grid=(t // tm,),
        in_specs=[pl.BlockSpec((tm * ROW_TILES, LANES), lambda i: (i, 0)),
                  pl.BlockSpec((D_MODEL, LANES), lambda i: (0, 0)),
                  pl.BlockSpec((1, LANES), lambda i: (0, 0)),
                  pl.BlockSpec((tm, tm), lambda i: (0, 0))],
        out_specs=[pl.BlockSpec((tm, LANES), lambda i: (i, 0)),
                   pl.BlockSpec((tm, LANES), lambda i: (i, 0)),
                   pl.BlockSpec((tm, LANES), lambda i: (i, 0)),
                   pl.BlockSpec((1, LANES), lambda i: (0, 0))],
        out_shape=[jax.ShapeDtypeStruct((t, LANES), I32),
                   jax.ShapeDtypeStruct((t, LANES), F32),
                   jax.ShapeDtypeStruct((t, LANES), I32),
                   jax.ShapeDtypeStruct((1, LANES), F32)],
        scratch_shapes=[pltpu.VMEM((1, LANES), F32)],
        compiler_params=_cparams(("arbitrary",)),
        name="router",
    )(h2, w_r, b_r, tri)


def _token_copy(src, src_row, dst, dst_row, sem):
    return pltpu.make_async_copy(src.at[pl.ds(pl.multiple_of(src_row * ROW_TILES, ROW_TILES), ROW_TILES), :],
                                 dst.at[pl.ds(pl.multiple_of(dst_row * ROW_TILES, ROW_TILES), ROW_TILES), :], sem)


def _dispatch_body(dest_ref, h_ref, xs_in_ref, xs_ref, sem):
    del xs_in_ref
    tm = h_ref.shape[0] // ROW_TILES

    def issue(r, carry):
        for kk in range(TOP_K):
            _token_copy(h_ref, r, xs_ref, dest_ref[0, 0, r * TOP_K + kk], sem).start()
        return carry

    lax.fori_loop(0, tm, issue, 0)

    def drain(r, carry):
        for kk in range(TOP_K):
            _token_copy(h_ref, 0, xs_ref, 0, sem).wait()
        return carry

    lax.fori_loop(0, tm, drain, 0)


def _dispatch(dest3, h2, xs_init):
    n_tiles, _, per = dest3.shape
    tm = per // TOP_K
    return pl.pallas_call(
        _dispatch_body,
        grid=(n_tiles,),
        in_specs=[pl.BlockSpec((1, 1, per), lambda i: (i, 0, 0), memory_space=pltpu.SMEM),
                  pl.BlockSpec((tm * ROW_TILES, LANES), lambda i: (i, 0)),
                  pl.BlockSpec(memory_space=pl.ANY)],
        out_specs=pl.BlockSpec(memory_space=pl.ANY),
        out_shape=jax.ShapeDtypeStruct(xs_init.shape, F32),
        scratch_shapes=[pltpu.SemaphoreType.DMA(())],
        input_output_aliases={2: 0},
        compiler_params=_cparams(("arbitrary",)),
        name="dispatch",
    )(dest3, h2, xs_init)


def _experts_body(te_ref, na_ref, xs_ref, wgu_ref, bgu_ref, wdn_ref, bdn_ref, ys_ref, wgu_bf, wdn_bf):
    i = pl.program_id(0)
    tm = xs_ref.shape[0] // ROW_TILES
    active = i < na_ref[0]

    @pl.when(active)
    def _():
        prev = te_ref[jnp.maximum(i - 1, 0)]

        @pl.when((i == 0) | (te_ref[i] != prev))
        def _():
            wgu_bf[...] = wgu_ref[0, 0].astype(BF16)
            wdn_bf[...] = wdn_ref[0, 0].astype(BF16)

        x = _load_rows(xs_ref, tm).astype(BF16)
        gu = _dot(x, wgu_bf[...]) + bgu_ref[0, 0]
        gate = jnp.minimum(gu[:, :D_FF], SWIGLU_LIMIT)
        up = jnp.clip(gu[:, D_FF:], -SWIGLU_LIMIT, SWIGLU_LIMIT)
        act = (up + 1.0) * gate * jax.nn.sigmoid(SWIGLU_ALPHA * gate)
        y = _dot(act.astype(BF16), wdn_bf[...]) + bdn_ref[0, 0]
        _store_rows(ys_ref, y, tm)

    @pl.when(jnp.logical_not(active))
    def _():
        ys_ref[...] = jnp.zeros_like(ys_ref)


def _experts(layer, tile_expert, n_active, xs, w_gu, b_gu, w_dn, b_dn):
    n_rows = xs.shape[0] // ROW_TILES
    tm = TM_FFN
    b_gu4 = b_gu.reshape(b_gu.shape[0], N_EXPERTS, 1, 2 * D_FF)
    b_dn4 = b_dn.reshape(b_dn.shape[0], N_EXPERTS, 1, D_MODEL)
    grid_spec = pltpu.PrefetchScalarGridSpec(
        num_scalar_prefetch=2,
        grid=(n_rows // tm,),
        in_specs=[pl.BlockSpec((tm * ROW_TILES, LANES), lambda i, te, na: (i, 0)),
                  pl.BlockSpec((1, 1, D_MODEL, 2 * D_FF), lambda i, te, na: (layer, te[i], 0, 0)),
                  pl.BlockSpec((1, 1, 1, 2 * D_FF), lambda i, te, na: (layer, te[i], 0, 0)),
                  pl.BlockSpec((1, 1, D_FF, D_MODEL), lambda i, te, na: (layer, te[i], 0, 0)),
                  pl.BlockSpec((1, 1, 1, D_MODEL), lambda i, te, na: (layer, te[i], 0, 0))],
        out_specs=pl.BlockSpec((tm * ROW_TILES, LANES), lambda i, te, na: (i, 0)),
        scratch_shapes=[pltpu.VMEM((D_MODEL, 2 * D_FF), BF16), pltpu.VMEM((D_FF, D_MODEL), BF16)],
    )
    return pl.pallas_call(
        _experts_body,
        grid_spec=grid_spec,
        out_shape=jax.ShapeDtypeStruct(xs.shape, F32),
        compiler_params=_cparams(("arbitrary",)),
        name="experts",
    )(tile_expert, n_active, xs, w_gu, b_gu4, w_dn, b_dn4)


def _combine_body(dest_ref, wgt_ref, x_ref, m_ref, g_ref, ys_ref, xo_ref, buf, sem):
    tm = x_ref.shape[0]

    def issue(r, carry):
        for kk in range(TOP_K):
            _token_copy(ys_ref, dest_ref[0, 0, r * TOP_K + kk], buf.at[kk], r, sem).start()
        return carry

    lax.fori_loop(0, tm, issue, 0)

    def drain(r, carry):
        for kk in range(TOP_K):
            _token_copy(ys_ref, 0, buf.at[kk], 0, sem).wait()
        return carry

    lax.fori_loop(0, tm, drain, 0)
    w = wgt_ref[...]
    f = jnp.zeros((tm, D_MODEL), F32)
    for kk in range(TOP_K):
        f = f + w[:, kk:kk + 1] * _load_rows(buf.at[kk], tm)
    xo_ref[...] = x_ref[...] + m_ref[5, 0] * _rms(f, g_ref[...])


def _combine(dest3, wgt, x, mods, gpost, ys, tiles_per_seq):
    t = x.shape[0]
    tm = TM_COMB
    if mods.shape[2] == 1:
        mspec = pl.BlockSpec((6, 1, 1, D_MODEL), lambda i: (0, i // tiles_per_seq, 0, 0))
    else:
        mspec = pl.BlockSpec((6, 1, tm, D_MODEL), lambda i: (0, 0, i, 0))
    return pl.pallas_call(
        _combine_body,
        grid=(t // tm,),
        in_specs=[pl.BlockSpec((1, 1, tm * TOP_K), lambda i: (i, 0, 0), memory_space=pltpu.SMEM),
                  pl.BlockSpec((tm, LANES), lambda i: (i, 0)),
                  pl.BlockSpec((tm, D_MODEL), lambda i: (i, 0)),
                  mspec,
                  pl.BlockSpec((1, D_MODEL), lambda i: (0, 0)),
                  pl.BlockSpec(memory_space=pl.ANY)],
        out_specs=pl.BlockSpec((tm, D_MODEL), lambda i: (i, 0)),
        out_shape=jax.ShapeDtypeStruct((t, D_MODEL), F32),
        scratch_shapes=[pltpu.VMEM((TOP_K, tm * ROW_TILES, LANES), F32), pltpu.SemaphoreType.DMA(())],
        compiler_params=_cparams(("arbitrary",)),
        name="combine",
    )(dest3, wgt, x, mods, gpost, ys)


def _moe(layer, h2_all, t_prompt, x_p, x_s, mods_p, mods_s, gpost, w_r, b_r, tri_strict,
         w_gu, b_gu, w_dn, b_dn, tiles_per_seq_comb):
    t_all = h2_all.shape[0] // ROW_TILES
    idx_w, wgt_w, rank_w, cnt_w = _router(h2_all, w_r, b_r, tri_strict)
    idx = idx_w[:, :TOP_K]
    rank = rank_w[:, :TOP_K]
    cnt = cnt_w[0, :N_EXPERTS].astype(I32)
    padded = ((cnt + TM_FFN - 1) // TM_FFN) * TM_FFN
    ends = jnp.cumsum(padded)
    offs = ends - padded
    dest = jnp.sum(jnp.where(idx[..., None] == jnp.arange(N_EXPERTS, dtype=I32), offs, 0), axis=-1) + rank
    n_tiles = (t_all * TOP_K) // TM_FFN + N_EXPERTS
    tile_start = jnp.arange(n_tiles, dtype=I32) * TM_FFN
    tile_expert = jnp.minimum(jnp.sum((tile_start[:, None] >= ends[None, :]).astype(I32), axis=1), N_EXPERTS - 1)
    n_active = (ends[-1] // TM_FFN).astype(I32).reshape(1)

    dest_flat = dest.reshape(-1)
    xs_init = jnp.zeros((n_tiles * TM_FFN * ROW_TILES, LANES), F32)
    xs = _dispatch(dest_flat.reshape(t_all // TM, 1, TM * TOP_K), h2_all, xs_init)
    ys = _experts(layer, tile_expert, n_active, xs, w_gu, b_gu, w_dn, b_dn)

    dest_p = dest_flat[:t_prompt * TOP_K].reshape(t_prompt // TM_COMB, 1, TM_COMB * TOP_K)
    dest_s = dest_flat[t_prompt * TOP_K:].reshape(-1, 1, TM_COMB * TOP_K)
    xo_p = _combine(dest_p, wgt_w[:t_prompt], x_p, mods_p, gpost, ys, tiles_per_seq_comb)
    xo_s = _combine(dest_s, wgt_w[t_prompt:], x_s, mods_s, gpost, ys, 1)
    return xo_p, xo_s


def _fox_prompt_body(qa_ref, ka_ref, vat_ref, g_ref, o_ref, m_s, acc_s):
    i = pl.program_id(2)
    tq = qa_ref.shape[0]
    m_s[...] = jnp.full_like(m_s, -jnp.inf)
    acc_s[...] = jnp.zeros_like(acc_s)

    def block(start, diagonal):
        if diagonal:
            causal = lax.broadcasted_iota(I32, (tq, tq), 0) <= lax.broadcasted_iota(I32, (tq, tq), 1)
        for a in range(2):
            cols = slice(a * LANES, (a + 1) * LANES)
            st = _dot_nt(ka_ref[pl.ds(start, tq), cols], qa_ref[:, cols])
            if diagonal:
                st = jnp.where(causal, st, -jnp.inf)
            m_old = m_s[a]
            m_new = jnp.maximum(m_old, jnp.max(st, axis=0, keepdims=True))
            p = jnp.exp((st - m_new).astype(BF16))
            acc_s[a] = jnp.exp(m_old - m_new) * acc_s[a] + _dot(vat_ref[0, cols, pl.ds(start, tq)], p)
            m_s[a] = m_new

    def body(j, carry):
        block(pl.multiple_of(j * tq, tq), False)
        return carry

    lax.fori_loop(0, i, body, 0)
    block(pl.multiple_of(i * tq, tq), True)
    o_t = jnp.concatenate([acc_s[a][:FOX_HD, :] / acc_s[a][FOX_HD:FOX_HD + 1, :] for a in range(2)], axis=0)
    o_ref[...] = o_t.T * jax.nn.sigmoid(g_ref[...])


def _fox_prompt_attn(qa, ka, vat, gate, n_seq, seq):
    t = gate.shape[0]
    nq = seq // TQ
    pairs = FOX_HEADS // 2
    return pl.pallas_call(
        _fox_prompt_body,
        grid=(n_seq, pairs, nq),
        in_specs=[pl.BlockSpec((TQ, 2 * LANES), lambda b, hp, i: (b * nq + i, hp)),
                  pl.BlockSpec((seq, 2 * LANES), lambda b, hp, i: (b, hp)),
                  pl.BlockSpec((1, 2 * LANES, seq), lambda b, hp, i: (b, hp, 0)),
                  pl.BlockSpec((TQ, LANES), lambda b, hp, i: (b * nq + i, hp))],
        out_specs=pl.BlockSpec((TQ, LANES), lambda b, hp, i: (b * nq + i, hp)),
        out_shape=jax.ShapeDtypeStruct((t, FOX_HEADS * FOX_HD), F32),
        scratch_shapes=[pltpu.VMEM((2, 1, TQ), F32), pltpu.VMEM((2, LANES, TQ), F32)],
        compiler_params=_cparams(("arbitrary", "arbitrary", "arbitrary")),
        name="fox_prompt_attn",
    )(qa, ka, vat, gate)


def _fox_sample_body(pt_ref, q_ref, kn_ref, vn_ref, g_ref, cncol_ref, cnrow_ref, *rest):
    npg = PAGES_PER_STEP
    k_refs = rest[:npg]
    v_refs = rest[npg:2 * npg]
    f_refs = rest[2 * npg:3 * npg]
    u_ref, e_ref, o_ref, qbd_s, m_s, l_s, acc_s, car_s = rest[3 * npg:]
    del pt_ref
    j = pl.program_id(1)
    nj = pl.num_programs(1)
    rows = FOX_HEADS * TOK_PAD
    wide = FOX_HEADS * FOX_HD
    own = ((lax.broadcasted_iota(I32, (rows, wide), 0) // TOK_PAD)
           == (lax.broadcasted_iota(I32, (rows, wide), 1) // FOX_HD))

    @pl.when(j == 0)
    def _():
        m_s[...] = jnp.full_like(m_s, -jnp.inf)
        l_s[...] = jnp.zeros_like(l_s)
        acc_s[...] = jnp.zeros_like(acc_s)
        car_s[...] = jnp.zeros_like(car_s)
        q = q_ref[...] * (FOX_HD ** -0.5)
        qbd_s[...] = jnp.where(own, jnp.concatenate([q] * FOX_HEADS, axis=0), 0.0).astype(BF16)

    qbd = qbd_s[...]
    cn = cncol_ref[0]

    def update(s, pv_of):
        m_old = m_s[...]
        m_new = jnp.maximum(m_old, jnp.max(s, axis=1, keepdims=True))
        alpha = jnp.exp(m_old - m_new)
        p = jnp.exp(s - m_new)
        l_s[...] = alpha * l_s[...] + jnp.sum(p, axis=1, keepdims=True)
        acc_s[...] = alpha * acc_s[...] + pv_of(p.astype(BF16))
        m_s[...] = m_new

    suffixes = []
    for r in range(npg):
        ft = f_refs[r][0]
        suffixes.append(_dot(ft, u_ref[...], HI) + car_s[...])
        car_s[...] = car_s[...] + jnp.sum(ft, axis=1, keepdims=True)
    bias = _dot(e_ref[...], jnp.concatenate(suffixes, axis=1), HI) + cn
    kt = jnp.concatenate([k_refs[r][0].astype(BF16) for r in range(npg)], axis=1)
    vt = jnp.concatenate([v_refs[r][0].astype(BF16) for r in range(npg)], axis=1)
    update(_dot(qbd, kt) + bias, lambda pb: _dot_nt(pb, vt))

    @pl.when(j == nj - 1)
    def _():
        zpad = jnp.zeros((PAGE_SIZE - TOK_PAD, wide), F32)
        kn = jnp.concatenate([kn_ref[...], zpad], axis=0).astype(BF16)
        vn = jnp.concatenate([vn_ref[...], zpad], axis=0).astype(BF16)
        trow = lax.broadcasted_iota(I32, (rows, PAGE_SIZE), 0) & (TOK_PAD - 1)
        tcol = lax.broadcasted_iota(I32, (rows, PAGE_SIZE), 1)
        s = jnp.where(tcol <= trow, _dot_nt(qbd, kn) + cn - cnrow_ref[0], -jnp.inf)
        update(s, lambda pb: _dot(pb, vn))
        o = jnp.where(own, acc_s[...] / l_s[...], 0.0)
        o8 = o[0:TOK_PAD, :]
        for h in range(1, FOX_HEADS):
            o8 = o8 + o[h * TOK_PAD:(h + 1) * TOK_PAD, :]
        o_ref[...] = o8 * jax.nn.sigmoid(g_ref[...])


def _fox_sample_attn(proj, cncol, cnrow, cache_k, cache_v, cache_ft, page_table, u_mat, e_mat):
    n_seq, n_pages = page_table.shape
    npg = PAGES_PER_STEP
    steps = n_pages // npg
    wide = FOX_HEADS * FOX_HD
    rows = FOX_HEADS * TOK_PAD

    def page(r):
        return lambda b, j, pt: pt[b * n_pages + (n_pages - 1 - (j * npg + r))]

    kv_specs = [pl.BlockSpec((1, wide, PAGE_SIZE), (lambda b, j, pt, r=r: (page(r)(b, j, pt), 0, 0)))
                for r in range(npg)]
    f_specs = [pl.BlockSpec((1, FOX_HEADS, PAGE_SIZE), (lambda b, j, pt, r=r: (page(r)(b, j, pt), 0, 0)))
               for r in range(npg)]
    grid_spec = pltpu.PrefetchScalarGridSpec(
        num_scalar_prefetch=1,
        grid=(n_seq, steps),
        in_specs=[pl.BlockSpec((TOK_PAD, wide), lambda b, j, pt: (b, 0)),
                  pl.BlockSpec((TOK_PAD, wide), lambda b, j, pt: (b, 1)),
                  pl.BlockSpec((TOK_PAD, wide), lambda b, j, pt: (b, 2)),
                  pl.BlockSpec((TOK_PAD, wide), lambda b, j, pt: (b, 3)),
                  pl.BlockSpec((1, rows, 1), lambda b, j, pt: (b, 0, 0)),
                  pl.BlockSpec((1, rows, PAGE_SIZE), lambda b, j, pt: (b, 0, 0))]
                 + kv_specs + kv_specs + f_specs
                 + [pl.BlockSpec((PAGE_SIZE, PAGE_SIZE), lambda b, j, pt: (0, 0)),
                    pl.BlockSpec((rows, FOX_HEADS), lambda b, j, pt: (0, 0))],
        out_specs=pl.BlockSpec((TOK_PAD, wide), lambda b, j, pt: (b, 0)),
        scratch_shapes=[pltpu.VMEM((rows, wide), BF16), pltpu.VMEM((rows, 1), F32), pltpu.VMEM((rows, 1), F32),
                        pltpu.VMEM((rows, wide), F32), pltpu.VMEM((FOX_HEADS, 1), F32)],
    )
    return pl.pallas_call(
        _fox_sample_body,
        grid_spec=grid_spec,
        out_shape=jax.ShapeDtypeStruct((n_seq * TOK_PAD, wide), F32),
        compiler_params=_cparams(("arbitrary", "arbitrary")),
        name="fox_sample_attn",
    )(page_table.reshape(-1), proj, proj, proj, proj, cncol, cnrow,
      *([cache_k] * npg), *([cache_v] * npg), *([cache_ft] * npg), u_mat, e_mat)


def _rope_tables(pos):
    half = RET_DK // 2
    inv = ROPE_BASE ** (-jnp.arange(half, dtype=F32) / half)
    ang = pos.astype(F32)[:, None] * inv[None, :]
    return jnp.cos(ang), jnp.sin(ang)


def _mods_prompt(m):
    b = m.shape[0]
    return m.reshape(b, 6, D_MODEL).transpose(1, 0, 2)[:, :, None, :]


def _mods_sample(m):
    b = m.shape[0]
    mm = jnp.repeat(m.reshape(b, 6, D_MODEL), TOK_PAD, axis=0)
    return mm.transpose(1, 0, 2)[:, None, :, :]


def kernel(x_prompt, x_sample, c_prompt, c_sample, state_ret, cache_fox_k, cache_fox_v, cache_fox_logf, page_table, w_ada, b_ada, norm_pre_mix, norm_post_mix, norm_pre_ffn, norm_post_ffn, ret_w_in, ret_gn_gain, ret_w_out, fox_w_in, fox_b_f, fox_w_out, moe_w_router, moe_b_router, moe_w_gate_up, moe_b_gate_up, moe_w_down, moe_b_down):
    n_seq, seq, d = x_prompt.shape
    n_dec, dec_seq, _ = x_sample.shape
    t_p = n_seq * seq
    t_s = n_dec * TOK_PAD
    tps = seq // TM

    xp = x_prompt.reshape(t_p, d)
    xs = jnp.pad(x_sample, ((0, 0), (0, TOK_PAD - dec_seq), (0, 0))).reshape(t_s, d)

    m_all = _adaln(jnp.concatenate([c_prompt, c_sample], axis=0), w_ada, b_ada)

    ar = jnp.arange(TM)
    tri_incl = (ar[None, :] <= ar[:, None]).astype(F32)
    tri_strict = (ar[None, :] < ar[:, None]).astype(BF16)
    tri_group = ((ar[None, :] <= ar[:, None]) & (ar[None, :] // TOK_PAD == ar[:, None] // TOK_PAD)).astype(F32)

    mods_p = _mods_prompt(m_all[0, :n_seq])
    mods_s = _mods_sample(m_all[0, n_seq:])
    w_in = ret_w_in[0].astype(BF16)
    cos_p, sin_p = _rope_tables(jnp.arange(seq))
    pos_s = jnp.tile(PAST_LEN + jnp.arange(TOK_PAD), n_dec)
    cos_s, sin_s = _rope_tables(pos_s)
    g_pre = norm_pre_mix[0:1]
    proj_p = _inproj_ret(xp, mods_p, g_pre, w_in, cos_p, sin_p, tps)
    proj_s = _inproj_ret(xs, mods_s, g_pre, ret_w_in[0], cos_s, sin_s, 1)

    n_chunks = seq // RET_CHUNK
    din, dq, dk, dc = _ret_decay_tables(RET_CHUNK, RET_CHUNK, RET_CHUNK)
    s0p = jnp.zeros((n_seq, RET_HEADS, RET_DK, RET_DV), F32)
    u_p, st_p = _retention(proj_p, s0p, din, dq, dk, dc, ret_gn_gain, n_seq, n_chunks, RET_CHUNK, False)
    din, dq, dk, dc = _ret_decay_tables(dec_seq, TOK_PAD, RET_CHUNK)
    u_s, st_s = _retention(proj_s, state_ret[0], din, dq, dk, dc, ret_gn_gain, n_dec, 1, TOK_PAD, True)

    w_out = ret_w_out[0].astype(BF16)
    xp, h2p = _outproj(u_p, w_out, xp, mods_p, norm_post_mix[0:1], norm_pre_ffn[0:1], tps)
    xs, h2s = _outproj(u_s, ret_w_out[0], xs, mods_s, norm_post_mix[0:1], norm_pre_ffn[0:1], 1)

    def router_params(i):
        w_r = jnp.pad(moe_w_router[i], ((0, 0), (0, LANES - N_EXPERTS)))
        b_r = jnp.pad(moe_b_router[i], (0, LANES - N_EXPERTS), constant_values=-1e30)[None, :]
        return w_r, b_r

    w_r, b_r = router_params(0)
    xp, xs = _moe(0, jnp.concatenate([h2p, h2s], axis=0), t_p, xp, xs, mods_p, mods_s, norm_post_ffn[0:1],
                  w_r, b_r, tri_strict, moe_w_gate_up, moe_b_gate_up, moe_w_down, moe_b_down, seq // TM_COMB)

    mods_p = _mods_prompt(m_all[1, :n_seq])
    mods_s = _mods_sample(m_all[1, n_seq:])
    fw = FOX_HEADS * FOX_HD
    w_in = fox_w_in[0, :, :4 * fw].astype(BF16)
    w_fl = jnp.pad(fox_w_in[0, :, 4 * fw:], ((0, 0), (0, LANES - FOX_HEADS)))
    b_fl = jnp.pad(fox_b_f[0], (0, LANES - FOX_HEADS))[None, :]
    g_pre = norm_pre_mix[1:2]
    wkt = fox_w_in[0, :, fw:2 * fw].T.astype(BF16)
    wvt = fox_w_in[0, :, 2 * fw:3 * fw].T.astype(BF16)
    gate_p, lf_p, qa, ka, vat, kt_p, vt_p = _inproj_fox(xp, mods_p, g_pre, w_in, w_fl, b_fl, tri_incl, tps, wkt, wvt)
    fproj_s, lf_s, cum_s = _inproj_fox(xs, mods_s, g_pre, w_in, w_fl, b_fl, tri_group, 1)
    o_p = _fox_prompt_attn(qa, ka, vat, gate_p, n_seq, seq)

    cn = cum_s[:, :FOX_HEADS].reshape(n_dec, TOK_PAD, FOX_HEADS).transpose(0, 2, 1)
    cncol = cn.reshape(n_dec, FOX_HEADS * TOK_PAD, 1)
    cnrow = jnp.repeat(cn, TOK_PAD, axis=1)
    cnrow = jnp.pad(cnrow, ((0, 0), (0, 0), (0, PAGE_SIZE - TOK_PAD)))
    pr = jnp.arange(PAGE_SIZE)
    u_mat = (pr[:, None] > pr[None, :]).astype(F32)
    e_mat = (jnp.arange(FOX_HEADS * TOK_PAD)[:, None] // TOK_PAD == jnp.arange(FOX_HEADS)[None, :]).astype(F32)
    n_pool = cache_fox_k.shape[1]
    cache_kt = cache_fox_k[0].transpose(0, 2, 3, 1).reshape(n_pool, fw, PAGE_SIZE)
    cache_vt = cache_fox_v[0].transpose(0, 2, 3, 1).reshape(n_pool, fw, PAGE_SIZE)
    cache_ft = cache_fox_logf[0].transpose(0, 2, 1)
    o_s = _fox_sample_attn(fproj_s, cncol, cnrow, cache_kt, cache_vt, cache_ft, page_table, u_mat, e_mat)

    w_out = fox_w_out[0].astype(BF16)
    xp, h2p = _outproj(o_p, w_out, xp, mods_p, norm_post_mix[1:2], norm_pre_ffn[1:2], tps)
    xs, h2s = _outproj(o_s, w_out, xs, mods_s, norm_post_mix[1:2], norm_pre_ffn[1:2], 1)
    w_r, b_r = router_params(1)
    xp, xs = _moe(1, jnp.concatenate([h2p, h2s], axis=0), t_p, xp, xs, mods_p, mods_s, norm_post_ffn[1:2],
                  w_r, b_r, tri_strict, moe_w_gate_up, moe_b_gate_up, moe_w_down, moe_b_down, seq // TM_COMB)

    y_prompt = xp.reshape(n_seq, seq, d)
    y_sample = xs.reshape(n_dec, TOK_PAD, d)[:, :dec_seq]
    k_prompt = kt_p.reshape(n_seq, FOX_HEADS, FOX_HD, seq).transpose(0, 3, 1, 2)[None]
    v_prompt = vt_p.reshape(n_seq, FOX_HEADS, FOX_HD, seq).transpose(0, 3, 1, 2)[None]
    logf_prompt = lf_p[:, :FOX_HEADS].reshape(1, n_seq, seq, FOX_HEADS)
    fs = fproj_s.reshape(n_dec, TOK_PAD, 4 * fw)[:, :dec_seq]
    kv_shape_s = (1, n_dec, dec_seq, FOX_HEADS, FOX_HD)
    k_sample = fs[..., fw:2 * fw].reshape(kv_shape_s)
    v_sample = fs[..., 2 * fw:3 * fw].reshape(kv_shape_s)
    logf_sample = lf_s.reshape(n_dec, TOK_PAD, LANES)[:, :dec_seq, :FOX_HEADS][None]
    return (y_prompt, y_sample, st_p[None], st_s[None], k_prompt, v_prompt, logf_prompt,
            k_sample, v_sample, logf_sample)
```

```python
import functools

import jax
import jax.numpy as jnp
from jax import lax
from jax.experimental import pallas as pl
from jax.experimental.pallas import tpu as pltpu

F32 = jnp.float32
BF16 = jnp.bfloat16
I32 = jnp.int32
HI = lax.Precision.HIGHEST

D_MODEL = 1024
PAST_LEN = 8192
PAGE_SIZE = 128
RET_HEADS = 4
RET_DK = D_MODEL // RET_HEADS
RET_DV = 2 * RET_DK
RET_CHUNK = 128
ROPE_BASE = 10000.0
FOX_HEADS = 16
FOX_HD = D_MODEL // FOX_HEADS
N_EXPERTS = 32
TOP_K = 4
D_FF = D_MODEL
SWIGLU_LIMIT = 7.0
SWIGLU_ALPHA = 1.702
NORM_EPS = 1e-6

LANES = 128
SUBLANES = 8
ROW_TILES = D_MODEL // LANES
TOK_PAD = SUBLANES
TM = 256
TM_FFN = 512
TM_COMB = 128
TQ = 512
PAGES_PER_STEP = 8
VMEM_LIMIT = 56 * 1024 * 1024


def _cparams(sem, vmem=VMEM_LIMIT):
    return pltpu.CompilerParams(dimension_semantics=sem, vmem_limit_bytes=vmem)


def _rms(x, gain):
    return x * lax.rsqrt(jnp.mean(x * x, axis=-1, keepdims=True) + NORM_EPS) * gain


def _dot(a, b, precision=None):
    return jnp.dot(a, b, precision=precision, preferred_element_type=F32)


def _dot_nt(a, b):
    return lax.dot_general(a, b, (((1,), (1,)), ((), ())), preferred_element_type=F32)


def _mm(a, b, precise, dims=None):
    if precise:
        a, b, prec = a.astype(F32), b.astype(F32), HI
    else:
        a, b, prec = a.astype(BF16), b.astype(BF16), None
    if dims is None:
        return jnp.dot(a, b, precision=prec, preferred_element_type=F32)
    return lax.dot_general(a, b, (dims, ((), ())), precision=prec, preferred_element_type=F32)


_NT = ((1,), (1,))
_TN = ((0,), (0,))


def _load_rows(ref, rows):
    return jnp.concatenate([ref[pl.ds(s, rows, stride=ROW_TILES), :] for s in range(ROW_TILES)], axis=1)


def _store_rows(ref, val, rows):
    for s in range(ROW_TILES):
        ref[pl.ds(s, rows, stride=ROW_TILES), :] = val[:, s * LANES:(s + 1) * LANES]


def _ada_body(c_ref, w_ref, b_ref, o_ref):
    c = c_ref[...]
    a = c * jax.nn.sigmoid(c)
    o_ref[0] = _dot(a, w_ref[0], HI) + b_ref[0]


def _adaln(c_all, w_ada, b_ada):
    depth, d, n = w_ada.shape
    nb = c_all.shape[0]
    tn = 768
    return pl.pallas_call(
        _ada_body,
        grid=(depth, n // tn),
        in_specs=[pl.BlockSpec((nb, d), lambda l, j: (0, 0)),
                  pl.BlockSpec((1, d, tn), lambda l, j: (l, 0, j)),
                  pl.BlockSpec((1, 1, tn), lambda l, j: (l, 0, j))],
        out_specs=pl.BlockSpec((1, nb, tn), lambda l, j: (l, 0, j)),
        out_shape=jax.ShapeDtypeStruct((depth, nb, n), F32),
        compiler_params=_cparams(("arbitrary", "arbitrary")),
        name="adaln",
    )(c_all, w_ada, b_ada.reshape(depth, 1, n))


def _mod_spec(mods, tiles_per_seq):
    r = mods.shape[2]
    if r == 1:
        return pl.BlockSpec((6, 1, 1, D_MODEL), lambda i: (0, i // tiles_per_seq, 0, 0))
    return pl.BlockSpec((6, 1, r, D_MODEL), lambda i: (0, 0, 0, 0))


def _inproj_ret_body(x_ref, m_ref, g_ref, w_ref, cos_ref, sin_ref, o_ref, *, precise):
    h = _rms(x_ref[...], g_ref[...]) * (1.0 + m_ref[1, 0]) + m_ref[0, 0]
    hb = h if precise else h.astype(BF16)
    cos = cos_ref[...]
    sin = sin_ref[...]
    half = RET_DK // 2
    for c in range(2 * RET_HEADS):
        lo = c * RET_DK
        r = _mm(hb, w_ref[:, lo:lo + RET_DK], precise)
        x1 = r[:, :half]
        x2 = r[:, half:]
        sc = RET_DK ** -0.5 if c < RET_HEADS else 1.0
        o_ref[:, lo:lo + half] = (x1 * cos - x2 * sin) * sc
        o_ref[:, lo + half:lo + RET_DK] = (x1 * sin + x2 * cos) * sc
    base = 2 * RET_HEADS * RET_DK
    for c in range(2 * RET_HEADS):
        lo = base + c * RET_DV
        o_ref[:, lo:lo + RET_DV] = _mm(hb, w_ref[:, lo:lo + RET_DV], precise)


def _inproj_ret(x, mods, gain, w, cos, sin, tiles_per_seq):
    t = x.shape[0]
    n = w.shape[1]
    tm = min(TM, t)
    pos_tiles = cos.shape[0] // tm
    return pl.pallas_call(
        functools.partial(_inproj_ret_body, precise=w.dtype == F32),
        grid=(t // tm,),
        in_specs=[pl.BlockSpec((tm, D_MODEL), lambda i: (i, 0)),
                  _mod_spec(mods, tiles_per_seq),
                  pl.BlockSpec((1, D_MODEL), lambda i: (0, 0)),
                  pl.BlockSpec((D_MODEL, n), lambda i: (0, 0), pipeline_mode=pl.Buffered(1)),
                  pl.BlockSpec((tm, RET_DK // 2), lambda i: (i % pos_tiles, 0)),
                  pl.BlockSpec((tm, RET_DK // 2), lambda i: (i % pos_tiles, 0))],
        out_specs=pl.BlockSpec((tm, n), lambda i: (i, 0)),
        out_shape=jax.ShapeDtypeStruct((t, n), F32),
        compiler_params=_cparams(("arbitrary",)),
        name="inproj_ret",
    )(x, mods, gain, w, cos, sin)


def _split3(c):
    hi = c.astype(BF16).astype(F32)
    r = c - hi
    mid = r.astype(BF16).astype(F32)
    return hi, mid, r - mid


def _emit_attention_operands(o_ref, cs, qa_ref, ka_ref):
    tm = cs.shape[0]
    fw = FOX_HEADS * FOX_HD
    lane = lax.broadcasted_iota(I32, (tm, LANES), 1)
    low = lane < FOX_HD
    ones_q = jnp.where((lane >= FOX_HD + 3) & (lane < FOX_HD + 6), 1.0, 0.0)
    ones_k = jnp.where((lane >= FOX_HD) & (lane < FOX_HD + 3), 1.0, 0.0)
    for p in range(FOX_HEADS // 2):
        pairs = [o_ref[:, s * fw + p * LANES:s * fw + (p + 1) * LANES] for s in range(2)]
        pairs[0] = pairs[0] * (FOX_HD ** -0.5)
        for a in range(2):
            hh = 2 * p + a
            qh, kh = pairs if a == 0 else [pltpu.roll(x, FOX_HD, 1) for x in pairs]
            c = jnp.sum(jnp.where(lane == hh, cs, 0.0), axis=1, keepdims=True)
            hi, mid, lo = _split3(c)
            fq = jnp.where(lane == FOX_HD, hi, jnp.where(lane == FOX_HD + 1, mid,
                                                         jnp.where(lane == FOX_HD + 2, lo, ones_q)))
            fk = jnp.where(lane == FOX_HD + 3, -hi, jnp.where(lane == FOX_HD + 4, -mid,
                                                              jnp.where(lane == FOX_HD + 5, -lo, ones_k)))
            cols = slice(hh * LANES, (hh + 1) * LANES)
            qa_ref[:, cols] = jnp.where(low, qh, fq).astype(BF16)
            ka_ref[:, cols] = jnp.where(low, kh, fk).astype(BF16)


def _inproj_fox_body(x_ref, m_ref, g_ref, w_ref, wfl_ref, bf_ref, tri_ref, *rest, tiles_per_seq, prompt):
    if prompt:
        wkt_ref, wvt_ref, gate_ref, lf_ref, qa_ref, ka_ref, vat_ref, kt_ref, vt_ref, stage, carry = rest
    else:
        stage, lf_ref, cum_ref = rest
    i = pl.program_id(0)
    tm = x_ref.shape[0]
    fw = FOX_HEADS * FOX_HD
    h = _rms(x_ref[...], g_ref[...]) * (1.0 + m_ref[1, 0]) + m_ref[0, 0]
    hb = h.astype(BF16)
    cw = 512
    for c in range(4 * fw // cw):
        if prompt and 2 * fw <= c * cw < 3 * fw:
            continue
        r = _dot(hb, w_ref[:, c * cw:(c + 1) * cw])
        if prompt and c * cw >= 3 * fw:
            gate_ref[:, c * cw - 3 * fw:(c + 1) * cw - 3 * fw] = r
        else:
            stage[:, c * cw:(c + 1) * cw] = r
    fl = _dot(h, wfl_ref[...], HI) + bf_ref[...]
    lf = jnp.minimum(fl, 0.0) - jnp.log1p(jnp.exp(-jnp.abs(fl)))
    lf_ref[...] = lf
    cs = _dot(tri_ref[...], lf, HI)
    if not prompt:
        cum_ref[...] = cs
        return

    @pl.when(i % tiles_per_seq == 0)
    def _():
        carry[...] = jnp.zeros_like(carry)
    cs = cs + carry[...]
    carry[...] = cs[tm - 1:tm, :]
    _emit_attention_operands(stage, cs, qa_ref, ka_ref)
    kt_ref[0] = _dot_nt(wkt_ref[...], hb)
    vt = _dot_nt(wvt_ref[...], hb)
    vt_ref[0] = vt
    ones_rows = jnp.where(lax.broadcasted_iota(I32, (LANES - FOX_HD, tm), 0) == 0, 1.0, 0.0)
    for hh in range(FOX_HEADS):
        vat_ref[0, hh * LANES:(hh + 1) * LANES, :] = jnp.concatenate(
            [vt[hh * FOX_HD:(hh + 1) * FOX_HD, :], ones_rows], axis=0).astype(BF16)


def _inproj_fox(x, mods, gain, w_bf, w_fl, b_fl, tri, tiles_per_seq, wkt=None, wvt=None):
    t = x.shape[0]
    n = w_bf.shape[1]
    tm = min(TM, t)
    prompt = wkt is not None
    body = functools.partial(_inproj_fox_body, tiles_per_seq=tiles_per_seq, prompt=prompt)
    fw = FOX_HEADS * FOX_HD
    wide = FOX_HEADS * LANES
    row = lambda width: pl.BlockSpec((tm, width), lambda i: (i, 0))
    in_specs = [row(D_MODEL),
                _mod_spec(mods, tiles_per_seq),
                pl.BlockSpec((1, D_MODEL), lambda i: (0, 0)),
                pl.BlockSpec((D_MODEL, n), lambda i: (0, 0), pipeline_mode=pl.Buffered(1)),
                pl.BlockSpec((D_MODEL, LANES), lambda i: (0, 0)),
                pl.BlockSpec((1, LANES), lambda i: (0, 0)),
                pl.BlockSpec((tm, tm), lambda i: (0, 0))]
    args = [x, mods, gain, w_bf, w_fl, b_fl, tri]
    if prompt:
        n_seq = t // (tiles_per_seq * tm)
        tspec = pl.BlockSpec((1, fw, tm), lambda i: (i // tiles_per_seq, 0, i % tiles_per_seq))
        vspec = pl.BlockSpec((1, wide, tm), lambda i: (i // tiles_per_seq, 0, i % tiles_per_seq))
        in_specs += [pl.BlockSpec((fw, D_MODEL), lambda i: (0, 0), pipeline_mode=pl.Buffered(1))] * 2
        args += [wkt, wvt]
        out_specs = [row(fw), row(LANES), row(wide), row(wide), vspec, tspec, tspec]
        out_shape = ([jax.ShapeDtypeStruct((t, fw), F32), jax.ShapeDtypeStruct((t, LANES), F32)]
                     + [jax.ShapeDtypeStruct((t, wide), BF16)] * 2
                     + [jax.ShapeDtypeStruct((n_seq, wide, tiles_per_seq * tm), BF16)]
                     + [jax.ShapeDtypeStruct((n_seq, fw, tiles_per_seq * tm), F32)] * 2)
        scratch = [pltpu.VMEM((tm, 2 * fw), F32), pltpu.VMEM((1, LANES), F32)]
    else:
        out_specs = [row(n), row(LANES), row(LANES)]
        out_shape = [jax.ShapeDtypeStruct((t, n), F32), jax.ShapeDtypeStruct((t, LANES), F32),
                     jax.ShapeDtypeStruct((t, LANES), F32)]
        scratch = []
    return pl.pallas_call(
        body,
        grid=(t // tm,),
        in_specs=in_specs,
        out_specs=out_specs,
        out_shape=out_shape,
        scratch_shapes=scratch,
        compiler_params=_cparams(("arbitrary",)),
        name="inproj_fox",
    )(*args)


def _retention_body(q_ref, k_ref, v_ref, g_ref, s0_ref, din_ref, dq_ref, dk_ref, dc_ref, gn_ref,
                    u_ref, so_ref, state, *, n_chunks, kv_rows, precise):
    c = pl.program_id(1)

    @pl.when(c == 0)
    def _():
        state[...] = s0_ref[0]

    for hh in range(RET_HEADS):
        qk = slice(hh * RET_DK, (hh + 1) * RET_DK)
        vg = slice(hh * RET_DV, (hh + 1) * RET_DV)
        q = q_ref[:, qk]
        k = k_ref[:, qk]
        v = v_ref[:, vg]
        rows = k.shape[0]
        if rows < kv_rows:
            k = jnp.concatenate([k, jnp.zeros((kv_rows - rows, RET_DK), F32)], axis=0)
            v = jnp.concatenate([v, jnp.zeros((kv_rows - rows, RET_DV), F32)], axis=0)
        kd = k * dk_ref[hh]
        s_old = state[hh]
        inner = _mm(q, k, precise, _NT) * din_ref[hh]
        o = _mm(inner, v, precise) + _mm(q, s_old, precise) * dq_ref[hh]
        s_new = s_old * dc_ref[hh] + _mm(kd, v, False, _TN)
        state[hh] = s_new
        g = g_ref[:, vg]
        u_ref[:, vg] = _rms(o, gn_ref[:, vg]) * (g * jax.nn.sigmoid(g))

        @pl.when(c == n_chunks - 1)
        def _(hh=hh, s_new=s_new):
            so_ref[0, hh] = s_new


def _retention(proj, s0, din, dq, dk, dc, gn_gain, n_seq, n_chunks, q_rows, precise):
    t = proj.shape[0]
    kv_rows = din.shape[2]
    h = RET_HEADS
    hk, hv = h * RET_DK, h * RET_DV
    body = functools.partial(_retention_body, n_chunks=n_chunks, kv_rows=kv_rows, precise=precise)
    row = lambda b, c: b * n_chunks + c
    whole = lambda shape: pl.BlockSpec(shape, lambda b, c: (0,) * len(shape))
    return pl.pallas_call(
        body,
        grid=(n_seq, n_chunks),
        in_specs=[pl.BlockSpec((q_rows, hk), lambda b, c: (row(b, c), 0)),
                  pl.BlockSpec((q_rows, hk), lambda b, c: (row(b, c), 1)),
                  pl.BlockSpec((q_rows, hv), lambda b, c: (row(b, c), 2 * hk // hv)),
                  pl.BlockSpec((q_rows, hv), lambda b, c: (row(b, c), 2 * hk // hv + 1)),
                  pl.BlockSpec((1, h, RET_DK, RET_DV), lambda b, c: (b, 0, 0, 0)),
                  whole((h, q_rows, kv_rows)), whole((h, q_rows, 1)), whole((h, kv_rows, 1)), whole((h, 1, 1)),
                  whole((1, hv))],
        out_specs=[pl.BlockSpec((q_rows, hv), lambda b, c: (row(b, c), 0)),
                   pl.BlockSpec((1, h, RET_DK, RET_DV), lambda b, c: (b, 0, 0, 0))],
        out_shape=[jax.ShapeDtypeStruct((t, hv), F32),
                   jax.ShapeDtypeStruct((n_seq, h, RET_DK, RET_DV), F32)],
        scratch_shapes=[pltpu.VMEM((h, RET_DK, RET_DV), F32)],
        compiler_params=_cparams(("arbitrary", "arbitrary")),
        name="retention",
    )(proj, proj, proj, proj, s0, din, dq, dk, dc, gn_gain)


def _ret_decay_tables(c, q_rows, kv_rows):
    lg = jnp.log(1.0 - 2.0 ** (-5.0 - jnp.arange(RET_HEADS, dtype=F32)))
    idx = jnp.arange(c, dtype=F32)
    rel = idx[:, None] - idx[None, :]
    din = jnp.where(rel[None] >= 0, jnp.exp(lg[:, None, None] * jnp.maximum(rel, 0.0)[None]), 0.0)
    dq = jnp.exp(lg[:, None] * (idx[None, :] + 1.0))
    dk = jnp.exp(lg[:, None] * (c - 1.0 - idx[None, :]))
    dc = jnp.exp(lg * c)
    din = jnp.pad(din, ((0, 0), (0, q_rows - c), (0, kv_rows - c)))
    dq = jnp.pad(dq, ((0, 0), (0, q_rows - c)))[..., None]
    dk = jnp.pad(dk, ((0, 0), (0, kv_rows - c)))[..., None]
    return din, dq, dk, dc[:, None, None]


def _outproj_body(u_ref, w_ref, x_ref, m_ref, gpost_ref, gpre_ref, xn_ref, h2_ref):
    tm = x_ref.shape[0]
    y = _mm(u_ref[...], w_ref[...], w_ref.dtype == F32)
    xn = x_ref[...] + m_ref[2, 0] * _rms(y, gpost_ref[...])
    xn_ref[...] = xn
    h2 = _rms(xn, gpre_ref[...]) * (1.0 + m_ref[4, 0]) + m_ref[3, 0]
    _store_rows(h2_ref, h2, tm)


def _outproj(u, w, x, mods, gpost, gpre, tiles_per_seq):
    t = x.shape[0]
    din = u.shape[1]
    tm = min(TM, t)
    return pl.pallas_call(
        _outproj_body,
        grid=(t // tm,),
        in_specs=[pl.BlockSpec((tm, din), lambda i: (i, 0)),
                  pl.BlockSpec((din, D_MODEL), lambda i: (0, 0), pipeline_mode=pl.Buffered(1)),
                  pl.BlockSpec((tm, D_MODEL), lambda i: (i, 0)),
                  _mod_spec(mods, tiles_per_seq),
                  pl.BlockSpec((1, D_MODEL), lambda i: (0, 0)),
                  pl.BlockSpec((1, D_MODEL), lambda i: (0, 0))],
        out_specs=[pl.BlockSpec((tm, D_MODEL), lambda i: (i, 0)),
                   pl.BlockSpec((tm * ROW_TILES, LANES), lambda i: (i, 0))],
        out_shape=[jax.ShapeDtypeStruct((t, D_MODEL), F32),
                   jax.ShapeDtypeStruct((t * ROW_TILES, LANES), F32)],
        compiler_params=_cparams(("arbitrary",)),
        name="outproj",
    )(u, w, x, mods, gpost, gpre)


def _router_body(h_ref, w_ref, b_ref, tri_ref, idx_ref, wgt_ref, rank_ref, cnt_ref, carry):
    i = pl.program_id(0)
    tm = idx_ref.shape[0]

    @pl.when(i == 0)
    def _():
        carry[...] = jnp.zeros_like(carry)

    h = _load_rows(h_ref, tm)
    logits = _dot(h, w_ref[...], HI) + b_ref[...]
    lane = lax.broadcasted_iota(I32, (tm, LANES), 1)
    lane_f = lane.astype(F32)
    work = logits
    vals, ids, hots = [], [], []
    for _ in range(TOP_K):
        mx = jnp.max(work, axis=1, keepdims=True)
        ik = jnp.min(jnp.where(work == mx, lane_f, float(LANES)), axis=1, keepdims=True)
        hot = lane_f == ik
        vals.append(mx)
        ids.append(ik.astype(I32))
        hots.append(hot)
        work = jnp.where(hot, -jnp.inf, work)
    ex = [jnp.exp(v - vals[0]) for v in vals]
    den = ex[0] + ex[1] + ex[2] + ex[3]
    chosen = hots[0] | hots[1] | hots[2] | hots[3]
    before = _dot(tri_ref[...], chosen.astype(BF16)) + carry[...]
    idx_o = jnp.zeros((tm, LANES), I32)
    wgt_o = jnp.zeros((tm, LANES), F32)
    rank_o = jnp.zeros((tm, LANES), I32)
    for kk in range(TOP_K):
        rk = jnp.sum(jnp.where(hots[kk], before, 0.0), axis=1, keepdims=True).astype(I32)
        idx_o = jnp.where(lane == kk, ids[kk], idx_o)
        wgt_o = jnp.where(lane == kk, ex[kk] / den, wgt_o)
        rank_o = jnp.where(lane == kk, rk, rank_o)
    idx_ref[...] = idx_o
    wgt_ref[...] = wgt_o
    rank_ref[...] = rank_o
    carry[...] = carry[...] + jnp.sum(chosen.astype(F32), axis=0, keepdims=True)
    cnt_ref[...] = carry[...]


def _router(h2, w_r, b_r, tri):
    t = h2.shape[0] // ROW_TILES
    tm = TM
    return pl.pallas_call(
        _router_body,
        grid=(t // tm,),
        in_specs=[pl.BlockSpec((tm * ROW_TILES, LANES), lambda i: (i, 0)),
                  pl.BlockSpec((D_MODEL, LANES), lambda i: (0, 0)),
                  pl.BlockSpec((1, LANES), lambda i: (0, 0)),
                  pl.BlockSpec((tm, tm), lambda i: (0, 0))],
        out_specs=[pl.BlockSpec((tm, LANES), lambda i: (i, 0)),
                   pl.BlockSpec((tm, LANES), lambda i: (i, 0)),
                   pl.BlockSpec((tm, LANES), lambda i: (i, 0)),
                   pl.BlockSpec((1, LANES), lambda i: (0, 0))],
        out_shape=[jax.ShapeDtypeStruct((t, LANES), I32),
                   jax.ShapeDtypeStruct((t, LANES), F32),
                   jax.ShapeDtypeStruct((t, LANES), I32),
                   jax.ShapeDtypeStruct((1, LANES), F32)],
        scratch_shapes=[pltpu.VMEM((1, LANES), F32)],
        compiler_params=_cparams(("arbitrary",)),
        name="router",
    )(h2, w_r, b_r, tri)


def _token_copy(src, src_row, dst, dst_row, sem):
    return pltpu.make_async_copy(src.at[pl.ds(pl.multiple_of(src_row * ROW_TILES, ROW_TILES), ROW_TILES), :],
                                 dst.at[pl.ds(pl.multiple_of(dst_row * ROW_TILES, ROW_TILES), ROW_TILES), :], sem)


def _dispatch_body(ends_ref, padded_ref, dest_ref, h_ref, xs_ref, zbuf, sem, zsem):
    tm = h_ref.shape[0] // ROW_TILES

    @pl.when(pl.program_id(0) == 0)
    def _():
        zbuf[...] = jnp.zeros_like(zbuf)

        def window(e):
            first = pl.multiple_of((ends_ref[e] - TM_FFN) * ROW_TILES, ROW_TILES)
            return pltpu.make_async_copy(zbuf, xs_ref.at[pl.ds(first, TM_FFN * ROW_TILES), :], zsem)

        for e in range(N_EXPERTS):
            @pl.when(padded_ref[e] > 0)
            def _(e=e):
                window(e).start()
        def tile(t):
            return pltpu.make_async_copy(
                zbuf, xs_ref.at[pl.ds(pl.multiple_of(t * TM_FFN * ROW_TILES, ROW_TILES), TM_FFN * ROW_TILES), :], zsem)

        def start_tile(t, carry):
            tile(t).start()
            return carry

        def wait_tile(t, carry):
            tile(t).wait()
            return carry

        used = ends_ref[N_EXPERTS - 1] // TM_FFN
        lax.fori_loop(used, xs_ref.shape[0] // (TM_FFN * ROW_TILES), start_tile, 0)
        for e in range(N_EXPERTS):
            @pl.when(padded_ref[e] > 0)
            def _(e=e):
                window(e).wait()
        lax.fori_loop(used, xs_ref.shape[0] // (TM_FFN * ROW_TILES), wait_tile, 0)

    def issue(r, carry):
        for kk in range(TOP_K):
            _token_copy(h_ref, r, xs_ref, dest_ref[0, 0, r * TOP_K + kk], sem).start()
        return carry

    lax.fori_loop(0, tm, issue, 0)

    def drain(r, carry):
        for kk in range(TOP_K):
            _token_copy(h_ref, 0, xs_ref, 0, sem).wait()
        return carry

    lax.fori_loop(0, tm, drain, 0)


def _dispatch(ends, padded, dest3, h2, n_rows):
    n_tiles, _, per = dest3.shape
    tm = per // TOP_K
    grid_spec = pltpu.PrefetchScalarGridSpec(
        num_scalar_prefetch=2,
        grid=(n_tiles,),
        in_specs=[pl.BlockSpec((1, 1, per), lambda i, en, pa: (i, 0, 0), memory_space=pltpu.SMEM),
                  pl.BlockSpec((tm * ROW_TILES, LANES), lambda i, en, pa: (i, 0))],
        out_specs=pl.BlockSpec(memory_space=pl.ANY),
        scratch_shapes=[pltpu.VMEM((TM_FFN * ROW_TILES, LANES), F32),
                        pltpu.SemaphoreType.DMA(()), pltpu.SemaphoreType.DMA(())],
    )
    return pl.pallas_call(
        _dispatch_body,
        grid_spec=grid_spec,
        out_shape=jax.ShapeDtypeStruct((n_rows * ROW_TILES, LANES), F32),
        compiler_params=_cparams(("arbitrary",)),
        name="dispatch",
    )(ends, padded, dest3, h2)


def _experts_body(te_ref, na_ref, xs_ref, wgu_ref, bgu_ref, wdn_ref, bdn_ref, ys_ref, wgu_bf, wdn_bf):
    i = pl.program_id(0)
    tm = xs_ref.shape[0] // ROW_TILES
    active = i < na_ref[0]

    @pl.when(active)
    def _():
        prev = te_ref[jnp.maximum(i - 1, 0)]

        @pl.when((i == 0) | (te_ref[i] != prev))
        def _():
            wgu_bf[...] = wgu_ref[0, 0].astype(BF16)
            wdn_bf[...] = wdn_ref[0, 0].astype(BF16)

        x = _load_rows(xs_ref, tm).astype(BF16)
        gu = _dot(x, wgu_bf[...]) + bgu_ref[0, 0]
        gate = jnp.minimum(gu[:, :D_FF], SWIGLU_LIMIT)
        up = jnp.clip(gu[:, D_FF:], -SWIGLU_LIMIT, SWIGLU_LIMIT)
        act = (up + 1.0) * gate * jax.nn.sigmoid(SWIGLU_ALPHA * gate)
        y = _dot(act.astype(BF16), wdn_bf[...]) + bdn_ref[0, 0]
        _store_rows(ys_ref, y, tm)

    @pl.when(jnp.logical_not(active))
    def _():
        ys_ref[...] = jnp.zeros_like(ys_ref)


def _experts(layer, tile_expert, n_active, xs, w_gu, b_gu, w_dn, b_dn):
    n_rows = xs.shape[0] // ROW_TILES
    tm = TM_FFN
    b_gu4 = b_gu.reshape(b_gu.shape[0], N_EXPERTS, 1, 2 * D_FF)
    b_dn4 = b_dn.reshape(b_dn.shape[0], N_EXPERTS, 1, D_MODEL)
    grid_spec = pltpu.PrefetchScalarGridSpec(
        num_scalar_prefetch=2,
        grid=(n_rows // tm,),
        in_specs=[pl.BlockSpec((tm * ROW_TILES, LANES), lambda i, te, na: (jnp.minimum(i, na[0] - 1), 0)),
                  pl.BlockSpec((1, 1, D_MODEL, 2 * D_FF), lambda i, te, na: (layer, te[i], 0, 0)),
                  pl.BlockSpec((1, 1, 1, 2 * D_FF), lambda i, te, na: (layer, te[i], 0, 0)),
                  pl.BlockSpec((1, 1, D_FF, D_MODEL), lambda i, te, na: (layer, te[i], 0, 0)),
                  pl.BlockSpec((1, 1, 1, D_MODEL), lambda i, te, na: (layer, te[i], 0, 0))],
        out_specs=pl.BlockSpec((tm * ROW_TILES, LANES), lambda i, te, na: (i, 0)),
        scratch_shapes=[pltpu.VMEM((D_MODEL, 2 * D_FF), BF16), pltpu.VMEM((D_FF, D_MODEL), BF16)],
    )
    return pl.pallas_call(
        _experts_body,
        grid_spec=grid_spec,
        out_shape=jax.ShapeDtypeStruct(xs.shape, F32),
        compiler_params=_cparams(("arbitrary",)),
        name="experts",
    )(tile_expert, n_active, xs, w_gu, b_gu4, w_dn, b_dn4)


def _combine_body(dest_ref, wgt_ref, x_ref, m_ref, g_ref, ys_ref, xo_ref, buf, sem):
    tm = x_ref.shape[0]

    def issue(r, carry):
        for kk in range(TOP_K):
            _token_copy(ys_ref, dest_ref[0, 0, r * TOP_K + kk], buf.at[kk], r, sem).start()
        return carry

    lax.fori_loop(0, tm, issue, 0)

    def drain(r, carry):
        for kk in range(TOP_K):
            _token_copy(ys_ref, 0, buf.at[kk], 0, sem).wait()
        return carry

    lax.fori_loop(0, tm, drain, 0)
    w = wgt_ref[...]
    f = jnp.zeros((tm, D_MODEL), F32)
    for kk in range(TOP_K):
        f = f + w[:, kk:kk + 1] * _load_rows(buf.at[kk], tm)
    xo_ref[...] = x_ref[...] + m_ref[5, 0] * _rms(f, g_ref[...])


def _combine(dest3, wgt, x, mods, gpost, ys, tiles_per_seq):
    t = x.shape[0]
    tm = TM_COMB
    if mods.shape[2] == 1:
        mspec = pl.BlockSpec((6, 1, 1, D_MODEL), lambda i: (0, i // tiles_per_seq, 0, 0))
    else:
        mspec = pl.BlockSpec((6, 1, tm, D_MODEL), lambda i: (0, 0, i, 0))
    return pl.pallas_call(
        _combine_body,
        grid=(t // tm,),
        in_specs=[pl.BlockSpec((1, 1, tm * TOP_K), lambda i: (i, 0, 0), memory_space=pltpu.SMEM),
                  pl.BlockSpec((tm, LANES), lambda i: (i, 0)),
                  pl.BlockSpec((tm, D_MODEL), lambda i: (i, 0)),
                  mspec,
                  pl.BlockSpec((1, D_MODEL), lambda i: (0, 0)),
                  pl.BlockSpec(memory_space=pl.ANY)],
        out_specs=pl.BlockSpec((tm, D_MODEL), lambda i: (i, 0)),
        out_shape=jax.ShapeDtypeStruct((t, D_MODEL), F32),
        scratch_shapes=[pltpu.VMEM((TOP_K, tm * ROW_TILES, LANES), F32), pltpu.SemaphoreType.DMA(())],
        compiler_params=_cparams(("arbitrary",)),
        name="combine",
    )(dest3, wgt, x, mods, gpost, ys)


def _moe(layer, h2_all, t_prompt, x_p, x_s, mods_p, mods_s, gpost, w_r, b_r, tri_strict,
         w_gu, b_gu, w_dn, b_dn, tiles_per_seq_comb):
    t_all = h2_all.shape[0] // ROW_TILES
    idx_w, wgt_w, rank_w, cnt_w = _router(h2_all, w_r, b_r, tri_strict)
    idx = idx_w[:, :TOP_K]
    rank = rank_w[:, :TOP_K]
    cnt = cnt_w[0, :N_EXPERTS].astype(I32)
    padded = ((cnt + TM_FFN - 1) // TM_FFN) * TM_FFN
    ends = jnp.cumsum(padded)
    offs = ends - padded
    dest = jnp.sum(jnp.where(idx[..., None] == jnp.arange(N_EXPERTS, dtype=I32), offs, 0), axis=-1) + rank
    n_tiles = (t_all * TOP_K) // TM_FFN + N_EXPERTS
    tile_start = jnp.arange(n_tiles, dtype=I32) * TM_FFN
    tile_expert = jnp.minimum(jnp.sum((tile_start[:, None] >= ends[None, :]).astype(I32), axis=1), N_EXPERTS - 1)
    n_active = (ends[-1] // TM_FFN).astype(I32).reshape(1)

    dest_flat = dest.reshape(-1)
    xs = _dispatch(ends.astype(I32), padded, dest_flat.reshape(t_all // TM, 1, TM * TOP_K), h2_all, n_tiles * TM_FFN)
    ys = _experts(layer, tile_expert, n_active, xs, w_gu, b_gu, w_dn, b_dn)

    dest_p = dest_flat[:t_prompt * TOP_K].reshape(t_prompt // TM_COMB, 1, TM_COMB * TOP_K)
    dest_s = dest_flat[t_prompt * TOP_K:].reshape(-1, 1, TM_COMB * TOP_K)
    xo_p = _combine(dest_p, wgt_w[:t_prompt], x_p, mods_p, gpost, ys, tiles_per_seq_comb)
    xo_s = _combine(dest_s, wgt_w[t_prompt:], x_s, mods_s, gpost, ys, 1)
    return xo_p, xo_s


def _fox_prompt_body(qa_ref, ka_ref, vat_ref, g_ref, o_ref, m_s, acc_s):
    i = pl.program_id(2)
    tq = qa_ref.shape[0]
    m_s[...] = jnp.full_like(m_s, -jnp.inf)
    acc_s[...] = jnp.zeros_like(acc_s)

    def block(start, diagonal):
        if diagonal:
            causal = lax.broadcasted_iota(I32, (tq, tq), 0) <= lax.broadcasted_iota(I32, (tq, tq), 1)
        for a in range(2):
            cols = slice(a * LANES, (a + 1) * LANES)
            st = _dot_nt(ka_ref[pl.ds(start, tq), cols], qa_ref[:, cols])
            if diagonal:
                st = jnp.where(causal, st, -jnp.inf)
            m_old = m_s[a]
            m_new = jnp.maximum(m_old, jnp.max(st, axis=0, keepdims=True))
            p = jnp.exp((st - m_new).astype(BF16))
            acc_s[a] = jnp.exp(m_old - m_new) * acc_s[a] + _dot(vat_ref[0, cols, pl.ds(start, tq)], p)
            m_s[a] = m_new

    def body(j, carry):
        block(pl.multiple_of(j * tq, tq), False)
        return carry

    lax.fori_loop(0, i, body, 0)
    block(pl.multiple_of(i * tq, tq), True)
    o_t = jnp.concatenate([acc_s[a][:FOX_HD, :] / acc_s[a][FOX_HD:FOX_HD + 1, :] for a in range(2)], axis=0)
    o_ref[...] = o_t.T * jax.nn.sigmoid(g_ref[...])


def _fox_prompt_attn(qa, ka, vat, gate, n_seq, seq):
    t = gate.shape[0]
    nq = seq // TQ
    pairs = FOX_HEADS // 2
    return pl.pallas_call(
        _fox_prompt_body,
        grid=(n_seq, pairs, nq),
        in_specs=[pl.BlockSpec((TQ, 2 * LANES), lambda b, hp, i: (b * nq + i, hp)),
                  pl.BlockSpec((seq, 2 * LANES), lambda b, hp, i: (b, hp)),
                  pl.BlockSpec((1, 2 * LANES, seq), lambda b, hp, i: (b, hp, 0)),
                  pl.BlockSpec((TQ, LANES), lambda b, hp, i: (b * nq + i, hp))],
        out_specs=pl.BlockSpec((TQ, LANES), lambda b, hp, i: (b * nq + i, hp)),
        out_shape=jax.ShapeDtypeStruct((t, FOX_HEADS * FOX_HD), F32),
        scratch_shapes=[pltpu.VMEM((2, 1, TQ), F32), pltpu.VMEM((2, LANES, TQ), F32)],
        compiler_params=_cparams(("arbitrary", "arbitrary", "arbitrary")),
        name="fox_prompt_attn",
    )(qa, ka, vat, gate)


def _fox_sample_body(pt_ref, q_ref, kn_ref, vn_ref, g_ref, cncol_ref, cnrow_ref, *rest):
    npg = PAGES_PER_STEP
    k_refs = rest[:npg]
    v_refs = rest[npg:2 * npg]
    f_refs = rest[2 * npg:3 * npg]
    u_ref, e_ref, o_ref, qbd_s, m_s, l_s, acc_s, car_s = rest[3 * npg:]
    del pt_ref
    j = pl.program_id(1)
    nj = pl.num_programs(1)
    rows = FOX_HEADS * TOK_PAD
    wide = FOX_HEADS * FOX_HD
    own = ((lax.broadcasted_iota(I32, (rows, wide), 0) // TOK_PAD)
           == (lax.broadcasted_iota(I32, (rows, wide), 1) // FOX_HD))

    @pl.when(j == 0)
    def _():
        m_s[...] = jnp.full_like(m_s, -jnp.inf)
        l_s[...] = jnp.zeros_like(l_s)
        acc_s[...] = jnp.zeros_like(acc_s)
        car_s[...] = jnp.zeros_like(car_s)
        q = q_ref[...] * (FOX_HD ** -0.5)
        qbd_s[...] = jnp.where(own, jnp.concatenate([q] * FOX_HEADS, axis=0), 0.0).astype(BF16)

    qbd = qbd_s[...]
    cn = cncol_ref[0]

    def update(s, pv_of):
        m_old = m_s[...]
        m_new = jnp.maximum(m_old, jnp.max(s, axis=1, keepdims=True))
        alpha = jnp.exp(m_old - m_new)
        p = jnp.exp(s - m_new)
        l_s[...] = alpha * l_s[...] + jnp.sum(p, axis=1, keepdims=True)
        acc_s[...] = alpha * acc_s[...] + pv_of(p.astype(BF16))
        m_s[...] = m_new

    suffixes = []
    for r in range(npg):
        ft = f_refs[r][0]
        suffixes.append(_dot(ft, u_ref[...], HI) + car_s[...])
        car_s[...] = car_s[...] + jnp.sum(ft, axis=1, keepdims=True)
    bias = _dot(e_ref[...], jnp.concatenate(suffixes, axis=1), HI) + cn
    kt = jnp.concatenate([k_refs[r][0].astype(BF16) for r in range(npg)], axis=1)
    vt = jnp.concatenate([v_refs[r][0].astype(BF16) for r in range(npg)], axis=1)
    update(_dot(qbd, kt) + bias, lambda pb: _dot_nt(pb, vt))

    @pl.when(j == nj - 1)
    def _():
        zpad = jnp.zeros((PAGE_SIZE - TOK_PAD, wide), F32)
        kn = jnp.concatenate([kn_ref[...], zpad], axis=0).astype(BF16)
        vn = jnp.concatenate([vn_ref[...], zpad], axis=0).astype(BF16)
        trow = lax.broadcasted_iota(I32, (rows, PAGE_SIZE), 0) & (TOK_PAD - 1)
        tcol = lax.broadcasted_iota(I32, (rows, PAGE_SIZE), 1)
        s = jnp.where(tcol <= trow, _dot_nt(qbd, kn) + cn - cnrow_ref[0], -jnp.inf)
        update(s, lambda pb: _dot(pb, vn))
        o = jnp.where(own, acc_s[...] / l_s[...], 0.0)
        o8 = o[0:TOK_PAD, :]
        for h in range(1, FOX_HEADS):
            o8 = o8 + o[h * TOK_PAD:(h + 1) * TOK_PAD, :]
        o_ref[...] = o8 * jax.nn.sigmoid(g_ref[...])


def _fox_sample_attn(proj, cncol, cnrow, cache_k, cache_v, cache_ft, page_table, u_mat, e_mat):
    n_seq, n_pages = page_table.shape
    npg = PAGES_PER_STEP
    steps = n_pages // npg
    wide = FOX_HEADS * FOX_HD
    rows = FOX_HEADS * TOK_PAD

    def page(r):
        return lambda b, j, pt: pt[b * n_pages + (n_pages - 1 - (j * npg + r))]

    kv_specs = [pl.BlockSpec((1, wide, PAGE_SIZE), (lambda b, j, pt, r=r: (page(r)(b, j, pt), 0, 0)))
                for r in range(npg)]
    f_specs = [pl.BlockSpec((1, FOX_HEADS, PAGE_SIZE), (lambda b, j, pt, r=r: (page(r)(b, j, pt), 0, 0)))
               for r in range(npg)]
    grid_spec = pltpu.PrefetchScalarGridSpec(
        num_scalar_prefetch=1,
        grid=(n_seq, steps),
        in_specs=[pl.BlockSpec((TOK_PAD, wide), lambda b, j, pt: (b, 0)),
                  pl.BlockSpec((TOK_PAD, wide), lambda b, j, pt: (b, 1)),
                  pl.BlockSpec((TOK_PAD, wide), lambda b, j, pt: (b, 2)),
                  pl.BlockSpec((TOK_PAD, wide), lambda b, j, pt: (b, 3)),
                  pl.BlockSpec((1, rows, 1), lambda b, j, pt: (b, 0, 0)),
                  pl.BlockSpec((1, rows, PAGE_SIZE), lambda b, j, pt: (b, 0, 0))]
                 + kv_specs + kv_specs + f_specs
                 + [pl.BlockSpec((PAGE_SIZE, PAGE_SIZE), lambda b, j, pt: (0, 0)),
                    pl.BlockSpec((rows, FOX_HEADS), lambda b, j, pt: (0, 0))],
        out_specs=pl.BlockSpec((TOK_PAD, wide), lambda b, j, pt: (b, 0)),
        scratch_shapes=[pltpu.VMEM((rows, wide), BF16), pltpu.VMEM((rows, 1), F32), pltpu.VMEM((rows, 1), F32),
                        pltpu.VMEM((rows, wide), F32), pltpu.VMEM((FOX_HEADS, 1), F32)],
    )
    return pl.pallas_call(
        _fox_sample_body,
        grid_spec=grid_spec,
        out_shape=jax.ShapeDtypeStruct((n_seq * TOK_PAD, wide), F32),
        compiler_params=_cparams(("arbitrary", "arbitrary")),
        name="fox_sample_attn",
    )(page_table.reshape(-1), proj, proj, proj, proj, cncol, cnrow,
      *([cache_k] * npg), *([cache_v] * npg), *([cache_ft] * npg), u_mat, e_mat)


def _rope_tables(pos):
    half = RET_DK // 2
    inv = ROPE_BASE ** (-jnp.arange(half, dtype=F32) / half)
    ang = pos.astype(F32)[:, None] * inv[None, :]
    return jnp.cos(ang), jnp.sin(ang)


def _mods_prompt(m):
    b = m.shape[0]
    return m.reshape(b, 6, D_MODEL).transpose(1, 0, 2)[:, :, None, :]


def _mods_sample(m):
    b = m.shape[0]
    mm = jnp.repeat(m.reshape(b, 6, D_MODEL), TOK_PAD, axis=0)
    return mm.transpose(1, 0, 2)[:, None, :, :]


def kernel(x_prompt, x_sample, c_prompt, c_sample, state_ret, cache_fox_k, cache_fox_v, cache_fox_logf, page_table, w_ada, b_ada, norm_pre_mix, norm_post_mix, norm_pre_ffn, norm_post_ffn, ret_w_in, ret_gn_gain, ret_w_out, fox_w_in, fox_b_f, fox_w_out, moe_w_router, moe_b_router, moe_w_gate_up, moe_b_gate_up, moe_w_down, moe_b_down):
    n_seq, seq, d = x_prompt.shape
    n_dec, dec_seq, _ = x_sample.shape
    t_p = n_seq * seq
    t_s = n_dec * TOK_PAD
    tps = seq // TM

    xp = x_prompt.reshape(t_p, d)
    xs = jnp.pad(x_sample, ((0, 0), (0, TOK_PAD - dec_seq), (0, 0))).reshape(t_s, d)

    m_all = _adaln(jnp.concatenate([c_prompt, c_sample], axis=0), w_ada, b_ada)

    ar = jnp.arange(TM)
    tri_incl = (ar[None, :] <= ar[:, None]).astype(F32)
    tri_strict = (ar[None, :] < ar[:, None]).astype(BF16)
    tri_group = ((ar[None, :] <= ar[:, None]) & (ar[None, :] // TOK_PAD == ar[:, None] // TOK_PAD)).astype(F32)

    mods_p = _mods_prompt(m_all[0, :n_seq])
    mods_s = _mods_sample(m_all[0, n_seq:])
    w_in = ret_w_in[0].astype(BF16)
    cos_p, sin_p = _rope_tables(jnp.arange(seq))
    pos_s = jnp.tile(PAST_LEN + jnp.arange(TOK_PAD), n_dec)
    cos_s, sin_s = _rope_tables(pos_s)
    g_pre = norm_pre_mix[0:1]
    proj_p = _inproj_ret(xp, mods_p, g_pre, w_in, cos_p, sin_p, tps)
    proj_s = _inproj_ret(xs, mods_s, g_pre, ret_w_in[0], cos_s, sin_s, 1)

    n_chunks = seq // RET_CHUNK
    din, dq, dk, dc = _ret_decay_tables(RET_CHUNK, RET_CHUNK, RET_CHUNK)
    s0p = jnp.zeros((n_seq, RET_HEADS, RET_DK, RET_DV), F32)
    u_p, st_p = _retention(proj_p, s0p, din, dq, dk, dc, ret_gn_gain, n_seq, n_chunks, RET_CHUNK, False)
    din, dq, dk, dc = _ret_decay_tables(dec_seq, TOK_PAD, RET_CHUNK)
    u_s, st_s = _retention(proj_s, state_ret[0], din, dq, dk, dc, ret_gn_gain, n_dec, 1, TOK_PAD, True)

    w_out = ret_w_out[0].astype(BF16)
    xp, h2p = _outproj(u_p, w_out, xp, mods_p, norm_post_mix[0:1], norm_pre_ffn[0:1], tps)
    xs, h2s = _outproj(u_s, ret_w_out[0], xs, mods_s, norm_post_mix[0:1], norm_pre_ffn[0:1], 1)

    def router_params(i):
        w_r = jnp.pad(moe_w_router[i], ((0, 0), (0, LANES - N_EXPERTS)))
        b_r = jnp.pad(moe_b_router[i], (0, LANES - N_EXPERTS), constant_values=-1e30)[None, :]
        return w_r, b_r

    w_r, b_r = router_params(0)
    xp, xs = _moe(0, jnp.concatenate([h2p, h2s], axis=0), t_p, xp, xs, mods_p, mods_s, norm_post_ffn[0:1],
                  w_r, b_r, tri_strict, moe_w_gate_up, moe_b_gate_up, moe_w_down, moe_b_down, seq // TM_COMB)

    mods_p = _mods_prompt(m_all[1, :n_seq])
    mods_s = _mods_sample(m_all[1, n_seq:])
    fw = FOX_HEADS * FOX_HD
    w_in = fox_w_in[0, :, :4 * fw].astype(BF16)
    w_fl = jnp.pad(fox_w_in[0, :, 4 * fw:], ((0, 0), (0, LANES - FOX_HEADS)))
    b_fl = jnp.pad(fox_b_f[0], (0, LANES - FOX_HEADS))[None, :]
    g_pre = norm_pre_mix[1:2]
    wkt = fox_w_in[0, :, fw:2 * fw].T.astype(BF16)
    wvt = fox_w_in[0, :, 2 * fw:3 * fw].T.astype(BF16)
    gate_p, lf_p, qa, ka, vat, kt_p, vt_p = _inproj_fox(xp, mods_p, g_pre, w_in, w_fl, b_fl, tri_incl, tps, wkt, wvt)
    fproj_s, lf_s, cum_s = _inproj_fox(xs, mods_s, g_pre, w_in, w_fl, b_fl, tri_group, 1)
    o_p = _fox_prompt_attn(qa, ka, vat, gate_p, n_seq, seq)

    cn = cum_s[:, :FOX_HEADS].reshape(n_dec, TOK_PAD, FOX_HEADS).transpose(0, 2, 1)
    cncol = cn.reshape(n_dec, FOX_HEADS * TOK_PAD, 1)
    cnrow = jnp.repeat(cn, TOK_PAD, axis=1)
    cnrow = jnp.pad(cnrow, ((0, 0), (0, 0), (0, PAGE_SIZE - TOK_PAD)))
    pr = jnp.arange(PAGE_SIZE)
    u_mat = (pr[:, None] > pr[None, :]).astype(F32)
    e_mat = (jnp.arange(FOX_HEADS * TOK_PAD)[:, None] // TOK_PAD == jnp.arange(FOX_HEADS)[None, :]).astype(F32)
    n_pool = cache_fox_k.shape[1]
    cache_kt = cache_fox_k[0].transpose(0, 2, 3, 1).reshape(n_pool, fw, PAGE_SIZE)
    cache_vt = cache_fox_v[0].transpose(0, 2, 3, 1).reshape(n_pool, fw, PAGE_SIZE)
    cache_ft = cache_fox_logf[0].transpose(0, 2, 1)
    o_s = _fox_sample_attn(fproj_s, cncol, cnrow, cache_kt, cache_vt, cache_ft, page_table, u_mat, e_mat)

    w_out = fox_w_out[0].astype(BF16)
    xp, h2p = _outproj(o_p, w_out, xp, mods_p, norm_post_mix[1:2], norm_pre_ffn[1:2], tps)
    xs, h2s = _outproj(o_s, w_out, xs, mods_s, norm_post_mix[1:2], norm_pre_ffn[1:2], 1)
    w_r, b_r = router_params(1)
    xp, xs = _moe(1, jnp.concatenate([h2p, h2s], axis=0), t_p, xp, xs, mods_p, mods_s, norm_post_ffn[1:2],
                  w_r, b_r, tri_strict, moe_w_gate_up, moe_b_gate_up, moe_w_down, moe_b_down, seq // TM_COMB)

    y_prompt = xp.reshape(n_seq, seq, d)
    y_sample = xs.reshape(n_dec, TOK_PAD, d)[:, :dec_seq]
    k_prompt = kt_p.reshape(n_seq, FOX_HEADS, FOX_HD, seq).transpose(0, 3, 1, 2)[None]
    v_prompt = vt_p.reshape(n_seq, FOX_HEADS, FOX_HD, seq).transpose(0, 3, 1, 2)[None]
    logf_prompt = lf_p[:, :FOX_HEADS].reshape(1, n_seq, seq, FOX_HEADS)
    fs = fproj_s.reshape(n_dec, TOK_PAD, 4 * fw)[:, :dec_seq]
    kv_shape_s = (1, n_dec, dec_seq, FOX_HEADS, FOX_HD)
    k_sample = fs[..., fw:2 * fw].reshape(kv_shape_s)
    v_sample = fs[..., 2 * fw:3 * fw].reshape(kv_shape_s)
    logf_sample = lf_s.reshape(n_dec, TOK_PAD, LANES)[:, :dec_seq, :FOX_HEADS][None]
    return (y_prompt, y_sample, st_p[None], st_s[None], k_prompt, v_prompt, logf_prompt,
            k_sample, v_sample, logf_sample)
```

```python
import functools

import jax
import jax.numpy as jnp
from jax import lax
from jax.experimental import pallas as pl
from jax.experimental.pallas import tpu as pltpu

F32 = jnp.float32
BF16 = jnp.bfloat16
I32 = jnp.int32
HI = lax.Precision.HIGHEST

D_MODEL = 1024
PAST_LEN = 8192
PAGE_SIZE = 128
RET_HEADS = 4
RET_DK = D_MODEL // RET_HEADS
RET_DV = 2 * RET_DK
RET_CHUNK = 128
ROPE_BASE = 10000.0
FOX_HEADS = 16
FOX_HD = D_MODEL // FOX_HEADS
N_EXPERTS = 32
TOP_K = 4
D_FF = D_MODEL
SWIGLU_LIMIT = 7.0
SWIGLU_ALPHA = 1.702
NORM_EPS = 1e-6

LANES = 128
SUBLANES = 8
ROW_TILES = D_MODEL // LANES
TOK_PAD = SUBLANES
TM = 256
TM_FFN = 512
TM_COMB = 128
TQ = 512
PAGES_PER_STEP = 16
VMEM_LIMIT = 56 * 1024 * 1024


def _cparams(sem, vmem=VMEM_LIMIT):
    return pltpu.CompilerParams(dimension_semantics=sem, vmem_limit_bytes=vmem)


def _rms(x, gain):
    return x * lax.rsqrt(jnp.mean(x * x, axis=-1, keepdims=True) + NORM_EPS) * gain


def _dot(a, b, precision=None):
    return jnp.dot(a, b, precision=precision, preferred_element_type=F32)


def _dot_nt(a, b):
    return lax.dot_general(a, b, (((1,), (1,)), ((), ())), preferred_element_type=F32)


def _mm(a, b, precise, dims=None):
    if precise:
        a, b, prec = a.astype(F32), b.astype(F32), HI
    else:
        a, b, prec = a.astype(BF16), b.astype(BF16), None
    if dims is None:
        return jnp.dot(a, b, precision=prec, preferred_element_type=F32)
    return lax.dot_general(a, b, (dims, ((), ())), precision=prec, preferred_element_type=F32)


_NT = ((1,), (1,))
_TN = ((0,), (0,))


def _load_rows(ref, rows):
    return jnp.concatenate([ref[pl.ds(s, rows, stride=ROW_TILES), :] for s in range(ROW_TILES)], axis=1)


def _store_rows(ref, val, rows):
    for s in range(ROW_TILES):
        ref[pl.ds(s, rows, stride=ROW_TILES), :] = val[:, s * LANES:(s + 1) * LANES]


def _ada_body(c_ref, w_ref, b_ref, o_ref):
    c = c_ref[...]
    a = c * jax.nn.sigmoid(c)
    o_ref[0] = _dot(a, w_ref[0], HI) + b_ref[0]


def _adaln(c_all, w_ada, b_ada):
    depth, d, n = w_ada.shape
    nb = c_all.shape[0]
    tn = 768
    return pl.pallas_call(
        _ada_body,
        grid=(depth, n // tn),
        in_specs=[pl.BlockSpec((nb, d), lambda l, j: (0, 0)),
                  pl.BlockSpec((1, d, tn), lambda l, j: (l, 0, j)),
                  pl.BlockSpec((1, 1, tn), lambda l, j: (l, 0, j))],
        out_specs=pl.BlockSpec((1, nb, tn), lambda l, j: (l, 0, j)),
        out_shape=jax.ShapeDtypeStruct((depth, nb, n), F32),
        compiler_params=_cparams(("arbitrary", "arbitrary")),
        name="adaln",
    )(c_all, w_ada, b_ada.reshape(depth, 1, n))


def _mod_spec(mods, tiles_per_seq):
    r = mods.shape[2]
    if r == 1:
        return pl.BlockSpec((6, 1, 1, D_MODEL), lambda i: (0, i // tiles_per_seq, 0, 0))
    return pl.BlockSpec((6, 1, r, D_MODEL), lambda i: (0, 0, 0, 0))


def _inproj_ret_body(x_ref, m_ref, g_ref, w_ref, cos_ref, sin_ref, o_ref, *, precise):
    h = _rms(x_ref[...], g_ref[...]) * (1.0 + m_ref[1, 0]) + m_ref[0, 0]
    hb = h if precise else h.astype(BF16)
    cos = cos_ref[...]
    sin = sin_ref[...]
    half = RET_DK // 2
    for c in range(2 * RET_HEADS):
        lo = c * RET_DK
        r = _mm(hb, w_ref[:, lo:lo + RET_DK], precise)
        x1 = r[:, :half]
        x2 = r[:, half:]
        sc = RET_DK ** -0.5 if c < RET_HEADS else 1.0
        o_ref[:, lo:lo + half] = (x1 * cos - x2 * sin) * sc
        o_ref[:, lo + half:lo + RET_DK] = (x1 * sin + x2 * cos) * sc
    base = 2 * RET_HEADS * RET_DK
    for c in range(2 * RET_HEADS):
        lo = base + c * RET_DV
        o_ref[:, lo:lo + RET_DV] = _mm(hb, w_ref[:, lo:lo + RET_DV], precise)


def _inproj_ret(x, mods, gain, w, cos, sin, tiles_per_seq):
    t = x.shape[0]
    n = w.shape[1]
    tm = min(TM, t)
    pos_tiles = cos.shape[0] // tm
    return pl.pallas_call(
        functools.partial(_inproj_ret_body, precise=w.dtype == F32),
        grid=(t // tm,),
        in_specs=[pl.BlockSpec((tm, D_MODEL), lambda i: (i, 0)),
                  _mod_spec(mods, tiles_per_seq),
                  pl.BlockSpec((1, D_MODEL), lambda i: (0, 0)),
                  pl.BlockSpec((D_MODEL, n), lambda i: (0, 0), pipeline_mode=pl.Buffered(1)),
                  pl.BlockSpec((tm, RET_DK // 2), lambda i: (i % pos_tiles, 0)),
                  pl.BlockSpec((tm, RET_DK // 2), lambda i: (i % pos_tiles, 0))],
        out_specs=pl.BlockSpec((tm, n), lambda i: (i, 0)),
        out_shape=jax.ShapeDtypeStruct((t, n), F32),
        compiler_params=_cparams(("arbitrary",)),
        name="inproj_ret",
    )(x, mods, gain, w, cos, sin)


def _split3(c):
    hi = c.astype(BF16).astype(F32)
    r = c - hi
    mid = r.astype(BF16).astype(F32)
    return hi, mid, r - mid


def _emit_attention_operands(o_ref, cs, qa_ref, ka_ref):
    tm = cs.shape[0]
    fw = FOX_HEADS * FOX_HD
    lane = lax.broadcasted_iota(I32, (tm, LANES), 1)
    low = lane < FOX_HD
    ones_q = jnp.where((lane >= FOX_HD + 3) & (lane < FOX_HD + 6), 1.0, 0.0)
    ones_k = jnp.where((lane >= FOX_HD) & (lane < FOX_HD + 3), 1.0, 0.0)
    for p in range(FOX_HEADS // 2):
        pairs = [o_ref[:, s * fw + p * LANES:s * fw + (p + 1) * LANES] for s in range(2)]
        pairs[0] = pairs[0] * (FOX_HD ** -0.5)
        for a in range(2):
            hh = 2 * p + a
            qh, kh = pairs if a == 0 else [pltpu.roll(x, FOX_HD, 1) for x in pairs]
            c = jnp.sum(jnp.where(lane == hh, cs, 0.0), axis=1, keepdims=True)
            hi, mid, lo = _split3(c)
            fq = jnp.where(lane == FOX_HD, hi, jnp.where(lane == FOX_HD + 1, mid,
                                                         jnp.where(lane == FOX_HD + 2, lo, ones_q)))
            fk = jnp.where(lane == FOX_HD + 3, -hi, jnp.where(lane == FOX_HD + 4, -mid,
                                                              jnp.where(lane == FOX_HD + 5, -lo, ones_k)))
            cols = slice(hh * LANES, (hh + 1) * LANES)
            qa_ref[:, cols] = jnp.where(low, qh, fq).astype(BF16)
            ka_ref[:, cols] = jnp.where(low, kh, fk).astype(BF16)


def _inproj_fox_body(x_ref, m_ref, g_ref, w_ref, wfl_ref, bf_ref, tri_ref, *rest, tiles_per_seq, prompt):
    if prompt:
        wkt_ref, wvt_ref, gate_ref, lf_ref, qa_ref, ka_ref, vat_ref, kt_ref, vt_ref, stage, carry = rest
    else:
        stage, lf_ref, cum_ref = rest
    i = pl.program_id(0)
    tm = x_ref.shape[0]
    fw = FOX_HEADS * FOX_HD
    h = _rms(x_ref[...], g_ref[...]) * (1.0 + m_ref[1, 0]) + m_ref[0, 0]
    hb = h.astype(BF16)
    cw = 512
    for c in range(4 * fw // cw):
        if prompt and 2 * fw <= c * cw < 3 * fw:
            continue
        r = _dot(hb, w_ref[:, c * cw:(c + 1) * cw])
        if prompt and c * cw >= 3 * fw:
            gate_ref[:, c * cw - 3 * fw:(c + 1) * cw - 3 * fw] = r
        else:
            stage[:, c * cw:(c + 1) * cw] = r
    fl = _dot(h, wfl_ref[...], HI) + bf_ref[...]
    lf = jnp.minimum(fl, 0.0) - jnp.log1p(jnp.exp(-jnp.abs(fl)))
    lf_ref[...] = lf
    cs = _dot(tri_ref[...], lf, HI)
    if not prompt:
        cum_ref[...] = cs
        return

    @pl.when(i % tiles_per_seq == 0)
    def _():
        carry[...] = jnp.zeros_like(carry)
    cs = cs + carry[...]
    carry[...] = cs[tm - 1:tm, :]
    _emit_attention_operands(stage, cs, qa_ref, ka_ref)
    kt_ref[0] = _dot_nt(wkt_ref[...], hb)
    vt = _dot_nt(wvt_ref[...], hb)
    vt_ref[0] = vt
    ones_rows = jnp.where(lax.broadcasted_iota(I32, (LANES - FOX_HD, tm), 0) == 0, 1.0, 0.0)
    for hh in range(FOX_HEADS):
        vat_ref[0, hh * LANES:(hh + 1) * LANES, :] = jnp.concatenate(
            [vt[hh * FOX_HD:(hh + 1) * FOX_HD, :], ones_rows], axis=0).astype(BF16)


def _inproj_fox(x, mods, gain, w_bf, w_fl, b_fl, tri, tiles_per_seq, wkt=None, wvt=None):
    t = x.shape[0]
    n = w_bf.shape[1]
    tm = min(TM, t)
    prompt = wkt is not None
    body = functools.partial(_inproj_fox_body, tiles_per_seq=tiles_per_seq, prompt=prompt)
    fw = FOX_HEADS * FOX_HD
    wide = FOX_HEADS * LANES
    row = lambda width: pl.BlockSpec((tm, width), lambda i: (i, 0))
    in_specs = [row(D_MODEL),
                _mod_spec(mods, tiles_per_seq),
                pl.BlockSpec((1, D_MODEL), lambda i: (0, 0)),
                pl.BlockSpec((D_MODEL, n), lambda i: (0, 0), pipeline_mode=pl.Buffered(1)),
                pl.BlockSpec((D_MODEL, LANES), lambda i: (0, 0)),
                pl.BlockSpec((1, LANES), lambda i: (0, 0)),
                pl.BlockSpec((tm, tm), lambda i: (0, 0))]
    args = [x, mods, gain, w_bf, w_fl, b_fl, tri]
    if prompt:
        n_seq = t // (tiles_per_seq * tm)
        tspec = pl.BlockSpec((1, fw, tm), lambda i: (i // tiles_per_seq, 0, i % tiles_per_seq))
        vspec = pl.BlockSpec((1, wide, tm), lambda i: (i // tiles_per_seq, 0, i % tiles_per_seq))
        in_specs += [pl.BlockSpec((fw, D_MODEL), lambda i: (0, 0), pipeline_mode=pl.Buffered(1))] * 2
        args += [wkt, wvt]
        out_specs = [row(fw), row(LANES), row(wide), row(wide), vspec, tspec, tspec]
        out_shape = ([jax.ShapeDtypeStruct((t, fw), F32), jax.ShapeDtypeStruct((t, LANES), F32)]
                     + [jax.ShapeDtypeStruct((t, wide), BF16)] * 2
                     + [jax.ShapeDtypeStruct((n_seq, wide, tiles_per_seq * tm), BF16)]
                     + [jax.ShapeDtypeStruct((n_seq, fw, tiles_per_seq * tm), F32)] * 2)
        scratch = [pltpu.VMEM((tm, 2 * fw), F32), pltpu.VMEM((1, LANES), F32)]
    else:
        out_specs = [row(n), row(LANES), row(LANES)]
        out_shape = [jax.ShapeDtypeStruct((t, n), F32), jax.ShapeDtypeStruct((t, LANES), F32),
                     jax.ShapeDtypeStruct((t, LANES), F32)]
        scratch = []
    return pl.pallas_call(
        body,
        grid=(t // tm,),
        in_specs=in_specs,
        out_specs=out_specs,
        out_shape=out_shape,
        scratch_shapes=scratch,
        compiler_params=_cparams(("arbitrary",)),
        name="inproj_fox",
    )(*args)


def _retention_body(q_ref, k_ref, v_ref, g_ref, s0_ref, din_ref, dq_ref, dk_ref, dc_ref, gn_ref,
                    u_ref, so_ref, state, *, n_chunks, kv_rows, precise):
    c = pl.program_id(1)

    @pl.when(c == 0)
    def _():
        state[...] = s0_ref[0]

    for hh in range(RET_HEADS):
        qk = slice(hh * RET_DK, (hh + 1) * RET_DK)
        vg = slice(hh * RET_DV, (hh + 1) * RET_DV)
        q = q_ref[:, qk]
        k = k_ref[:, qk]
        v = v_ref[:, vg]
        rows = k.shape[0]
        if rows < kv_rows:
            k = jnp.concatenate([k, jnp.zeros((kv_rows - rows, RET_DK), F32)], axis=0)
            v = jnp.concatenate([v, jnp.zeros((kv_rows - rows, RET_DV), F32)], axis=0)
        kd = k * dk_ref[hh]
        s_old = state[hh]
        inner = _mm(q, k, precise, _NT) * din_ref[hh]
        o = _mm(inner, v, precise) + _mm(q, s_old, precise) * dq_ref[hh]
        s_new = s_old * dc_ref[hh] + _mm(kd, v, False, _TN)
        state[hh] = s_new
        g = g_ref[:, vg]
        u_ref[:, vg] = _rms(o, gn_ref[:, vg]) * (g * jax.nn.sigmoid(g))

        @pl.when(c == n_chunks - 1)
        def _(hh=hh, s_new=s_new):
            so_ref[0, hh] = s_new


def _retention(proj, s0, din, dq, dk, dc, gn_gain, n_seq, n_chunks, q_rows, precise):
    t = proj.shape[0]
    kv_rows = din.shape[2]
    h = RET_HEADS
    hk, hv = h * RET_DK, h * RET_DV
    body = functools.partial(_retention_body, n_chunks=n_chunks, kv_rows=kv_rows, precise=precise)
    row = lambda b, c: b * n_chunks + c
    whole = lambda shape: pl.BlockSpec(shape, lambda b, c: (0,) * len(shape))
    return pl.pallas_call(
        body,
        grid=(n_seq, n_chunks),
        in_specs=[pl.BlockSpec((q_rows, hk), lambda b, c: (row(b, c), 0)),
                  pl.BlockSpec((q_rows, hk), lambda b, c: (row(b, c), 1)),
                  pl.BlockSpec((q_rows, hv), lambda b, c: (row(b, c), 2 * hk // hv)),
                  pl.BlockSpec((q_rows, hv), lambda b, c: (row(b, c), 2 * hk // hv + 1)),
                  pl.BlockSpec((1, h, RET_DK, RET_DV), lambda b, c: (b, 0, 0, 0)),
                  whole((h, q_rows, kv_rows)), whole((h, q_rows, 1)), whole((h, kv_rows, 1)), whole((h, 1, 1)),
                  whole((1, hv))],
        out_specs=[pl.BlockSpec((q_rows, hv), lambda b, c: (row(b, c), 0)),
                   pl.BlockSpec((1, h, RET_DK, RET_DV), lambda b, c: (b, 0, 0, 0))],
        out_shape=[jax.ShapeDtypeStruct((t, hv), F32),
                   jax.ShapeDtypeStruct((n_seq, h, RET_DK, RET_DV), F32)],
        scratch_shapes=[pltpu.VMEM((h, RET_DK, RET_DV), F32)],
        compiler_params=_cparams(("arbitrary", "arbitrary")),
        name="retention",
    )(proj, proj, proj, proj, s0, din, dq, dk, dc, gn_gain)


def _ret_decay_tables(c, q_rows, kv_rows):
    lg = jnp.log(1.0 - 2.0 ** (-5.0 - jnp.arange(RET_HEADS, dtype=F32)))
    idx = jnp.arange(c, dtype=F32)
    rel = idx[:, None] - idx[None, :]
    din = jnp.where(rel[None] >= 0, jnp.exp(lg[:, None, None] * jnp.maximum(rel, 0.0)[None]), 0.0)
    dq = jnp.exp(lg[:, None] * (idx[None, :] + 1.0))
    dk = jnp.exp(lg[:, None] * (c - 1.0 - idx[None, :]))
    dc = jnp.exp(lg * c)
    din = jnp.pad(din, ((0, 0), (0, q_rows - c), (0, kv_rows - c)))
    dq = jnp.pad(dq, ((0, 0), (0, q_rows - c)))[..., None]
    dk = jnp.pad(dk, ((0, 0), (0, kv_rows - c)))[..., None]
    return din, dq, dk, dc[:, None, None]


def _outproj_body(u_ref, w_ref, x_ref, m_ref, gpost_ref, gpre_ref, xn_ref, h2_ref):
    tm = x_ref.shape[0]
    y = _mm(u_ref[...], w_ref[...], w_ref.dtype == F32)
    xn = x_ref[...] + m_ref[2, 0] * _rms(y, gpost_ref[...])
    xn_ref[...] = xn
    h2 = _rms(xn, gpre_ref[...]) * (1.0 + m_ref[4, 0]) + m_ref[3, 0]
    _store_rows(h2_ref, h2, tm)


def _outproj(u, w, x, mods, gpost, gpre, tiles_per_seq):
    t = x.shape[0]
    din = u.shape[1]
    tm = min(TM, t)
    return pl.pallas_call(
        _outproj_body,
        grid=(t // tm,),
        in_specs=[pl.BlockSpec((tm, din), lambda i: (i, 0)),
                  pl.BlockSpec((din, D_MODEL), lambda i: (0, 0), pipeline_mode=pl.Buffered(1)),
                  pl.BlockSpec((tm, D_MODEL), lambda i: (i, 0)),
                  _mod_spec(mods, tiles_per_seq),
                  pl.BlockSpec((1, D_MODEL), lambda i: (0, 0)),
                  pl.BlockSpec((1, D_MODEL), lambda i: (0, 0))],
        out_specs=[pl.BlockSpec((tm, D_MODEL), lambda i: (i, 0)),
                   pl.BlockSpec((tm * ROW_TILES, LANES), lambda i: (i, 0))],
        out_shape=[jax.ShapeDtypeStruct((t, D_MODEL), F32),
                   jax.ShapeDtypeStruct((t * ROW_TILES, LANES), F32)],
        compiler_params=_cparams(("arbitrary",)),
        name="outproj",
    )(u, w, x, mods, gpost, gpre)


def _router_body(h_ref, w_ref, b_ref, tri_ref, idx_ref, wgt_ref, rank_ref, cnt_ref, carry):
    i = pl.program_id(0)
    tm = idx_ref.shape[0]

    @pl.when(i == 0)
    def _():
        carry[...] = jnp.zeros_like(carry)

    h = _load_rows(h_ref, tm)
    logits = _dot(h, w_ref[...], HI) + b_ref[...]
    lane = lax.broadcasted_iota(I32, (tm, LANES), 1)
    lane_f = lane.astype(F32)
    work = logits
    vals, ids, hots = [], [], []
    for _ in range(TOP_K):
        mx = jnp.max(work, axis=1, keepdims=True)
        ik = jnp.min(jnp.where(work == mx, lane_f, float(LANES)), axis=1, keepdims=True)
        hot = lane_f == ik
        vals.append(mx)
        ids.append(ik.astype(I32))
        hots.append(hot)
        work = jnp.where(hot, -jnp.inf, work)
    ex = [jnp.exp(v - vals[0]) for v in vals]
    den = ex[0] + ex[1] + ex[2] + ex[3]
    chosen = hots[0] | hots[1] | hots[2] | hots[3]
    before = _dot(tri_ref[...], chosen.astype(BF16)) + carry[...]
    idx_o = jnp.zeros((tm, LANES), I32)
    wgt_o = jnp.zeros((tm, LANES), F32)
    rank_o = jnp.zeros((tm, LANES), I32)
    for kk in range(TOP_K):
        rk = jnp.sum(jnp.where(hots[kk], before, 0.0), axis=1, keepdims=True).astype(I32)
        idx_o = jnp.where(lane == kk, ids[kk], idx_o)
        wgt_o = jnp.where(lane == kk, ex[kk] / den, wgt_o)
        rank_o = jnp.where(lane == kk, rk, rank_o)
    idx_ref[...] = idx_o
    wgt_ref[...] = wgt_o
    rank_ref[...] = rank_o
    carry[...] = carry[...] + jnp.sum(chosen.astype(F32), axis=0, keepdims=True)
    cnt_ref[...] = carry[...]


def _router(h2, w_r, b_r, tri):
    t = h2.shape[0] // ROW_TILES
    tm = TM
    return pl.pallas_call(
        _router_body,
        grid=(t // tm,),
        in_specs=[pl.BlockSpec((tm * ROW_TILES, LANES), lambda i: (i, 0)),
                  pl.BlockSpec((D_MODEL, LANES), lambda i: (0, 0)),
                  pl.BlockSpec((1, LANES), lambda i: (0, 0)),
                  pl.BlockSpec((tm, tm), lambda i: (0, 0))],
        out_specs=[pl.BlockSpec((tm, LANES), lambda i: (i, 0)),
                   pl.BlockSpec((tm, LANES), lambda i: (i, 0)),
                   pl.BlockSpec((tm, LANES), lambda i: (i, 0)),
                   pl.BlockSpec((1, LANES), lambda i: (0, 0))],
        out_shape=[jax.ShapeDtypeStruct((t, LANES), I32),
                   jax.ShapeDtypeStruct((t, LANES), F32),
                   jax.ShapeDtypeStruct((t, LANES), I32),
                   jax.ShapeDtypeStruct((1, LANES), F32)],
        scratch_shapes=[pltpu.VMEM((1, LANES), F32)],
        compiler_params=_cparams(("arbitrary",)),
        name="router",
    )(h2, w_r, b_r, tri)


def _token_copy(src, src_row, dst, dst_row, sem):
    return pltpu.make_async_copy(src.at[pl.ds(pl.multiple_of(src_row * ROW_TILES, ROW_TILES), ROW_TILES), :],
                                 dst.at[pl.ds(pl.multiple_of(dst_row * ROW_TILES, ROW_TILES), ROW_TILES), :], sem)


def _dispatch_body(ends_ref, padded_ref, dest_ref, h_ref, xs_ref, zbuf, sem, zsem):
    tm = h_ref.shape[0] // ROW_TILES

    @pl.when(pl.program_id(0) == 0)
    def _():
        zbuf[...] = jnp.zeros_like(zbuf)

        def window(e):
            first = pl.multiple_of((ends_ref[e] - TM_FFN) * ROW_TILES, ROW_TILES)
            return pltpu.make_async_copy(zbuf, xs_ref.at[pl.ds(first, TM_FFN * ROW_TILES), :], zsem)

        for e in range(N_EXPERTS):
            @pl.when(padded_ref[e] > 0)
            def _(e=e):
                window(e).start()
        def tile(t):
            return pltpu.make_async_copy(
                zbuf, xs_ref.at[pl.ds(pl.multiple_of(t * TM_FFN * ROW_TILES, ROW_TILES), TM_FFN * ROW_TILES), :], zsem)

        def start_tile(t, carry):
            tile(t).start()
            return carry

        def wait_tile(t, carry):
            tile(t).wait()
            return carry

        used = ends_ref[N_EXPERTS - 1] // TM_FFN
        lax.fori_loop(used, xs_ref.shape[0] // (TM_FFN * ROW_TILES), start_tile, 0)
        for e in range(N_EXPERTS):
            @pl.when(padded_ref[e] > 0)
            def _(e=e):
                window(e).wait()
        lax.fori_loop(used, xs_ref.shape[0] // (TM_FFN * ROW_TILES), wait_tile, 0)

    def issue(r, carry):
        for kk in range(TOP_K):
            _token_copy(h_ref, r, xs_ref, dest_ref[0, 0, r * TOP_K + kk], sem).start()
        return carry

    lax.fori_loop(0, tm, issue, 0)

    def drain(r, carry):
        for kk in range(TOP_K):
            _token_copy(h_ref, 0, xs_ref, 0, sem).wait()
        return carry

    lax.fori_loop(0, tm, drain, 0)


def _dispatch(ends, padded, dest3, h2, n_rows):
    n_tiles, _, per = dest3.shape
    tm = per // TOP_K
    grid_spec = pltpu.PrefetchScalarGridSpec(
        num_scalar_prefetch=2,
        grid=(n_tiles,),
        in_specs=[pl.BlockSpec((1, 1, per), lambda i, en, pa: (i, 0, 0), memory_space=pltpu.SMEM),
                  pl.BlockSpec((tm * ROW_TILES, LANES), lambda i, en, pa: (i, 0))],
        out_specs=pl.BlockSpec(memory_space=pl.ANY),
        scratch_shapes=[pltpu.VMEM((TM_FFN * ROW_TILES, LANES), F32),
                        pltpu.SemaphoreType.DMA(()), pltpu.SemaphoreType.DMA(())],
    )
    return pl.pallas_call(
        _dispatch_body,
        grid_spec=grid_spec,
        out_shape=jax.ShapeDtypeStruct((n_rows * ROW_TILES, LANES), F32),
        compiler_params=_cparams(("arbitrary",)),
        name="dispatch",
    )(ends, padded, dest3, h2)


def _experts_body(te_ref, na_ref, xs_ref, wgu_ref, bgu_ref, wdn_ref, bdn_ref, ys_ref, wgu_bf, wdn_bf):
    i = pl.program_id(0)
    tm = xs_ref.shape[0] // ROW_TILES
    active = i < na_ref[0]

    @pl.when(active)
    def _():
        prev = te_ref[jnp.maximum(i - 1, 0)]

        @pl.when((i == 0) | (te_ref[i] != prev))
        def _():
            wgu_bf[...] = wgu_ref[0, 0].astype(BF16)
            wdn_bf[...] = wdn_ref[0, 0].astype(BF16)

        x = _load_rows(xs_ref, tm).astype(BF16)
        gu = _dot(x, wgu_bf[...]) + bgu_ref[0, 0]
        gate = jnp.minimum(gu[:, :D_FF], SWIGLU_LIMIT)
        up = jnp.clip(gu[:, D_FF:], -SWIGLU_LIMIT, SWIGLU_LIMIT)
        act = (up + 1.0) * gate * jax.nn.sigmoid(SWIGLU_ALPHA * gate)
        y = _dot(act.astype(BF16), wdn_bf[...]) + bdn_ref[0, 0]
        _store_rows(ys_ref, y, tm)

    @pl.when(jnp.logical_not(active))
    def _():
        ys_ref[...] = jnp.zeros_like(ys_ref)


def _experts(layer, tile_expert, n_active, xs, w_gu, b_gu, w_dn, b_dn):
    n_rows = xs.shape[0] // ROW_TILES
    tm = TM_FFN
    b_gu4 = b_gu.reshape(b_gu.shape[0], N_EXPERTS, 1, 2 * D_FF)
    b_dn4 = b_dn.reshape(b_dn.shape[0], N_EXPERTS, 1, D_MODEL)
    grid_spec = pltpu.PrefetchScalarGridSpec(
        num_scalar_prefetch=2,
        grid=(n_rows // tm,),
        in_specs=[pl.BlockSpec((tm * ROW_TILES, LANES), lambda i, te, na: (jnp.minimum(i, na[0] - 1), 0)),
                  pl.BlockSpec((1, 1, D_MODEL, 2 * D_FF), lambda i, te, na: (layer, te[i], 0, 0)),
                  pl.BlockSpec((1, 1, 1, 2 * D_FF), lambda i, te, na: (layer, te[i], 0, 0)),
                  pl.BlockSpec((1, 1, D_FF, D_MODEL), lambda i, te, na: (layer, te[i], 0, 0)),
                  pl.BlockSpec((1, 1, 1, D_MODEL), lambda i, te, na: (layer, te[i], 0, 0))],
        out_specs=pl.BlockSpec((tm * ROW_TILES, LANES), lambda i, te, na: (i, 0)),
        scratch_shapes=[pltpu.VMEM((D_MODEL, 2 * D_FF), BF16), pltpu.VMEM((D_FF, D_MODEL), BF16)],
    )
    return pl.pallas_call(
        _experts_body,
        grid_spec=grid_spec,
        out_shape=jax.ShapeDtypeStruct(xs.shape, F32),
        compiler_params=_cparams(("arbitrary",)),
        name="experts",
    )(tile_expert, n_active, xs, w_gu, b_gu4, w_dn, b_dn4)


def _combine_body(dest_ref, wgt_ref, x_ref, m_ref, g_ref, ys_ref, xo_ref, buf, sem):
    tm = x_ref.shape[0]

    def issue(r, carry):
        for kk in range(TOP_K):
            _token_copy(ys_ref, dest_ref[0, 0, r * TOP_K + kk], buf.at[kk], r, sem).start()
        return carry

    lax.fori_loop(0, tm, issue, 0)

    def drain(r, carry):
        for kk in range(TOP_K):
            _token_copy(ys_ref, 0, buf.at[kk], 0, sem).wait()
        return carry

    lax.fori_loop(0, tm, drain, 0)
    w = wgt_ref[...]
    f = jnp.zeros((tm, D_MODEL), F32)
    for kk in range(TOP_K):
        f = f + w[:, kk:kk + 1] * _load_rows(buf.at[kk], tm)
    xo_ref[...] = x_ref[...] + m_ref[5, 0] * _rms(f, g_ref[...])


def _combine(dest3, wgt, x, mods, gpost, ys, tiles_per_seq):
    t = x.shape[0]
    tm = TM_COMB
    if mods.shape[2] == 1:
        mspec = pl.BlockSpec((6, 1, 1, D_MODEL), lambda i: (0, i // tiles_per_seq, 0, 0))
    else:
        mspec = pl.BlockSpec((6, 1, tm, D_MODEL), lambda i: (0, 0, i, 0))
    return pl.pallas_call(
        _combine_body,
        grid=(t // tm,),
        in_specs=[pl.BlockSpec((1, 1, tm * TOP_K), lambda i: (i, 0, 0), memory_space=pltpu.SMEM),
                  pl.BlockSpec((tm, LANES), lambda i: (i, 0)),
                  pl.BlockSpec((tm, D_MODEL), lambda i: (i, 0)),
                  mspec,
                  pl.BlockSpec((1, D_MODEL), lambda i: (0, 0)),
                  pl.BlockSpec(memory_space=pl.ANY)],
        out_specs=pl.BlockSpec((tm, D_MODEL), lambda i: (i, 0)),
        out_shape=jax.ShapeDtypeStruct((t, D_MODEL), F32),
        scratch_shapes=[pltpu.VMEM((TOP_K, tm * ROW_TILES, LANES), F32), pltpu.SemaphoreType.DMA(())],
        compiler_params=_cparams(("arbitrary",)),
        name="combine",
    )(dest3, wgt, x, mods, gpost, ys)


def _moe(layer, h2_all, t_prompt, x_p, x_s, mods_p, mods_s, gpost, w_r, b_r, tri_strict,
         w_gu, b_gu, w_dn, b_dn, tiles_per_seq_comb):
    t_all = h2_all.shape[0] // ROW_TILES
    idx_w, wgt_w, rank_w, cnt_w = _router(h2_all, w_r, b_r, tri_strict)
    idx = idx_w[:, :TOP_K]
    rank = rank_w[:, :TOP_K]
    cnt = cnt_w[0, :N_EXPERTS].astype(I32)
    padded = ((cnt + TM_FFN - 1) // TM_FFN) * TM_FFN
    ends = jnp.cumsum(padded)
    offs = ends - padded
    dest = jnp.sum(jnp.where(idx[..., None] == jnp.arange(N_EXPERTS, dtype=I32), offs, 0), axis=-1) + rank
    n_tiles = (t_all * TOP_K) // TM_FFN + N_EXPERTS
    tile_start = jnp.arange(n_tiles, dtype=I32) * TM_FFN
    tile_expert = jnp.minimum(jnp.sum((tile_start[:, None] >= ends[None, :]).astype(I32), axis=1), N_EXPERTS - 1)
    n_active = (ends[-1] // TM_FFN).astype(I32).reshape(1)

    dest_flat = dest.reshape(-1)
    xs = _dispatch(ends.astype(I32), padded, dest_flat.reshape(t_all // TM, 1, TM * TOP_K), h2_all, n_tiles * TM_FFN)
    ys = _experts(layer, tile_expert, n_active, xs, w_gu, b_gu, w_dn, b_dn)

    dest_p = dest_flat[:t_prompt * TOP_K].reshape(t_prompt // TM_COMB, 1, TM_COMB * TOP_K)
    dest_s = dest_flat[t_prompt * TOP_K:].reshape(-1, 1, TM_COMB * TOP_K)
    xo_p = _combine(dest_p, wgt_w[:t_prompt], x_p, mods_p, gpost, ys, tiles_per_seq_comb)
    xo_s = _combine(dest_s, wgt_w[t_prompt:], x_s, mods_s, gpost, ys, 1)
    return xo_p, xo_s


def _fox_prompt_body(qa_ref, ka_ref, vat_ref, g_ref, o_ref, m_s, acc_s):
    i = pl.program_id(2)
    tq = qa_ref.shape[0]
    m_s[...] = jnp.full_like(m_s, -jnp.inf)
    acc_s[...] = jnp.zeros_like(acc_s)

    def block(start, diagonal):
        if diagonal:
            causal = lax.broadcasted_iota(I32, (tq, tq), 0) <= lax.broadcasted_iota(I32, (tq, tq), 1)
        for a in range(2):
            cols = slice(a * LANES, (a + 1) * LANES)
            st = _dot_nt(ka_ref[pl.ds(start, tq), cols], qa_ref[:, cols])
            if diagonal:
                st = jnp.where(causal, st, -jnp.inf)
            m_old = m_s[a]
            m_new = jnp.maximum(m_old, jnp.max(st, axis=0, keepdims=True))
            p = jnp.exp((st - m_new).astype(BF16))
            acc_s[a] = jnp.exp(m_old - m_new) * acc_s[a] + _dot(vat_ref[0, cols, pl.ds(start, tq)], p)
            m_s[a] = m_new

    def body(j, carry):
        block(pl.multiple_of(j * tq, tq), False)
        return carry

    lax.fori_loop(0, i, body, 0)
    block(pl.multiple_of(i * tq, tq), True)
    o_t = jnp.concatenate([acc_s[a][:FOX_HD, :] / acc_s[a][FOX_HD:FOX_HD + 1, :] for a in range(2)], axis=0)
    o_ref[...] = o_t.T * jax.nn.sigmoid(g_ref[...])


def _fox_prompt_attn(qa, ka, vat, gate, n_seq, seq):
    t = gate.shape[0]
    nq = seq // TQ
    pairs = FOX_HEADS // 2
    return pl.pallas_call(
        _fox_prompt_body,
        grid=(n_seq, pairs, nq),
        in_specs=[pl.BlockSpec((TQ, 2 * LANES), lambda b, hp, i: (b * nq + i, hp)),
                  pl.BlockSpec((seq, 2 * LANES), lambda b, hp, i: (b, hp)),
                  pl.BlockSpec((1, 2 * LANES, seq), lambda b, hp, i: (b, hp, 0)),
                  pl.BlockSpec((TQ, LANES), lambda b, hp, i: (b * nq + i, hp))],
        out_specs=pl.BlockSpec((TQ, LANES), lambda b, hp, i: (b * nq + i, hp)),
        out_shape=jax.ShapeDtypeStruct((t, FOX_HEADS * FOX_HD), F32),
        scratch_shapes=[pltpu.VMEM((2, 1, TQ), F32), pltpu.VMEM((2, LANES, TQ), F32)],
        compiler_params=_cparams(("arbitrary", "arbitrary", "arbitrary")),
        name="fox_prompt_attn",
    )(qa, ka, vat, gate)


def _fox_sample_body(pt_ref, q_ref, kn_ref, vn_ref, g_ref, cncol_ref, cnrow_ref, *rest):
    npg = PAGES_PER_STEP
    k_refs = rest[:npg]
    v_refs = rest[npg:2 * npg]
    f_refs = rest[2 * npg:3 * npg]
    u_ref, e_ref, o_ref, qbd_s, m_s, l_s, acc_s, car_s = rest[3 * npg:]
    del pt_ref
    j = pl.program_id(1)
    nj = pl.num_programs(1)
    rows = FOX_HEADS * TOK_PAD
    wide = FOX_HEADS * FOX_HD
    own = ((lax.broadcasted_iota(I32, (rows, wide), 0) // TOK_PAD)
           == (lax.broadcasted_iota(I32, (rows, wide), 1) // FOX_HD))

    @pl.when(j == 0)
    def _():
        m_s[...] = jnp.full_like(m_s, -jnp.inf)
        l_s[...] = jnp.zeros_like(l_s)
        acc_s[...] = jnp.zeros_like(acc_s)
        car_s[...] = jnp.zeros_like(car_s)
        q = q_ref[...] * (FOX_HD ** -0.5)
        qbd_s[...] = jnp.where(own, jnp.concatenate([q] * FOX_HEADS, axis=0), 0.0).astype(BF16)

    qbd = qbd_s[...]
    cn = cncol_ref[0]

    def update(s, pv_of):
        m_old = m_s[...]
        m_new = jnp.maximum(m_old, jnp.max(s, axis=1, keepdims=True))
        alpha = jnp.exp(m_old - m_new)
        p = jnp.exp(s - m_new)
        l_s[...] = alpha * l_s[...] + jnp.sum(p, axis=1, keepdims=True)
        acc_s[...] = alpha * acc_s[...] + pv_of(p.astype(BF16))
        m_s[...] = m_new

    suffixes = []
    for r in range(npg):
        ft = f_refs[r][0]
        suffixes.append(_dot(ft, u_ref[...], HI) + car_s[...])
        car_s[...] = car_s[...] + jnp.sum(ft, axis=1, keepdims=True)
    bias = _dot(e_ref[...], jnp.concatenate(suffixes, axis=1), HI) + cn
    kt = jnp.concatenate([k_refs[r][0].astype(BF16) for r in range(npg)], axis=1)
    vt = jnp.concatenate([v_refs[r][0].astype(BF16) for r in range(npg)], axis=1)
    update(_dot(qbd, kt) + bias, lambda pb: _dot_nt(pb, vt))

    @pl.when(j == nj - 1)
    def _():
        zpad = jnp.zeros((PAGE_SIZE - TOK_PAD, wide), F32)
        kn = jnp.concatenate([kn_ref[...], zpad], axis=0).astype(BF16)
        vn = jnp.concatenate([vn_ref[...], zpad], axis=0).astype(BF16)
        trow = lax.broadcasted_iota(I32, (rows, PAGE_SIZE), 0) & (TOK_PAD - 1)
        tcol = lax.broadcasted_iota(I32, (rows, PAGE_SIZE), 1)
        s = jnp.where(tcol <= trow, _dot_nt(qbd, kn) + cn - cnrow_ref[0], -jnp.inf)
        update(s, lambda pb: _dot(pb, vn))
        o = jnp.where(own, acc_s[...] / l_s[...], 0.0)
        o8 = o[0:TOK_PAD, :]
        for h in range(1, FOX_HEADS):
            o8 = o8 + o[h * TOK_PAD:(h + 1) * TOK_PAD, :]
        o_ref[...] = o8 * jax.nn.sigmoid(g_ref[...])


def _fox_sample_attn(proj, cncol, cnrow, cache_k, cache_v, cache_ft, page_table, u_mat, e_mat):
    n_seq, n_pages = page_table.shape
    npg = PAGES_PER_STEP
    steps = n_pages // npg
    wide = FOX_HEADS * FOX_HD
    rows = FOX_HEADS * TOK_PAD

    def page(r):
        return lambda b, j, pt: pt[b * n_pages + (n_pages - 1 - (j * npg + r))]

    kv_specs = [pl.BlockSpec((1, wide, PAGE_SIZE), (lambda b, j, pt, r=r: (page(r)(b, j, pt), 0, 0)))
                for r in range(npg)]
    f_specs = [pl.BlockSpec((1, FOX_HEADS, PAGE_SIZE), (lambda b, j, pt, r=r: (page(r)(b, j, pt), 0, 0)))
               for r in range(npg)]
    grid_spec = pltpu.PrefetchScalarGridSpec(
        num_scalar_prefetch=1,
        grid=(n_seq, steps),
        in_specs=[pl.BlockSpec((TOK_PAD, wide), lambda b, j, pt: (b, 0)),
                  pl.BlockSpec((TOK_PAD, wide), lambda b, j, pt: (b, 1)),
                  pl.BlockSpec((TOK_PAD, wide), lambda b, j, pt: (b, 2)),
                  pl.BlockSpec((TOK_PAD, wide), lambda b, j, pt: (b, 3)),
                  pl.BlockSpec((1, rows, 1), lambda b, j, pt: (b, 0, 0)),
                  pl.BlockSpec((1, rows, PAGE_SIZE), lambda b, j, pt: (b, 0, 0))]
                 + kv_specs + kv_specs + f_specs
                 + [pl.BlockSpec((PAGE_SIZE, PAGE_SIZE), lambda b, j, pt: (0, 0)),
                    pl.BlockSpec((rows, FOX_HEADS), lambda b, j, pt: (0, 0))],
        out_specs=pl.BlockSpec((TOK_PAD, wide), lambda b, j, pt: (b, 0)),
        scratch_shapes=[pltpu.VMEM((rows, wide), BF16), pltpu.VMEM((rows, 1), F32), pltpu.VMEM((rows, 1), F32),
                        pltpu.VMEM((rows, wide), F32), pltpu.VMEM((FOX_HEADS, 1), F32)],
    )
    return pl.pallas_call(
        _fox_sample_body,
        grid_spec=grid_spec,
        out_shape=jax.ShapeDtypeStruct((n_seq * TOK_PAD, wide), F32),
        compiler_params=_cparams(("arbitrary", "arbitrary")),
        name="fox_sample_attn",
    )(page_table.reshape(-1), proj, proj, proj, proj, cncol, cnrow,
      *([cache_k] * npg), *([cache_v] * npg), *([cache_ft] * npg), u_mat, e_mat)


def _rope_tables(pos):
    half = RET_DK // 2
    inv = ROPE_BASE ** (-jnp.arange(half, dtype=F32) / half)
    ang = pos.astype(F32)[:, None] * inv[None, :]
    return jnp.cos(ang), jnp.sin(ang)


def _mods_prompt(m):
    b = m.shape[0]
    return m.reshape(b, 6, D_MODEL).transpose(1, 0, 2)[:, :, None, :]


def _mods_sample(m):
    b = m.shape[0]
    mm = jnp.repeat(m.reshape(b, 6, D_MODEL), TOK_PAD, axis=0)
    return mm.transpose(1, 0, 2)[:, None, :, :]


def kernel(x_prompt, x_sample, c_prompt, c_sample, state_ret, cache_fox_k, cache_fox_v, cache_fox_logf, page_table, w_ada, b_ada, norm_pre_mix, norm_post_mix, norm_pre_ffn, norm_post_ffn, ret_w_in, ret_gn_gain, ret_w_out, fox_w_in, fox_b_f, fox_w_out, moe_w_router, moe_b_router, moe_w_gate_up, moe_b_gate_up, moe_w_down, moe_b_down):
    n_seq, seq, d = x_prompt.shape
    n_dec, dec_seq, _ = x_sample.shape
    t_p = n_seq * seq
    t_s = n_dec * TOK_PAD
    tps = seq // TM

    xp = x_prompt.reshape(t_p, d)
    xs = jnp.pad(x_sample, ((0, 0), (0, TOK_PAD - dec_seq), (0, 0))).reshape(t_s, d)

    m_all = _adaln(jnp.concatenate([c_prompt, c_sample], axis=0), w_ada, b_ada)

    ar = jnp.arange(TM)
    tri_incl = (ar[None, :] <= ar[:, None]).astype(F32)
    tri_strict = (ar[None, :] < ar[:, None]).astype(BF16)
    tri_group = ((ar[None, :] <= ar[:, None]) & (ar[None, :] // TOK_PAD == ar[:, None] // TOK_PAD)).astype(F32)

    mods_p = _mods_prompt(m_all[0, :n_seq])
    mods_s = _mods_sample(m_all[0, n_seq:])
    w_in = ret_w_in[0].astype(BF16)
    cos_p, sin_p = _rope_tables(jnp.arange(seq))
    pos_s = jnp.tile(PAST_LEN + jnp.arange(TOK_PAD), n_dec)
    cos_s, sin_s = _rope_tables(pos_s)
    g_pre = norm_pre_mix[0:1]
    proj_p = _inproj_ret(xp, mods_p, g_pre, w_in, cos_p, sin_p, tps)
    proj_s = _inproj_ret(xs, mods_s, g_pre, ret_w_in[0], cos_s, sin_s, 1)

    n_chunks = seq // RET_CHUNK
    din, dq, dk, dc = _ret_decay_tables(RET_CHUNK, RET_CHUNK, RET_CHUNK)
    s0p = jnp.zeros((n_seq, RET_HEADS, RET_DK, RET_DV), F32)
    u_p, st_p = _retention(proj_p, s0p, din, dq, dk, dc, ret_gn_gain, n_seq, n_chunks, RET_CHUNK, False)
    din, dq, dk, dc = _ret_decay_tables(dec_seq, TOK_PAD, RET_CHUNK)
    u_s, st_s = _retention(proj_s, state_ret[0], din, dq, dk, dc, ret_gn_gain, n_dec, 1, TOK_PAD, True)

    w_out = ret_w_out[0].astype(BF16)
    xp, h2p = _outproj(u_p, w_out, xp, mods_p, norm_post_mix[0:1], norm_pre_ffn[0:1], tps)
    xs, h2s = _outproj(u_s, ret_w_out[0], xs, mods_s, norm_post_mix[0:1], norm_pre_ffn[0:1], 1)

    def router_params(i):
        w_r = jnp.pad(moe_w_router[i], ((0, 0), (0, LANES - N_EXPERTS)))
        b_r = jnp.pad(moe_b_router[i], (0, LANES - N_EXPERTS), constant_values=-1e30)[None, :]
        return w_r, b_r

    w_r, b_r = router_params(0)
    xp, xs = _moe(0, jnp.concatenate([h2p, h2s], axis=0), t_p, xp, xs, mods_p, mods_s, norm_post_ffn[0:1],
                  w_r, b_r, tri_strict, moe_w_gate_up, moe_b_gate_up, moe_w_down, moe_b_down, seq // TM_COMB)

    mods_p = _mods_prompt(m_all[1, :n_seq])
    mods_s = _mods_sample(m_all[1, n_seq:])
    fw = FOX_HEADS * FOX_HD
    w_in = fox_w_in[0, :, :4 * fw].astype(BF16)
    w_fl = jnp.pad(fox_w_in[0, :, 4 * fw:], ((0, 0), (0, LANES - FOX_HEADS)))
    b_fl = jnp.pad(fox_b_f[0], (0, LANES - FOX_HEADS))[None, :]
    g_pre = norm_pre_mix[1:2]
    wkt = fox_w_in[0, :, fw:2 * fw].T.astype(BF16)
    wvt = fox_w_in[0, :, 2 * fw:3 * fw].T.astype(BF16)
    gate_p, lf_p, qa, ka, vat, kt_p, vt_p = _inproj_fox(xp, mods_p, g_pre, w_in, w_fl, b_fl, tri_incl, tps, wkt, wvt)
    fproj_s, lf_s, cum_s = _inproj_fox(xs, mods_s, g_pre, w_in, w_fl, b_fl, tri_group, 1)
    o_p = _fox_prompt_attn(qa, ka, vat, gate_p, n_seq, seq)

    cn = cum_s[:, :FOX_HEADS].reshape(n_dec, TOK_PAD, FOX_HEADS).transpose(0, 2, 1)
    cncol = cn.reshape(n_dec, FOX_HEADS * TOK_PAD, 1)
    cnrow = jnp.repeat(cn, TOK_PAD, axis=1)
    cnrow = jnp.pad(cnrow, ((0, 0), (0, 0), (0, PAGE_SIZE - TOK_PAD)))
    pr = jnp.arange(PAGE_SIZE)
    u_mat = (pr[:, None] > pr[None, :]).astype(F32)
    e_mat = (jnp.arange(FOX_HEADS * TOK_PAD)[:, None] // TOK_PAD == jnp.arange(FOX_HEADS)[None, :]).astype(F32)
    n_pool = cache_fox_k.shape[1]
    cache_kt = cache_fox_k[0].transpose(0, 2, 3, 1).reshape(n_pool, fw, PAGE_SIZE)
    cache_vt = cache_fox_v[0].transpose(0, 2, 3, 1).reshape(n_pool, fw, PAGE_SIZE)
    cache_ft = cache_fox_logf[0].transpose(0, 2, 1)
    o_s = _fox_sample_attn(fproj_s, cncol, cnrow, cache_kt, cache_vt, cache_ft, page_table, u_mat, e_mat)

    w_out = fox_w_out[0].astype(BF16)
    xp, h2p = _outproj(o_p, w_out, xp, mods_p, norm_post_mix[1:2], norm_pre_ffn[1:2], tps)
    xs, h2s = _outproj(o_s, w_out, xs, mods_s, norm_post_mix[1:2], norm_pre_ffn[1:2], 1)
    w_r, b_r = router_params(1)
    xp, xs = _moe(1, jnp.concatenate([h2p, h2s], axis=0), t_p, xp, xs, mods_p, mods_s, norm_post_ffn[1:2],
                  w_r, b_r, tri_strict, moe_w_gate_up, moe_b_gate_up, moe_w_down, moe_b_down, seq // TM_COMB)

    y_prompt = xp.reshape(n_seq, seq, d)
    y_sample = xs.reshape(n_dec, TOK_PAD, d)[:, :dec_seq]
    k_prompt = kt_p.reshape(n_seq, FOX_HEADS, FOX_HD, seq).transpose(0, 3, 1, 2)[None]
    v_prompt = vt_p.reshape(n_seq, FOX_HEADS, FOX_HD, seq).transpose(0, 3, 1, 2)[None]
    logf_prompt = lf_p[:, :FOX_HEADS].reshape(1, n_seq, seq, FOX_HEADS)
    fs = fproj_s.reshape(n_dec, TOK_PAD, 4 * fw)[:, :dec_seq]
    kv_shape_s = (1, n_dec, dec_seq, FOX_HEADS, FOX_HD)
    k_sample = fs[..., fw:2 * fw].reshape(kv_shape_s)
    v_sample = fs[..., 2 * fw:3 * fw].reshape(kv_shape_s)
    logf_sample = lf_s.reshape(n_dec, TOK_PAD, LANES)[:, :dec_seq, :FOX_HEADS][None]
    return (y_prompt, y_sample, st_p[None], st_s[None], k_prompt, v_prompt, logf_prompt,
            k_sample, v_sample, logf_sample)
```

```python
import functools

import jax
import jax.numpy as jnp
from jax import lax
from jax.experimental import pallas as pl
from jax.experimental.pallas import tpu as pltpu

F32 = jnp.float32
BF16 = jnp.bfloat16
I32 = jnp.int32
HI = lax.Precision.HIGHEST

D_MODEL = 1024
PAST_LEN = 8192
PAGE_SIZE = 128
RET_HEADS = 4
RET_DK = D_MODEL // RET_HEADS
RET_DV = 2 * RET_DK
RET_CHUNK = 128
ROPE_BASE = 10000.0
FOX_HEADS = 16
FOX_HD = D_MODEL // FOX_HEADS
N_EXPERTS = 32
TOP_K = 4
D_FF = D_MODEL
SWIGLU_LIMIT = 7.0
SWIGLU_ALPHA = 1.702
NORM_EPS = 1e-6

LANES = 128
SUBLANES = 8
ROW_TILES = D_MODEL // LANES
TOK_PAD = SUBLANES
TM = 256
TM_FFN = 512
TM_COMB = 128
TQ = 512
PAGES_PER_STEP = 16
VMEM_LIMIT = 56 * 1024 * 1024


def _cparams(sem, vmem=VMEM_LIMIT):
    return pltpu.CompilerParams(dimension_semantics=sem, vmem_limit_bytes=vmem)


def _rms(x, gain):
    return x * lax.rsqrt(jnp.mean(x * x, axis=-1, keepdims=True) + NORM_EPS) * gain


def _dot(a, b, precision=None):
    return jnp.dot(a, b, precision=precision, preferred_element_type=F32)


def _dot_nt(a, b):
    return lax.dot_general(a, b, (((1,), (1,)), ((), ())), preferred_element_type=F32)


def _mm(a, b, precise, dims=None):
    if precise:
        a, b, prec = a.astype(F32), b.astype(F32), HI
    else:
        a, b, prec = a.astype(BF16), b.astype(BF16), None
    if dims is None:
        return jnp.dot(a, b, precision=prec, preferred_element_type=F32)
    return lax.dot_general(a, b, (dims, ((), ())), precision=prec, preferred_element_type=F32)


_NT = ((1,), (1,))
_TN = ((0,), (0,))


def _load_rows(ref, rows):
    return jnp.concatenate([ref[pl.ds(s, rows, stride=ROW_TILES), :] for s in range(ROW_TILES)], axis=1)


def _store_rows(ref, val, rows):
    for s in range(ROW_TILES):
        ref[pl.ds(s, rows, stride=ROW_TILES), :] = val[:, s * LANES:(s + 1) * LANES]


def _ada_body(c_ref, w_ref, b_ref, o_ref):
    c = c_ref[...]
    a = c * jax.nn.sigmoid(c)
    o_ref[0] = _dot(a, w_ref[0], HI) + b_ref[0]


def _adaln(c_all, w_ada, b_ada):
    depth, d, n = w_ada.shape
    nb = c_all.shape[0]
    tn = 768
    return pl.pallas_call(
        _ada_body,
        grid=(depth, n // tn),
        in_specs=[pl.BlockSpec((nb, d), lambda l, j: (0, 0)),
                  pl.BlockSpec((1, d, tn), lambda l, j: (l, 0, j)),
                  pl.BlockSpec((1, 1, tn), lambda l, j: (l, 0, j))],
        out_specs=pl.BlockSpec((1, nb, tn), lambda l, j: (l, 0, j)),
        out_shape=jax.ShapeDtypeStruct((depth, nb, n), F32),
        compiler_params=_cparams(("arbitrary", "arbitrary")),
        name="adaln",
    )(c_all, w_ada, b_ada.reshape(depth, 1, n))


def _mod_spec(mods, tiles_per_seq):
    r = mods.shape[2]
    if r == 1:
        return pl.BlockSpec((6, 1, 1, D_MODEL), lambda i: (0, i // tiles_per_seq, 0, 0))
    return pl.BlockSpec((6, 1, r, D_MODEL), lambda i: (0, 0, 0, 0))


def _inproj_ret_body(x_ref, m_ref, g_ref, w_ref, cos_ref, sin_ref, o_ref, *, precise):
    h = _rms(x_ref[...], g_ref[...]) * (1.0 + m_ref[1, 0]) + m_ref[0, 0]
    hb = h if precise else h.astype(BF16)
    cos = cos_ref[...]
    sin = sin_ref[...]
    half = RET_DK // 2
    for c in range(2 * RET_HEADS):
        lo = c * RET_DK
        r = _mm(hb, w_ref[:, lo:lo + RET_DK], precise)
        x1 = r[:, :half]
        x2 = r[:, half:]
        sc = RET_DK ** -0.5 if c < RET_HEADS else 1.0
        o_ref[:, lo:lo + half] = (x1 * cos - x2 * sin) * sc
        o_ref[:, lo + half:lo + RET_DK] = (x1 * sin + x2 * cos) * sc
    base = 2 * RET_HEADS * RET_DK
    for c in range(2 * RET_HEADS):
        lo = base + c * RET_DV
        o_ref[:, lo:lo + RET_DV] = _mm(hb, w_ref[:, lo:lo + RET_DV], precise)


def _inproj_ret(x, mods, gain, w, cos, sin, tiles_per_seq):
    t = x.shape[0]
    n = w.shape[1]
    tm = min(TM, t)
    pos_tiles = cos.shape[0] // tm
    return pl.pallas_call(
        functools.partial(_inproj_ret_body, precise=w.dtype == F32),
        grid=(t // tm,),
        in_specs=[pl.BlockSpec((tm, D_MODEL), lambda i: (i, 0)),
                  _mod_spec(mods, tiles_per_seq),
                  pl.BlockSpec((1, D_MODEL), lambda i: (0, 0)),
                  pl.BlockSpec((D_MODEL, n), lambda i: (0, 0), pipeline_mode=pl.Buffered(1)),
                  pl.BlockSpec((tm, RET_DK // 2), lambda i: (i % pos_tiles, 0)),
                  pl.BlockSpec((tm, RET_DK // 2), lambda i: (i % pos_tiles, 0))],
        out_specs=pl.BlockSpec((tm, n), lambda i: (i, 0)),
        out_shape=jax.ShapeDtypeStruct((t, n), F32),
        compiler_params=_cparams(("arbitrary",)),
        name="inproj_ret",
    )(x, mods, gain, w, cos, sin)


def _split3(c):
    hi = c.astype(BF16).astype(F32)
    r = c - hi
    mid = r.astype(BF16).astype(F32)
    return hi, mid, r - mid


def _emit_attention_operands(o_ref, cs, qa_ref, ka_ref):
    tm = cs.shape[0]
    fw = FOX_HEADS * FOX_HD
    lane = lax.broadcasted_iota(I32, (tm, LANES), 1)
    low = lane < FOX_HD
    ones_q = jnp.where((lane >= FOX_HD + 3) & (lane < FOX_HD + 6), 1.0, 0.0)
    ones_k = jnp.where((lane >= FOX_HD) & (lane < FOX_HD + 3), 1.0, 0.0)
    for p in range(FOX_HEADS // 2):
        pairs = [o_ref[:, s * fw + p * LANES:s * fw + (p + 1) * LANES] for s in range(2)]
        pairs[0] = pairs[0] * (FOX_HD ** -0.5)
        for a in range(2):
            hh = 2 * p + a
            qh, kh = pairs if a == 0 else [pltpu.roll(x, FOX_HD, 1) for x in pairs]
            c = jnp.sum(jnp.where(lane == hh, cs, 0.0), axis=1, keepdims=True)
            hi, mid, lo = _split3(c)
            fq = jnp.where(lane == FOX_HD, hi, jnp.where(lane == FOX_HD + 1, mid,
                                                         jnp.where(lane == FOX_HD + 2, lo, ones_q)))
            fk = jnp.where(lane == FOX_HD + 3, -hi, jnp.where(lane == FOX_HD + 4, -mid,
                                                              jnp.where(lane == FOX_HD + 5, -lo, ones_k)))
            cols = slice(hh * LANES, (hh + 1) * LANES)
            qa_ref[:, cols] = jnp.where(low, qh, fq).astype(BF16)
            ka_ref[:, cols] = jnp.where(low, kh, fk).astype(BF16)


def _inproj_fox_body(x_ref, m_ref, g_ref, w_ref, wfl_ref, bf_ref, tri_ref, *rest, tiles_per_seq, prompt):
    if prompt:
        wkt_ref, wvt_ref, gate_ref, lf_ref, qa_ref, ka_ref, vat_ref, kt_ref, vt_ref, stage, carry = rest
    else:
        stage, lf_ref, cum_ref = rest
    i = pl.program_id(0)
    tm = x_ref.shape[0]
    fw = FOX_HEADS * FOX_HD
    h = _rms(x_ref[...], g_ref[...]) * (1.0 + m_ref[1, 0]) + m_ref[0, 0]
    hb = h.astype(BF16)
    cw = 512
    for c in range(4 * fw // cw):
        if prompt and 2 * fw <= c * cw < 3 * fw:
            continue
        r = _dot(hb, w_ref[:, c * cw:(c + 1) * cw])
        if prompt and c * cw >= 3 * fw:
            gate_ref[:, c * cw - 3 * fw:(c + 1) * cw - 3 * fw] = r
        else:
            stage[:, c * cw:(c + 1) * cw] = r
    fl = _dot(h, wfl_ref[...], HI) + bf_ref[...]
    lf = jnp.minimum(fl, 0.0) - jnp.log1p(jnp.exp(-jnp.abs(fl)))
    lf_ref[...] = lf
    cs = _dot(tri_ref[...], lf, HI)
    if not prompt:
        cum_ref[...] = cs
        return

    @pl.when(i % tiles_per_seq == 0)
    def _():
        carry[...] = jnp.zeros_like(carry)
    cs = cs + carry[...]
    carry[...] = cs[tm - 1:tm, :]
    _emit_attention_operands(stage, cs, qa_ref, ka_ref)
    kt_ref[0] = _dot_nt(wkt_ref[...], hb)
    vt = _dot_nt(wvt_ref[...], hb)
    vt_ref[0] = vt
    ones_rows = jnp.where(lax.broadcasted_iota(I32, (LANES - FOX_HD, tm), 0) == 0, 1.0, 0.0)
    for hh in range(FOX_HEADS):
        vat_ref[0, hh * LANES:(hh + 1) * LANES, :] = jnp.concatenate(
            [vt[hh * FOX_HD:(hh + 1) * FOX_HD, :], ones_rows], axis=0).astype(BF16)


def _inproj_fox(x, mods, gain, w_bf, w_fl, b_fl, tri, tiles_per_seq, wkt=None, wvt=None):
    t = x.shape[0]
    n = w_bf.shape[1]
    tm = min(TM, t)
    prompt = wkt is not None
    body = functools.partial(_inproj_fox_body, tiles_per_seq=tiles_per_seq, prompt=prompt)
    fw = FOX_HEADS * FOX_HD
    wide = FOX_HEADS * LANES
    row = lambda width: pl.BlockSpec((tm, width), lambda i: (i, 0))
    in_specs = [row(D_MODEL),
                _mod_spec(mods, tiles_per_seq),
                pl.BlockSpec((1, D_MODEL), lambda i: (0, 0)),
                pl.BlockSpec((D_MODEL, n), lambda i: (0, 0), pipeline_mode=pl.Buffered(1)),
                pl.BlockSpec((D_MODEL, LANES), lambda i: (0, 0)),
                pl.BlockSpec((1, LANES), lambda i: (0, 0)),
                pl.BlockSpec((tm, tm), lambda i: (0, 0))]
    args = [x, mods, gain, w_bf, w_fl, b_fl, tri]
    if prompt:
        n_seq = t // (tiles_per_seq * tm)
        tspec = pl.BlockSpec((1, fw, tm), lambda i: (i // tiles_per_seq, 0, i % tiles_per_seq))
        vspec = pl.BlockSpec((1, wide, tm), lambda i: (i // tiles_per_seq, 0, i % tiles_per_seq))
        in_specs += [pl.BlockSpec((fw, D_MODEL), lambda i: (0, 0), pipeline_mode=pl.Buffered(1))] * 2
        args += [wkt, wvt]
        out_specs = [row(fw), row(LANES), row(wide), row(wide), vspec, tspec, tspec]
        out_shape = ([jax.ShapeDtypeStruct((t, fw), F32), jax.ShapeDtypeStruct((t, LANES), F32)]
                     + [jax.ShapeDtypeStruct((t, wide), BF16)] * 2
                     + [jax.ShapeDtypeStruct((n_seq, wide, tiles_per_seq * tm), BF16)]
                     + [jax.ShapeDtypeStruct((n_seq, fw, tiles_per_seq * tm), F32)] * 2)
        scratch = [pltpu.VMEM((tm, 2 * fw), F32), pltpu.VMEM((1, LANES), F32)]
    else:
        out_specs = [row(n), row(LANES), row(LANES)]
        out_shape = [jax.ShapeDtypeStruct((t, n), F32), jax.ShapeDtypeStruct((t, LANES), F32),
                     jax.ShapeDtypeStruct((t, LANES), F32)]
        scratch = []
    return pl.pallas_call(
        body,
        grid=(t // tm,),
        in_specs=in_specs,
        out_specs=out_specs,
        out_shape=out_shape,
        scratch_shapes=scratch,
        compiler_params=_cparams(("arbitrary",)),
        name="inproj_fox",
    )(*args)


def _retention_body(q_ref, k_ref, v_ref, g_ref, s0_ref, din_ref, dq_ref, dk_ref, dc_ref, gn_ref,
                    u_ref, so_ref, state, *, n_chunks, kv_rows, precise):
    c = pl.program_id(1)

    @pl.when(c == 0)
    def _():
        state[...] = s0_ref[0]

    for hh in range(RET_HEADS):
        qk = slice(hh * RET_DK, (hh + 1) * RET_DK)
        vg = slice(hh * RET_DV, (hh + 1) * RET_DV)
        q = q_ref[:, qk]
        k = k_ref[:, qk]
        v = v_ref[:, vg]
        rows = k.shape[0]
        if rows < kv_rows:
            k = jnp.concatenate([k, jnp.zeros((kv_rows - rows, RET_DK), F32)], axis=0)
            v = jnp.concatenate([v, jnp.zeros((kv_rows - rows, RET_DV), F32)], axis=0)
        kd = k * dk_ref[hh]
        s_old = state[hh]
        inner = _mm(q, k, precise, _NT) * din_ref[hh]
        o = _mm(inner, v, precise) + _mm(q, s_old, precise) * dq_ref[hh]
        s_new = s_old * dc_ref[hh] + _mm(kd, v, False, _TN)
        state[hh] = s_new
        g = g_ref[:, vg]
        u_ref[:, vg] = _rms(o, gn_ref[:, vg]) * (g * jax.nn.sigmoid(g))

        @pl.when(c == n_chunks - 1)
        def _(hh=hh, s_new=s_new):
            so_ref[0, hh] = s_new


def _retention(proj, s0, din, dq, dk, dc, gn_gain, n_seq, n_chunks, q_rows, precise):
    t = proj.shape[0]
    kv_rows = din.shape[2]
    h = RET_HEADS
    hk, hv = h * RET_DK, h * RET_DV
    body = functools.partial(_retention_body, n_chunks=n_chunks, kv_rows=kv_rows, precise=precise)
    row = lambda b, c: b * n_chunks + c
    whole = lambda shape: pl.BlockSpec(shape, lambda b, c: (0,) * len(shape))
    return pl.pallas_call(
        body,
        grid=(n_seq, n_chunks),
        in_specs=[pl.BlockSpec((q_rows, hk), lambda b, c: (row(b, c), 0)),
                  pl.BlockSpec((q_rows, hk), lambda b, c: (row(b, c), 1)),
                  pl.BlockSpec((q_rows, hv), lambda b, c: (row(b, c), 2 * hk // hv)),
                  pl.BlockSpec((q_rows, hv), lambda b, c: (row(b, c), 2 * hk // hv + 1)),
                  pl.BlockSpec((1, h, RET_DK, RET_DV), lambda b, c: (b, 0, 0, 0)),
                  whole((h, q_rows, kv_rows)), whole((h, q_rows, 1)), whole((h, kv_rows, 1)), whole((h, 1, 1)),
                  whole((1, hv))],
        out_specs=[pl.BlockSpec((q_rows, hv), lambda b, c: (row(b, c), 0)),
                   pl.BlockSpec((1, h, RET_DK, RET_DV), lambda b, c: (b, 0, 0, 0))],
        out_shape=[jax.ShapeDtypeStruct((t, hv), F32),
                   jax.ShapeDtypeStruct((n_seq, h, RET_DK, RET_DV), F32)],
        scratch_shapes=[pltpu.VMEM((h, RET_DK, RET_DV), F32)],
        compiler_params=_cparams(("arbitrary", "arbitrary")),
        name="retention",
    )(proj, proj, proj, proj, s0, din, dq, dk, dc, gn_gain)


def _ret_decay_tables(c, q_rows, kv_rows):
    lg = jnp.log(1.0 - 2.0 ** (-5.0 - jnp.arange(RET_HEADS, dtype=F32)))
    idx = jnp.arange(c, dtype=F32)
    rel = idx[:, None] - idx[None, :]
    din = jnp.where(rel[None] >= 0, jnp.exp(lg[:, None, None] * jnp.maximum(rel, 0.0)[None]), 0.0)
    dq = jnp.exp(lg[:, None] * (idx[None, :] + 1.0))
    dk = jnp.exp(lg[:, None] * (c - 1.0 - idx[None, :]))
    dc = jnp.exp(lg * c)
    din = jnp.pad(din, ((0, 0), (0, q_rows - c), (0, kv_rows - c)))
    dq = jnp.pad(dq, ((0, 0), (0, q_rows - c)))[..., None]
    dk = jnp.pad(dk, ((0, 0), (0, kv_rows - c)))[..., None]
    return din, dq, dk, dc[:, None, None]


def _outproj_body(u_ref, w_ref, x_ref, m_ref, gpost_ref, gpre_ref, xn_ref, h2_ref):
    tm = x_ref.shape[0]
    y = _mm(u_ref[...], w_ref[...], w_ref.dtype == F32)
    xn = x_ref[...] + m_ref[2, 0] * _rms(y, gpost_ref[...])
    xn_ref[...] = xn
    h2 = _rms(xn, gpre_ref[...]) * (1.0 + m_ref[4, 0]) + m_ref[3, 0]
    _store_rows(h2_ref, h2, tm)


def _outproj(u, w, x, mods, gpost, gpre, tiles_per_seq):
    t = x.shape[0]
    din = u.shape[1]
    tm = min(TM, t)
    return pl.pallas_call(
        _outproj_body,
        grid=(t // tm,),
        in_specs=[pl.BlockSpec((tm, din), lambda i: (i, 0)),
                  pl.BlockSpec((din, D_MODEL), lambda i: (0, 0), pipeline_mode=pl.Buffered(1)),
                  pl.BlockSpec((tm, D_MODEL), lambda i: (i, 0)),
                  _mod_spec(mods, tiles_per_seq),
                  pl.BlockSpec((1, D_MODEL), lambda i: (0, 0)),
                  pl.BlockSpec((1, D_MODEL), lambda i: (0, 0))],
        out_specs=[pl.BlockSpec((tm, D_MODEL), lambda i: (i, 0)),
                   pl.BlockSpec((tm * ROW_TILES, LANES), lambda i: (i, 0))],
        out_shape=[jax.ShapeDtypeStruct((t, D_MODEL), F32),
                   jax.ShapeDtypeStruct((t * ROW_TILES, LANES), F32)],
        compiler_params=_cparams(("arbitrary",)),
        name="outproj",
    )(u, w, x, mods, gpost, gpre)


def _router_body(h_ref, w_ref, b_ref, tri_ref, idx_ref, wgt_ref, rank_ref, cnt_ref, carry):
    i = pl.program_id(0)
    tm = idx_ref.shape[0]

    @pl.when(i == 0)
    def _():
        carry[...] = jnp.zeros_like(carry)

    h = _load_rows(h_ref, tm)
    logits = _dot(h, w_ref[...], HI) + b_ref[...]
    lane = lax.broadcasted_iota(I32, (tm, LANES), 1)
    lane_f = lane.astype(F32)
    work = logits
    vals, ids, hots = [], [], []
    for _ in range(TOP_K):
        mx = jnp.max(work, axis=1, keepdims=True)
        ik = jnp.min(jnp.where(work == mx, lane_f, float(LANES)), axis=1, keepdims=True)
        hot = lane_f == ik
        vals.append(mx)
        ids.append(ik.astype(I32))
        hots.append(hot)
        work = jnp.where(hot, -jnp.inf, work)
    ex = [jnp.exp(v - vals[0]) for v in vals]
    den = ex[0] + ex[1] + ex[2] + ex[3]
    chosen = hots[0] | hots[1] | hots[2] | hots[3]
    before = _dot(tri_ref[...], chosen.astype(BF16)) + carry[...]
    idx_o = jnp.zeros((tm, LANES), I32)
    wgt_o = jnp.zeros((tm, LANES), F32)
    rank_o = jnp.zeros((tm, LANES), I32)
    for kk in range(TOP_K):
        rk = jnp.sum(jnp.where(hots[kk], before, 0.0), axis=1, keepdims=True).astype(I32)
        idx_o = jnp.where(lane == kk, ids[kk], idx_o)
        wgt_o = jnp.where(lane == kk, ex[kk] / den, wgt_o)
        rank_o = jnp.where(lane == kk, rk, rank_o)
    idx_ref[...] = idx_o
    wgt_ref[...] = wgt_o
    rank_ref[...] = rank_o
    carry[...] = carry[...] + jnp.sum(chosen.astype(F32), axis=0, keepdims=True)
    cnt_ref[...] = carry[...]


def _router(h2, w_r, b_r, tri):
    t = h2.shape[0] // ROW_TILES
    tm = TM
    return pl.pallas_call(
        _router_body,
        grid=(t // tm,),
        in_specs=[pl.BlockSpec((tm * ROW_TILES, LANES), lambda i: (i, 0)),
                  pl.BlockSpec((D_MODEL, LANES), lambda i: (0, 0)),
                  pl.BlockSpec((1, LANES), lambda i: (0, 0)),
                  pl.BlockSpec((tm, tm), lambda i: (0, 0))],
        out_specs=[pl.BlockSpec((tm, LANES), lambda i: (i, 0)),
                   pl.BlockSpec((tm, LANES), lambda i: (i, 0)),
                   pl.BlockSpec((tm, LANES), lambda i: (i, 0)),
                   pl.BlockSpec((1, LANES), lambda i: (0, 0))],
        out_shape=[jax.ShapeDtypeStruct((t, LANES), I32),
                   jax.ShapeDtypeStruct((t, LANES), F32),
                   jax.ShapeDtypeStruct((t, LANES), I32),
                   jax.ShapeDtypeStruct((1, LANES), F32)],
        scratch_shapes=[pltpu.VMEM((1, LANES), F32)],
        compiler_params=_cparams(("arbitrary",)),
        name="router",
    )(h2, w_r, b_r, tri)


def _token_copy(src, src_row, dst, dst_row, sem):
    return pltpu.make_async_copy(src.at[pl.ds(pl.multiple_of(src_row * ROW_TILES, ROW_TILES), ROW_TILES), :],
                                 dst.at[pl.ds(pl.multiple_of(dst_row * ROW_TILES, ROW_TILES), ROW_TILES), :], sem)


def _dispatch_body(ends_ref, padded_ref, dest_ref, h_ref, xs_ref, zbuf, sem, zsem):
    tm = h_ref.shape[0] // ROW_TILES

    @pl.when(pl.program_id(0) == 0)
    def _():
        zbuf[...] = jnp.zeros_like(zbuf)

        def window(e):
            first = pl.multiple_of((ends_ref[e] - TM_FFN) * ROW_TILES, ROW_TILES)
            return pltpu.make_async_copy(zbuf, xs_ref.at[pl.ds(first, TM_FFN * ROW_TILES), :], zsem)

        for e in range(N_EXPERTS):
            @pl.when(padded_ref[e] > 0)
            def _(e=e):
                window(e).start()
        def tile(t):
            return pltpu.make_async_copy(
                zbuf, xs_ref.at[pl.ds(pl.multiple_of(t * TM_FFN * ROW_TILES, ROW_TILES), TM_FFN * ROW_TILES), :], zsem)

        def start_tile(t, carry):
            tile(t).start()
            return carry

        def wait_tile(t, carry):
            tile(t).wait()
            return carry

        used = ends_ref[N_EXPERTS - 1] // TM_FFN
        lax.fori_loop(used, xs_ref.shape[0] // (TM_FFN * ROW_TILES), start_tile, 0)
        for e in range(N_EXPERTS):
            @pl.when(padded_ref[e] > 0)
            def _(e=e):
                window(e).wait()
        lax.fori_loop(used, xs_ref.shape[0] // (TM_FFN * ROW_TILES), wait_tile, 0)

    def issue(r, carry):
        for kk in range(TOP_K):
            _token_copy(h_ref, r, xs_ref, dest_ref[0, 0, r * TOP_K + kk], sem).start()
        return carry

    lax.fori_loop(0, tm, issue, 0)

    for kk in range(TOP_K):
        pltpu.make_async_copy(h_ref, xs_ref.at[pl.ds(0, tm * ROW_TILES), :], sem).wait()


def _dispatch(ends, padded, dest3, h2, n_rows):
    n_tiles, _, per = dest3.shape
    tm = per // TOP_K
    grid_spec = pltpu.PrefetchScalarGridSpec(
        num_scalar_prefetch=2,
        grid=(n_tiles,),
        in_specs=[pl.BlockSpec((1, 1, per), lambda i, en, pa: (i, 0, 0), memory_space=pltpu.SMEM),
                  pl.BlockSpec((tm * ROW_TILES, LANES), lambda i, en, pa: (i, 0))],
        out_specs=pl.BlockSpec(memory_space=pl.ANY),
        scratch_shapes=[pltpu.VMEM((TM_FFN * ROW_TILES, LANES), F32),
                        pltpu.SemaphoreType.DMA(()), pltpu.SemaphoreType.DMA(())],
    )
    return pl.pallas_call(
        _dispatch_body,
        grid_spec=grid_spec,
        out_shape=jax.ShapeDtypeStruct((n_rows * ROW_TILES, LANES), F32),
        compiler_params=_cparams(("arbitrary",)),
        name="dispatch",
    )(ends, padded, dest3, h2)


def _experts_body(te_ref, na_ref, xs_ref, wgu_ref, bgu_ref, wdn_ref, bdn_ref, ys_ref, wgu_bf, wdn_bf):
    i = pl.program_id(0)
    tm = xs_ref.shape[0] // ROW_TILES
    active = i < na_ref[0]

    @pl.when(active)
    def _():
        prev = te_ref[jnp.maximum(i - 1, 0)]

        @pl.when((i == 0) | (te_ref[i] != prev))
        def _():
            wgu_bf[...] = wgu_ref[0, 0].astype(BF16)
            wdn_bf[...] = wdn_ref[0, 0].astype(BF16)

        x = _load_rows(xs_ref, tm).astype(BF16)
        gu = _dot(x, wgu_bf[...]) + bgu_ref[0, 0]
        gate = jnp.minimum(gu[:, :D_FF], SWIGLU_LIMIT)
        up = jnp.clip(gu[:, D_FF:], -SWIGLU_LIMIT, SWIGLU_LIMIT)
        act = (up + 1.0) * gate * jax.nn.sigmoid(SWIGLU_ALPHA * gate)
        y = _dot(act.astype(BF16), wdn_bf[...]) + bdn_ref[0, 0]
        _store_rows(ys_ref, y, tm)

    @pl.when(jnp.logical_not(active))
    def _():
        ys_ref[...] = jnp.zeros_like(ys_ref)


def _experts(layer, tile_expert, n_active, xs, w_gu, b_gu, w_dn, b_dn):
    n_rows = xs.shape[0] // ROW_TILES
    tm = TM_FFN
    b_gu4 = b_gu.reshape(b_gu.shape[0], N_EXPERTS, 1, 2 * D_FF)
    b_dn4 = b_dn.reshape(b_dn.shape[0], N_EXPERTS, 1, D_MODEL)
    grid_spec = pltpu.PrefetchScalarGridSpec(
        num_scalar_prefetch=2,
        grid=(n_rows // tm,),
        in_specs=[pl.BlockSpec((tm * ROW_TILES, LANES), lambda i, te, na: (jnp.minimum(i, na[0] - 1), 0)),
                  pl.BlockSpec((1, 1, D_MODEL, 2 * D_FF), lambda i, te, na: (layer, te[i], 0, 0)),
                  pl.BlockSpec((1, 1, 1, 2 * D_FF), lambda i, te, na: (layer, te[i], 0, 0)),
                  pl.BlockSpec((1, 1, D_FF, D_MODEL), lambda i, te, na: (layer, te[i], 0, 0)),
                  pl.BlockSpec((1, 1, 1, D_MODEL), lambda i, te, na: (layer, te[i], 0, 0))],
        out_specs=pl.BlockSpec((tm * ROW_TILES, LANES), lambda i, te, na: (i, 0)),
        scratch_shapes=[pltpu.VMEM((D_MODEL, 2 * D_FF), BF16), pltpu.VMEM((D_FF, D_MODEL), BF16)],
    )
    return pl.pallas_call(
        _experts_body,
        grid_spec=grid_spec,
        out_shape=jax.ShapeDtypeStruct(xs.shape, F32),
        compiler_params=_cparams(("arbitrary",)),
        name="experts",
    )(tile_expert, n_active, xs, w_gu, b_gu4, w_dn, b_dn4)


def _combine_body(dest_ref, wgt_ref, x_ref, m_ref, g_ref, ys_ref, xo_ref, buf, sem):
    tm = x_ref.shape[0]

    def issue(r, carry):
        for kk in range(TOP_K):
            _token_copy(ys_ref, dest_ref[0, 0, r * TOP_K + kk], buf.at[kk], r, sem).start()
        return carry

    lax.fori_loop(0, tm, issue, 0)

    for kk in range(TOP_K):
        pltpu.make_async_copy(ys_ref.at[pl.ds(0, tm * ROW_TILES), :], buf.at[kk], sem).wait()
    w = wgt_ref[...]
    f = jnp.zeros((tm, D_MODEL), F32)
    for kk in range(TOP_K):
        f = f + w[:, kk:kk + 1] * _load_rows(buf.at[kk], tm)
    xo_ref[...] = x_ref[...] + m_ref[5, 0] * _rms(f, g_ref[...])


def _combine(dest3, wgt, x, mods, gpost, ys, tiles_per_seq):
    t = x.shape[0]
    tm = TM_COMB
    if mods.shape[2] == 1:
        mspec = pl.BlockSpec((6, 1, 1, D_MODEL), lambda i: (0, i // tiles_per_seq, 0, 0))
    else:
        mspec = pl.BlockSpec((6, 1, tm, D_MODEL), lambda i: (0, 0, i, 0))
    return pl.pallas_call(
        _combine_body,
        grid=(t // tm,),
        in_specs=[pl.BlockSpec((1, 1, tm * TOP_K), lambda i: (i, 0, 0), memory_space=pltpu.SMEM),
                  pl.BlockSpec((tm, LANES), lambda i: (i, 0)),
                  pl.BlockSpec((tm, D_MODEL), lambda i: (i, 0)),
                  mspec,
                  pl.BlockSpec((1, D_MODEL), lambda i: (0, 0)),
                  pl.BlockSpec(memory_space=pl.ANY)],
        out_specs=pl.BlockSpec((tm, D_MODEL), lambda i: (i, 0)),
        out_shape=jax.ShapeDtypeStruct((t, D_MODEL), F32),
        scratch_shapes=[pltpu.VMEM((TOP_K, tm * ROW_TILES, LANES), F32), pltpu.SemaphoreType.DMA(())],
        compiler_params=_cparams(("arbitrary",)),
        name="combine",
    )(dest3, wgt, x, mods, gpost, ys)


def _moe(layer, h2_all, t_prompt, x_p, x_s, mods_p, mods_s, gpost, w_r, b_r, tri_strict,
         w_gu, b_gu, w_dn, b_dn, tiles_per_seq_comb):
    t_all = h2_all.shape[0] // ROW_TILES
    idx_w, wgt_w, rank_w, cnt_w = _router(h2_all, w_r, b_r, tri_strict)
    idx = idx_w[:, :TOP_K]
    rank = rank_w[:, :TOP_K]
    cnt = cnt_w[0, :N_EXPERTS].astype(I32)
    padded = ((cnt + TM_FFN - 1) // TM_FFN) * TM_FFN
    ends = jnp.cumsum(padded)
    offs = ends - padded
    dest = jnp.sum(jnp.where(idx[..., None] == jnp.arange(N_EXPERTS, dtype=I32), offs, 0), axis=-1) + rank
    n_tiles = (t_all * TOP_K) // TM_FFN + N_EXPERTS
    tile_start = jnp.arange(n_tiles, dtype=I32) * TM_FFN
    tile_expert = jnp.minimum(jnp.sum((tile_start[:, None] >= ends[None, :]).astype(I32), axis=1), N_EXPERTS - 1)
    n_active = (ends[-1] // TM_FFN).astype(I32).reshape(1)

    dest_flat = dest.reshape(-1)
    xs = _dispatch(ends.astype(I32), padded, dest_flat.reshape(t_all // TM, 1, TM * TOP_K), h2_all, n_tiles * TM_FFN)
    ys = _experts(layer, tile_expert, n_active, xs, w_gu, b_gu, w_dn, b_dn)

    dest_p = dest_flat[:t_prompt * TOP_K].reshape(t_prompt // TM_COMB, 1, TM_COMB * TOP_K)
    dest_s = dest_flat[t_prompt * TOP_K:].reshape(-1, 1, TM_COMB * TOP_K)
    xo_p = _combine(dest_p, wgt_w[:t_prompt], x_p, mods_p, gpost, ys, tiles_per_seq_comb)
    xo_s = _combine(dest_s, wgt_w[t_prompt:], x_s, mods_s, gpost, ys, 1)
    return xo_p, xo_s


def _fox_prompt_body(qa_ref, ka_ref, vat_ref, g_ref, o_ref, m_s, acc_s):
    i = pl.program_id(2)
    tq = qa_ref.shape[0]
    m_s[...] = jnp.full_like(m_s, -jnp.inf)
    acc_s[...] = jnp.zeros_like(acc_s)

    def block(start, diagonal):
        if diagonal:
            causal = lax.broadcasted_iota(I32, (tq, tq), 0) <= lax.broadcasted_iota(I32, (tq, tq), 1)
        for a in range(2):
            cols = slice(a * LANES, (a + 1) * LANES)
            st = _dot_nt(ka_ref[pl.ds(start, tq), cols], qa_ref[:, cols])
            if diagonal:
                st = jnp.where(causal, st, -jnp.inf)
            m_old = m_s[a]
            m_new = jnp.maximum(m_old, jnp.max(st, axis=0, keepdims=True))
            p = jnp.exp((st - m_new).astype(BF16))
            acc_s[a] = jnp.exp(m_old - m_new) * acc_s[a] + _dot(vat_ref[0, cols, pl.ds(start, tq)], p)
            m_s[a] = m_new

    def body(j, carry):
        block(pl.multiple_of(j * tq, tq), False)
        return carry

    lax.fori_loop(0, i, body, 0)
    block(pl.multiple_of(i * tq, tq), True)
    o_t = jnp.concatenate([acc_s[a][:FOX_HD, :] / acc_s[a][FOX_HD:FOX_HD + 1, :] for a in range(2)], axis=0)
    o_ref[...] = o_t.T * jax.nn.sigmoid(g_ref[...])


def _fox_prompt_attn(qa, ka, vat, gate, n_seq, seq):
    t = gate.shape[0]
    nq = seq // TQ
    pairs = FOX_HEADS // 2
    return pl.pallas_call(
        _fox_prompt_body,
        grid=(n_seq, pairs, nq),
        in_specs=[pl.BlockSpec((TQ, 2 * LANES), lambda b, hp, i: (b * nq + i, hp)),
                  pl.BlockSpec((seq, 2 * LANES), lambda b, hp, i: (b, hp)),
                  pl.BlockSpec((1, 2 * LANES, seq), lambda b, hp, i: (b, hp, 0)),
                  pl.BlockSpec((TQ, LANES), lambda b, hp, i: (b * nq + i, hp))],
        out_specs=pl.BlockSpec((TQ, LANES), lambda b, hp, i: (b * nq + i, hp)),
        out_shape=jax.ShapeDtypeStruct((t, FOX_HEADS * FOX_HD), F32),
        scratch_shapes=[pltpu.VMEM((2, 1, TQ), F32), pltpu.VMEM((2, LANES, TQ), F32)],
        compiler_params=_cparams(("arbitrary", "arbitrary", "arbitrary")),
        name="fox_prompt_attn",
    )(qa, ka, vat, gate)


def _fox_sample_body(pt_ref, q_ref, kn_ref, vn_ref, g_ref, cncol_ref, cnrow_ref, *rest):
    npg = PAGES_PER_STEP
    k_refs = rest[:npg]
    v_refs = rest[npg:2 * npg]
    f_refs = rest[2 * npg:3 * npg]
    u_ref, e_ref, o_ref, qbd_s, m_s, l_s, acc_s, car_s = rest[3 * npg:]
    del pt_ref
    j = pl.program_id(1)
    nj = pl.num_programs(1)
    rows = FOX_HEADS * TOK_PAD
    wide = FOX_HEADS * FOX_HD
    own = ((lax.broadcasted_iota(I32, (rows, wide), 0) // TOK_PAD)
           == (lax.broadcasted_iota(I32, (rows, wide), 1) // FOX_HD))

    @pl.when(j == 0)
    def _():
        m_s[...] = jnp.full_like(m_s, -jnp.inf)
        l_s[...] = jnp.zeros_like(l_s)
        acc_s[...] = jnp.zeros_like(acc_s)
        car_s[...] = jnp.zeros_like(car_s)
        q = q_ref[...] * (FOX_HD ** -0.5)
        qbd_s[...] = jnp.where(own, jnp.concatenate([q] * FOX_HEADS, axis=0), 0.0).astype(BF16)

    qbd = qbd_s[...]
    cn = cncol_ref[0]

    def update(s, pv_of):
        m_old = m_s[...]
        m_new = jnp.maximum(m_old, jnp.max(s, axis=1, keepdims=True))
        alpha = jnp.exp(m_old - m_new)
        p = jnp.exp(s - m_new)
        l_s[...] = alpha * l_s[...] + jnp.sum(p, axis=1, keepdims=True)
        acc_s[...] = alpha * acc_s[...] + pv_of(p.astype(BF16))
        m_s[...] = m_new

    suffixes = []
    for r in range(npg):
        ft = f_refs[r][0]
        suffixes.append(_dot(ft, u_ref[...], HI) + car_s[...])
        car_s[...] = car_s[...] + jnp.sum(ft, axis=1, keepdims=True)
    bias = _dot(e_ref[...], jnp.concatenate(suffixes, axis=1), HI) + cn
    kt = jnp.concatenate([k_refs[r][0].astype(BF16) for r in range(npg)], axis=1)
    vt = jnp.concatenate([v_refs[r][0].astype(BF16) for r in range(npg)], axis=1)
    update(_dot(qbd, kt) + bias, lambda pb: _dot_nt(pb, vt))

    @pl.when(j == nj - 1)
    def _():
        zpad = jnp.zeros((PAGE_SIZE - TOK_PAD, wide), F32)
        kn = jnp.concatenate([kn_ref[...], zpad], axis=0).astype(BF16)
        vn = jnp.concatenate([vn_ref[...], zpad], axis=0).astype(BF16)
        trow = lax.broadcasted_iota(I32, (rows, PAGE_SIZE), 0) & (TOK_PAD - 1)
        tcol = lax.broadcasted_iota(I32, (rows, PAGE_SIZE), 1)
        s = jnp.where(tcol <= trow, _dot_nt(qbd, kn) + cn - cnrow_ref[0], -jnp.inf)
        update(s, lambda pb: _dot(pb, vn))
        o = jnp.where(own, acc_s[...] / l_s[...], 0.0)
        o8 = o[0:TOK_PAD, :]
        for h in range(1, FOX_HEADS):
            o8 = o8 + o[h * TOK_PAD:(h + 1) * TOK_PAD, :]
        o_ref[...] = o8 * jax.nn.sigmoid(g_ref[...])


def _fox_sample_attn(proj, cncol, cnrow, cache_k, cache_v, cache_ft, page_table, u_mat, e_mat):
    n_seq, n_pages = page_table.shape
    npg = PAGES_PER_STEP
    steps = n_pages // npg
    wide = FOX_HEADS * FOX_HD
    rows = FOX_HEADS * TOK_PAD

    def page(r):
        return lambda b, j, pt: pt[b * n_pages + (n_pages - 1 - (j * npg + r))]

    kv_specs = [pl.BlockSpec((1, wide, PAGE_SIZE), (lambda b, j, pt, r=r: (page(r)(b, j, pt), 0, 0)))
                for r in range(npg)]
    f_specs = [pl.BlockSpec((1, FOX_HEADS, PAGE_SIZE), (lambda b, j, pt, r=r: (page(r)(b, j, pt), 0, 0)))
               for r in range(npg)]
    grid_spec = pltpu.PrefetchScalarGridSpec(
        num_scalar_prefetch=1,
        grid=(n_seq, steps),
        in_specs=[pl.BlockSpec((TOK_PAD, wide), lambda b, j, pt: (b, 0)),
                  pl.BlockSpec((TOK_PAD, wide), lambda b, j, pt: (b, 1)),
                  pl.BlockSpec((TOK_PAD, wide), lambda b, j, pt: (b, 2)),
                  pl.BlockSpec((TOK_PAD, wide), lambda b, j, pt: (b, 3)),
                  pl.BlockSpec((1, rows, 1), lambda b, j, pt: (b, 0, 0)),
                  pl.BlockSpec((1, rows, PAGE_SIZE), lambda b, j, pt: (b, 0, 0))]
                 + kv_specs + kv_specs + f_specs
                 + [pl.BlockSpec((PAGE_SIZE, PAGE_SIZE), lambda b, j, pt: (0, 0)),
                    pl.BlockSpec((rows, FOX_HEADS), lambda b, j, pt: (0, 0))],
        out_specs=pl.BlockSpec((TOK_PAD, wide), lambda b, j, pt: (b, 0)),
        scratch_shapes=[pltpu.VMEM((rows, wide), BF16), pltpu.VMEM((rows, 1), F32), pltpu.VMEM((rows, 1), F32),
                        pltpu.VMEM((rows, wide), F32), pltpu.VMEM((FOX_HEADS, 1), F32)],
    )
    return pl.pallas_call(
        _fox_sample_body,
        grid_spec=grid_spec,
        out_shape=jax.ShapeDtypeStruct((n_seq * TOK_PAD, wide), F32),
        compiler_params=_cparams(("arbitrary", "arbitrary")),
        name="fox_sample_attn",
    )(page_table.reshape(-1), proj, proj, proj, proj, cncol, cnrow,
      *([cache_k] * npg), *([cache_v] * npg), *([cache_ft] * npg), u_mat, e_mat)


def _rope_tables(pos):
    half = RET_DK // 2
    inv = ROPE_BASE ** (-jnp.arange(half, dtype=F32) / half)
    ang = pos.astype(F32)[:, None] * inv[None, :]
    return jnp.cos(ang), jnp.sin(ang)


def _mods_prompt(m):
    b = m.shape[0]
    return m.reshape(b, 6, D_MODEL).transpose(1, 0, 2)[:, :, None, :]


def _mods_sample(m):
    b = m.shape[0]
    mm = jnp.repeat(m.reshape(b, 6, D_MODEL), TOK_PAD, axis=0)
    return mm.transpose(1, 0, 2)[:, None, :, :]


def kernel(x_prompt, x_sample, c_prompt, c_sample, state_ret, cache_fox_k, cache_fox_v, cache_fox_logf, page_table, w_ada, b_ada, norm_pre_mix, norm_post_mix, norm_pre_ffn, norm_post_ffn, ret_w_in, ret_gn_gain, ret_w_out, fox_w_in, fox_b_f, fox_w_out, moe_w_router, moe_b_router, moe_w_gate_up, moe_b_gate_up, moe_w_down, moe_b_down):
    n_seq, seq, d = x_prompt.shape
    n_dec, dec_seq, _ = x_sample.shape
    t_p = n_seq * seq
    t_s = n_dec * TOK_PAD
    tps = seq // TM

    xp = x_prompt.reshape(t_p, d)
    xs = jnp.pad(x_sample, ((0, 0), (0, TOK_PAD - dec_seq), (0, 0))).reshape(t_s, d)

    m_all = _adaln(jnp.concatenate([c_prompt, c_sample], axis=0), w_ada, b_ada)

    ar = jnp.arange(TM)
    tri_incl = (ar[None, :] <= ar[:, None]).astype(F32)
    tri_strict = (ar[None, :] < ar[:, None]).astype(BF16)
    tri_group = ((ar[None, :] <= ar[:, None]) & (ar[None, :] // TOK_PAD == ar[:, None] // TOK_PAD)).astype(F32)

    mods_p = _mods_prompt(m_all[0, :n_seq])
    mods_s = _mods_sample(m_all[0, n_seq:])
    w_in = ret_w_in[0].astype(BF16)
    cos_p, sin_p = _rope_tables(jnp.arange(seq))
    pos_s = jnp.tile(PAST_LEN + jnp.arange(TOK_PAD), n_dec)
    cos_s, sin_s = _rope_tables(pos_s)
    g_pre = norm_pre_mix[0:1]
    proj_p = _inproj_ret(xp, mods_p, g_pre, w_in, cos_p, sin_p, tps)
    proj_s = _inproj_ret(xs, mods_s, g_pre, ret_w_in[0], cos_s, sin_s, 1)

    n_chunks = seq // RET_CHUNK
    din, dq, dk, dc = _ret_decay_tables(RET_CHUNK, RET_CHUNK, RET_CHUNK)
    s0p = jnp.zeros((n_seq, RET_HEADS, RET_DK, RET_DV), F32)
    u_p, st_p = _retention(proj_p, s0p, din, dq, dk, dc, ret_gn_gain, n_seq, n_chunks, RET_CHUNK, False)
    din, dq, dk, dc = _ret_decay_tables(dec_seq, TOK_PAD, RET_CHUNK)
    u_s, st_s = _retention(proj_s, state_ret[0], din, dq, dk, dc, ret_gn_gain, n_dec, 1, TOK_PAD, True)

    w_out = ret_w_out[0].astype(BF16)
    xp, h2p = _outproj(u_p, w_out, xp, mods_p, norm_post_mix[0:1], norm_pre_ffn[0:1], tps)
    xs, h2s = _outproj(u_s, ret_w_out[0], xs, mods_s, norm_post_mix[0:1], norm_pre_ffn[0:1], 1)

    def router_params(i):
        w_r = jnp.pad(moe_w_router[i], ((0, 0), (0, LANES - N_EXPERTS)))
        b_r = jnp.pad(moe_b_router[i], (0, LANES - N_EXPERTS), constant_values=-1e30)[None, :]
        return w_r, b_r

    w_r, b_r = router_params(0)
    xp, xs = _moe(0, jnp.concatenate([h2p, h2s], axis=0), t_p, xp, xs, mods_p, mods_s, norm_post_ffn[0:1],
                  w_r, b_r, tri_strict, moe_w_gate_up, moe_b_gate_up, moe_w_down, moe_b_down, seq // TM_COMB)

    mods_p = _mods_prompt(m_all[1, :n_seq])
    mods_s = _mods_sample(m_all[1, n_seq:])
    fw = FOX_HEADS * FOX_HD
    w_in = fox_w_in[0, :, :4 * fw].astype(BF16)
    w_fl = jnp.pad(fox_w_in[0, :, 4 * fw:], ((0, 0), (0, LANES - FOX_HEADS)))
    b_fl = jnp.pad(fox_b_f[0], (0, LANES - FOX_HEADS))[None, :]
    g_pre = norm_pre_mix[1:2]
    wkt = fox_w_in[0, :, fw:2 * fw].T.astype(BF16)
    wvt = fox_w_in[0, :, 2 * fw:3 * fw].T.astype(BF16)
    gate_p, lf_p, qa, ka, vat, kt_p, vt_p = _inproj_fox(xp, mods_p, g_pre, w_in, w_fl, b_fl, tri_incl, tps, wkt, wvt)
    fproj_s, lf_s, cum_s = _inproj_fox(xs, mods_s, g_pre, w_in, w_fl, b_fl, tri_group, 1)
    o_p = _fox_prompt_attn(qa, ka, vat, gate_p, n_seq, seq)

    cn = cum_s[:, :FOX_HEADS].reshape(n_dec, TOK_PAD, FOX_HEADS).transpose(0, 2, 1)
    cncol = cn.reshape(n_dec, FOX_HEADS * TOK_PAD, 1)
    cnrow = jnp.repeat(cn, TOK_PAD, axis=1)
    cnrow = jnp.pad(cnrow, ((0, 0), (0, 0), (0, PAGE_SIZE - TOK_PAD)))
    pr = jnp.arange(PAGE_SIZE)
    u_mat = (pr[:, None] > pr[None, :]).astype(F32)
    e_mat = (jnp.arange(FOX_HEADS * TOK_PAD)[:, None] // TOK_PAD == jnp.arange(FOX_HEADS)[None, :]).astype(F32)
    n_pool = cache_fox_k.shape[1]
    cache_kt = cache_fox_k[0].transpose(0, 2, 3, 1).reshape(n_pool, fw, PAGE_SIZE)
    cache_vt = cache_fox_v[0].transpose(0, 2, 3, 1).reshape(n_pool, fw, PAGE_SIZE)
    cache_ft = cache_fox_logf[0].transpose(0, 2, 1)
    o_s = _fox_sample_attn(fproj_s, cncol, cnrow, cache_kt, cache_vt, cache_ft, page_table, u_mat, e_mat)

    w_out = fox_w_out[0].astype(BF16)
    xp, h2p = _outproj(o_p, w_out, xp, mods_p, norm_post_mix[1:2], norm_pre_ffn[1:2], tps)
    xs, h2s = _outproj(o_s, w_out, xs, mods_s, norm_post_mix[1:2], norm_pre_ffn[1:2], 1)
    w_r, b_r = router_params(1)
    xp, xs = _moe(1, jnp.concatenate([h2p, h2s], axis=0), t_p, xp, xs, mods_p, mods_s, norm_post_ffn[1:2],
                  w_r, b_r, tri_strict, moe_w_gate_up, moe_b_gate_up, moe_w_down, moe_b_down, seq // TM_COMB)

    y_prompt = xp.reshape(n_seq, seq, d)
    y_sample = xs.reshape(n_dec, TOK_PAD, d)[:, :dec_seq]
    k_prompt = kt_p.reshape(n_seq, FOX_HEADS, FOX_HD, seq).transpose(0, 3, 1, 2)[None]
    v_prompt = vt_p.reshape(n_seq, FOX_HEADS, FOX_HD, seq).transpose(0, 3, 1, 2)[None]
    logf_prompt = lf_p[:, :FOX_HEADS].reshape(1, n_seq, seq, FOX_HEADS)
    fs = fproj_s.reshape(n_dec, TOK_PAD, 4 * fw)[:, :dec_seq]
    kv_shape_s = (1, n_dec, dec_seq, FOX_HEADS, FOX_HD)
    k_sample = fs[..., fw:2 * fw].reshape(kv_shape_s)
    v_sample = fs[..., 2 * fw:3 * fw].reshape(kv_shape_s)
    logf_sample = lf_s.reshape(n_dec, TOK_PAD, LANES)[:, :dec_seq, :FOX_HEADS][None]
    return (y_prompt, y_sample, st_p[None], st_s[None], k_prompt, v_prompt, logf_prompt,
            k_sample, v_sample, logf_sample)
```
